```python
import math
import jax, jax.numpy as jnp
from jax import lax
import numpy as np

D_MODEL = 1024
BATCH = 2
SEQ = 16384
DEPTH = 4

N_MIXERS = 4
HEAD_DIM = 64
Q_BLOCK = 128
D_FF = 2816
EPS = 1e-6
NEG = -1e30
FORCE = 1e9

NSA_HQ = 16
NSA_HKV = 4
CMP_BLOCK = 32
CMP_STRIDE = 16
CMP_HIDDEN = 256
SEL_BLOCK = 64
SEL_TOPK = 16
NSA_WINDOW = 512
NSA_IN = NSA_HQ * HEAD_DIM + 6 * NSA_HKV * HEAD_DIM + 3 * NSA_HQ

DIFF_H = 8
DIFF_IN = 3 * DIFF_H * 2 * HEAD_DIM

GDN_H = 8
GDN_DK = 128
GDN_DV = 128
GDN_CONV = 4
GDN_CHUNK = 64
GDN_QKV = GDN_H * (2 * GDN_DK + GDN_DV)
GDN_IN = GDN_QKV + 2 * GDN_H + GDN_H * GDN_DV

SWA_HQ = 16
SWA_HKV = 4
SWA_WINDOW = 128
SWA_IN = (SWA_HQ + 2 * SWA_HKV) * HEAD_DIM

kernel_name = "hybrid_nsa_diff_gdn_swa_macaron"

f32 = jnp.float32


def rmsnorm(x, g):
    xf = x.astype(f32)
    y = xf * lax.rsqrt(jnp.mean(xf * xf, axis=-1, keepdims=True) + EPS)
    return (y * g.astype(f32)).astype(x.dtype)


def l2norm(x):
    xf = x.astype(f32)
    return xf * lax.rsqrt(jnp.sum(xf * xf, axis=-1, keepdims=True) + EPS)


def swiglu(x, w_in, w_out):
    a, b = jnp.split(x @ w_in, 2, axis=-1)
    return (jax.nn.silu(a) * b) @ w_out


def alibi_slopes(n):
    return 2.0 ** (-8.0 * jnp.arange(1, n + 1, dtype=f32) / n)


def masked_softmax(s, mask):
    return jax.nn.softmax(jnp.where(mask, s, NEG), axis=-1) * mask


def banded_attention(q, k, v, slopes, window, sinks=None):
    B, Hkv, G, S, dh = q.shape
    nb = S // Q_BLOCK
    span = window + Q_BLOCK
    pad = ((0, 0), (0, 0), (window, 0), (0, 0))
    kp, vp = jnp.pad(k, pad), jnp.pad(v, pad)
    qb = q.reshape(B, Hkv, G, nb, Q_BLOCK, dh).transpose(3, 0, 1, 2, 4, 5)
    scale = dh ** -0.5

    def block(args):
        qi, i = args
        start = i * Q_BLOCK
        ki = lax.dynamic_slice_in_dim(kp, start, span, axis=2)
        vi = lax.dynamic_slice_in_dim(vp, start, span, axis=2)
        t = start + jnp.arange(Q_BLOCK)
        s_pos = start - window + jnp.arange(span)
        dist = t[:, None] - s_pos[None, :]
        mask = (dist >= 0) & (dist < window) & (s_pos[None, :] >= 0)
        sc = jnp.einsum('bhgtd,bhsd->bhgts', qi, ki).astype(f32) * scale \
            - slopes[:, :, None, None] * dist
        if sinks is None:
            p = masked_softmax(sc, mask)
        else:
            sink = jnp.broadcast_to(sinks.astype(f32)[:, :, None, None], sc.shape[:-1] + (1,))
            p = jax.nn.softmax(jnp.concatenate([jnp.where(mask, sc, NEG), sink], axis=-1), axis=-1)[..., :-1]
        return jnp.einsum('bhgts,bhsd->bhgtd', p.astype(v.dtype), vi)

    o = lax.map(block, (qb, jnp.arange(nb)))
    return o.transpose(1, 2, 3, 0, 4, 5).reshape(B, Hkv, G, S, dh)


def compress_blocks(t, pos, w1, w2):
    B, H, S, dh = t.shape
    ratio = CMP_BLOCK // CMP_STRIDE
    n_sub = S // CMP_STRIDE
    n_cmp = n_sub - ratio + 1
    sub = t.reshape(B, H, n_sub, CMP_STRIDE, dh)
    blocks = jnp.concatenate([sub[:, :, r:r + n_cmp] for r in range(ratio)], axis=3)
    blocks = (blocks + pos).reshape(B, H, n_cmp, CMP_BLOCK * dh)
    return jax.nn.gelu(blocks @ w1) @ w2


def nsa_compressed_selected(q, kc, vc, ks, vs, slopes):
    B, Hkv, G, S, dh = q.shape
    nb = S // Q_BLOCK
    n_cmp = kc.shape[2]
    n_sel = S // SEL_BLOCK
    n_top = min(SEL_TOPK, n_sel)
    scale = dh ** -0.5
    cmp_start = jnp.arange(n_cmp) * CMP_STRIDE
    cmp_end = cmp_start + CMP_BLOCK - 1
    sel_start = jnp.arange(n_sel) * SEL_BLOCK
    overlap = ((cmp_start[:, None] < sel_start[None, :] + SEL_BLOCK)
               & (cmp_start[:, None] + CMP_BLOCK > sel_start[None, :])).astype(f32)
    ksb = ks.reshape(B, Hkv, n_sel, SEL_BLOCK, dh)
    vsb = vs.reshape(B, Hkv, n_sel, SEL_BLOCK, dh)
    gather = jax.vmap(jax.vmap(lambda tb, idx: tb[idx]))
    qb = q.reshape(B, Hkv, G, nb, Q_BLOCK, dh).transpose(3, 0, 1, 2, 4, 5)
    jsel = jnp.arange(n_sel)

    def block(args):
        qi, i = args
        t = i * Q_BLOCK + jnp.arange(Q_BLOCK)
        dist_c = t[:, None] - cmp_end[None, :]
        sc = jnp.einsum('bhgtd,bhnd->bhgtn', qi, kc).astype(f32) * scale \
            - slopes[:, :, None, None] * dist_c
        p_c = masked_softmax(sc, dist_c >= 0)
        o_cmp = jnp.einsum('bhgtn,bhnd->bhgtd', p_c.astype(vc.dtype), vc)
        imp = jnp.einsum('bhgtn,ns->bhts', p_c, overlap)
        cur = t // SEL_BLOCK
        forced = (jsel[None, :] == 0) | (jsel[None, :] == cur[:, None]) | (jsel[None, :] == cur[:, None] - 1)
        causal = jsel[None, :] <= cur[:, None]
        imp = jnp.where(forced, FORCE, jnp.where(causal, imp, -jnp.inf))
        _, idx = lax.top_k(imp, n_top)
        kg = gather(ksb, idx).reshape(B, Hkv, Q_BLOCK, n_top * SEL_BLOCK, dh)
        vg = gather(vsb, idx).reshape(B, Hkv, Q_BLOCK, n_top * SEL_BLOCK, dh)
        kpos = (idx[..., None] * SEL_BLOCK + jnp.arange(SEL_BLOCK)).reshape(B, Hkv, Q_BLOCK, n_top * SEL_BLOCK)
        dist_s = (t[None, None, :, None] - kpos)[:, :, None]
        ss = jnp.einsum('bhgtd,bhtkd->bhgtk', qi, kg).astype(f32) * scale \
            - slopes[:, :, None, None] * dist_s
        p_s = masked_softmax(ss, dist_s >= 0)
        o_sel = jnp.einsum('bhgtk,bhtkd->bhgtd', p_s.astype(vs.dtype), vg)
        return o_cmp, o_sel

    o_cmp, o_sel = lax.map(block, (qb, jnp.arange(nb)))
    back = lambda o: o.transpose(1, 2, 3, 0, 4, 5).reshape(B, Hkv, G, S, dh)
    return back(o_cmp), back(o_sel)


def nsa_mixer(h, w_in, w_out, q_norm, k_norm, cmp_pos, cmp_w1, cmp_w2):
    B, S, _ = h.shape
    G = NSA_HQ // NSA_HKV
    dh = HEAD_DIM
    splits = np.cumsum([NSA_HQ * dh] + [NSA_HKV * dh] * 6).tolist()
    q, kc, vc, ks, vs, kw, vw, gates = jnp.split(h @ w_in, splits, axis=-1)
    q = rmsnorm(q.reshape(B, S, NSA_HKV, G, dh), q_norm).transpose(0, 2, 3, 1, 4)
    heads = lambda t: t.reshape(B, S, NSA_HKV, dh).transpose(0, 2, 1, 3)
    kc = rmsnorm(compress_blocks(heads(kc), cmp_pos[0], cmp_w1[0], cmp_w2[0]), k_norm[0])
    vc = compress_blocks(heads(vc), cmp_pos[1], cmp_w1[1], cmp_w2[1])
    ks, vs = rmsnorm(heads(ks), k_norm[1]), heads(vs)
    kw, vw = rmsnorm(heads(kw), k_norm[2]), heads(vw)
    slopes = alibi_slopes(NSA_HQ).reshape(NSA_HKV, G)
    o_cmp, o_sel = nsa_compressed_selected(q, kc, vc, ks, vs, slopes)
    o_win = banded_attention(q, kw, vw, slopes, NSA_WINDOW)
    g = jax.nn.sigmoid(gates.reshape(B, S, NSA_HKV, G, 3).transpose(0, 2, 3, 1, 4))
    o = g[..., 0:1] * o_cmp + g[..., 1:2] * o_sel + g[..., 2:3] * o_win
    return o.transpose(0, 3, 1, 2, 4).reshape(B, S, NSA_HQ * dh) @ w_out


def dense_diff_attention(q, k, v, slopes, lmbda):
    B, H, _, S, dh = q.shape
    nb = S // Q_BLOCK
    scale = dh ** -0.5
    qb = q.reshape(B, H, 2, nb, Q_BLOCK, dh).transpose(3, 0, 1, 2, 4, 5)
    s_pos = jnp.arange(S)

    def block(args):
        qi, i = args
        t = i * Q_BLOCK + jnp.arange(Q_BLOCK)
        dist = t[:, None] - s_pos[None, :]
        sc = jnp.einsum('bhctd,bhcsd->bhcts', qi, k).astype(f32) * scale \
            - slopes[:, None, None, None] * dist
        p = masked_softmax(sc, dist >= 0)
        a = p[:, :, 0] - lmbda * p[:, :, 1]
        return jnp.einsum('bhts,bhsd->bhtd', a.astype(v.dtype), v)

    o = lax.map(block, (qb, jnp.arange(nb)))
    return o.transpose(1, 2, 0, 3, 4).reshape(B, H, S, 2 * dh)


def diff_mixer(h, w_in, w_out, q_norm, k_norm, lam, subln, lam_init):
    B, S, _ = h.shape
    H, dh = DIFF_H, HEAD_DIM
    q, k, v = jnp.split(h @ w_in, 3, axis=-1)
    q = rmsnorm(q.reshape(B, S, H, 2, dh), q_norm).transpose(0, 2, 3, 1, 4)
    k = rmsnorm(k.reshape(B, S, H, 2, dh), k_norm).transpose(0, 2, 3, 1, 4)
    v = v.reshape(B, S, H, 2 * dh).transpose(0, 2, 1, 3)
    lf = lam.astype(f32)
    lmbda = jnp.exp(jnp.sum(lf[0] * lf[1])) - jnp.exp(jnp.sum(lf[2] * lf[3])) + lam_init
    o = dense_diff_attention(q, k, v, alibi_slopes(H), lmbda)
    o = rmsnorm(o, subln) * (1.0 - lam_init)
    return o.transpose(0, 2, 1, 3).reshape(B, S, H * 2 * dh) @ w_out


def causal_conv(x, w):
    K, C = w.shape
    return lax.conv_general_dilated(x, w[:, None, :], window_strides=(1,), padding=[(K - 1, 0)],
                                    dimension_numbers=('NWC', 'WIO', 'NWC'), feature_group_count=C)


def chunked_gated_delta(q, k, v, beta, g):
    B, S, H, dk = q.shape
    dv = v.shape[-1]
    C = GDN_CHUNK
    n = S // C
    ch = lambda t: jnp.moveaxis(t.astype(f32).reshape((B, n, C, H) + t.shape[3:]), 3, 1)
    q, k, v, beta, g = ch(q), ch(k), ch(v), ch(beta), ch(g)
    gc = jnp.cumsum(g, axis=-1)
    tril = jnp.tril(jnp.ones((C, C), bool))
    strict = jnp.tril(jnp.ones((C, C), bool), -1)
    decay = jnp.exp(jnp.where(tril, gc[..., :, None] - gc[..., None, :], -jnp.inf))
    kb = k * beta[..., None]
    A = jnp.where(strict, jnp.einsum('bhnid,bhnjd->bhnij', kb, k) * decay, 0.0)
    eye = jnp.eye(C, dtype=f32)
    T = lax.linalg.triangular_solve(eye + A, jnp.broadcast_to(eye, A.shape), left_side=True, lower=True)
    u = T @ (v * beta[..., None])
    w = T @ (kb * jnp.exp(gc)[..., None])
    attn = jnp.einsum('bhnid,bhnjd->bhnij', q, k) * decay
    qg = q * jnp.exp(gc)[..., None]
    kd = k * jnp.exp(gc[..., -1:] - gc)[..., None]
    glast = jnp.exp(gc[..., -1])

    def step(state, xs):
        qg_i, w_i, u_i, attn_i, kd_i, gl_i = xs
        v_new = u_i - w_i @ state
        o = qg_i @ state + attn_i @ v_new
        state = state * gl_i[..., None, None] + jnp.swapaxes(kd_i, -1, -2) @ v_new
        return state, o

    xs = tuple(jnp.moveaxis(t, 2, 0) for t in (qg, w, u, attn, kd, glast))
    _, o = lax.scan(step, jnp.zeros((B, H, dk, dv), f32), xs)
    return o.transpose(1, 0, 3, 2, 4).reshape(B, S, H, dv)


def gdn_mixer(h, w_in, w_out, conv_w, a_log, dt_bias, o_norm):
    B, S, _ = h.shape
    H, dk, dv = GDN_H, GDN_DK, GDN_DV
    qkv, b, a, gate = jnp.split(h @ w_in, [GDN_QKV, GDN_QKV + H, GDN_QKV + 2 * H], axis=-1)
    qkv = jax.nn.silu(causal_conv(qkv, conv_w))
    q, k, v = jnp.split(qkv, [H * dk, 2 * H * dk], axis=-1)
    q = l2norm(q.reshape(B, S, H, dk)) * dk ** -0.5
    k = l2norm(k.reshape(B, S, H, dk))
    v = v.reshape(B, S, H, dv)
    beta = jax.nn.sigmoid(b.astype(f32))
    log_alpha = -jnp.exp(a_log.astype(f32)) * jax.nn.softplus(a.astype(f32) + dt_bias.astype(f32))
    o = chunked_gated_delta(q, k, v, beta, log_alpha).astype(h.dtype)
    o = rmsnorm(o, o_norm) * jax.nn.silu(gate.reshape(B, S, H, dv))
    return o.reshape(B, S, H * dv) @ w_out


def swa_mixer(h, w_in, w_out, q_norm, k_norm, sinks):
    B, S, _ = h.shape
    G = SWA_HQ // SWA_HKV
    dh = HEAD_DIM
    q, k, v = jnp.split(h @ w_in, [SWA_HQ * dh, (SWA_HQ + SWA_HKV) * dh], axis=-1)
    q = rmsnorm(q.reshape(B, S, SWA_HKV, G, dh), q_norm).transpose(0, 2, 3, 1, 4)
    k = rmsnorm(k.reshape(B, S, SWA_HKV, dh), k_norm).transpose(0, 2, 1, 3)
    v = v.reshape(B, S, SWA_HKV, dh).transpose(0, 2, 1, 3)
    slopes = alibi_slopes(SWA_HQ).reshape(SWA_HKV, G)
    o = banded_attention(q, k, v, slopes, SWA_WINDOW, sinks.reshape(SWA_HKV, G))
    return o.transpose(0, 3, 1, 2, 4).reshape(B, S, SWA_HQ * dh) @ w_out


def setup_inputs(seed: int = 0) -> dict:
    key = jax.random.key(seed)
    ks = iter(jax.random.split(key, 40))
    nrm = lambda shape, scale: jax.random.normal(next(ks), shape, f32) * scale
    gain = lambda shape: 1.0 + 0.02 * jax.random.normal(next(ks), shape, f32)
    n_of = lambda m: len(range(m, DEPTH, N_MIXERS))
    nA, nB, nC, nD = n_of(0), n_of(1), n_of(2), n_of(3)
    D, dh = D_MODEL, HEAD_DIM
    dt = jnp.exp(jax.random.uniform(next(ks), (nC, GDN_H), f32, math.log(1e-3), math.log(1e-1)))
    return {
        "x": jax.random.normal(next(ks), (BATCH, SEQ, D), f32),
        "ffn1_norm": gain((DEPTH, D)),
        "ffn1_w_in": nrm((DEPTH, D, 2 * D_FF), D ** -0.5),
        "ffn1_w_out": nrm((DEPTH, D_FF, D), D_FF ** -0.5),
        "mix_norm": gain((DEPTH, D)),
        "ffn2_norm": gain((DEPTH, D)),
        "ffn2_w_in": nrm((DEPTH, D, 2 * D_FF), D ** -0.5),
        "ffn2_w_out": nrm((DEPTH, D_FF, D), D_FF ** -0.5),
        "nsa_w_in": nrm((nA, D, NSA_IN), D ** -0.5),
        "nsa_w_out": nrm((nA, NSA_HQ * dh, D), (NSA_HQ * dh) ** -0.5),
        "nsa_q_norm": gain((nA, dh)),
        "nsa_k_norm": gain((nA, 3, dh)),
        "nsa_cmp_pos": nrm((nA, 2, CMP_BLOCK, dh), 0.02),
        "nsa_cmp_w1": nrm((nA, 2, CMP_BLOCK * dh, CMP_HIDDEN), (CMP_BLOCK * dh) ** -0.5),
        "nsa_cmp_w2": nrm((nA, 2, CMP_HIDDEN, dh), CMP_HIDDEN ** -0.5),
        "diff_w_in": nrm((nB, D, DIFF_IN), D ** -0.5),
        "diff_w_out": nrm((nB, DIFF_H * 2 * dh, D), (DIFF_H * 2 * dh) ** -0.5),
        "diff_q_norm": gain((nB, dh)),
        "diff_k_norm": gain((nB, dh)),
        "diff_lambda": nrm((nB, 4, dh), 0.1),
        "diff_subln": gain((nB, 2 * dh)),
        "gdn_w_in": nrm((nC, D, GDN_IN), D ** -0.5),
        "gdn_w_out": nrm((nC, GDN_H * GDN_DV, D), (GDN_H * GDN_DV) ** -0.5),
        "gdn_conv_w": nrm((nC, GDN_CONV, GDN_QKV), GDN_CONV ** -0.5),
        "gdn_a_log": jnp.log(jax.random.uniform(next(ks), (nC, GDN_H), f32, 1.0, 16.0)),
        "gdn_dt_bias": dt + jnp.log(-jnp.expm1(-dt)),
        "gdn_o_norm": gain((nC, GDN_DV)),
        "swa_w_in": nrm((nD, D, SWA_IN), D ** -0.5),
        "swa_w_out": nrm((nD, SWA_HQ * dh, D), (SWA_HQ * dh) ** -0.5),
        "swa_q_norm": gain((nD, dh)),
        "swa_k_norm": gain((nD, dh)),
        "swa_sinks": nrm((nD, SWA_HQ), 1.0),
    }


def reference(x, ffn1_norm, ffn1_w_in, ffn1_w_out, mix_norm, ffn2_norm, ffn2_w_in, ffn2_w_out,
              nsa_w_in, nsa_w_out, nsa_q_norm, nsa_k_norm, nsa_cmp_pos, nsa_cmp_w1, nsa_cmp_w2,
              diff_w_in, diff_w_out, diff_q_norm, diff_k_norm, diff_lambda, diff_subln,
              gdn_w_in, gdn_w_out, gdn_conv_w, gdn_a_log, gdn_dt_bias, gdn_o_norm,
              swa_w_in, swa_w_out, swa_q_norm, swa_k_norm, swa_sinks):
    for layer in range(DEPTH):
        kind, j = layer % N_MIXERS, layer // N_MIXERS
        x = x + 0.5 * swiglu(rmsnorm(x, ffn1_norm[layer]), ffn1_w_in[layer], ffn1_w_out[layer])
        hn = rmsnorm(x, mix_norm[layer])
        if kind == 0:
            y = nsa_mixer(hn, nsa_w_in[j], nsa_w_out[j], nsa_q_norm[j], nsa_k_norm[j],
                          nsa_cmp_pos[j], nsa_cmp_w1[j], nsa_cmp_w2[j])
        elif kind == 1:
            lam_init = 0.8 - 0.6 * math.exp(-0.3 * layer)
            y = diff_mixer(hn, diff_w_in[j], diff_w_out[j], diff_q_norm[j], diff_k_norm[j],
                           diff_lambda[j], diff_subln[j], lam_init)
        elif kind == 2:
            y = gdn_mixer(hn, gdn_w_in[j], gdn_w_out[j], gdn_conv_w[j], gdn_a_log[j],
                          gdn_dt_bias[j], gdn_o_norm[j])
        else:
            y = swa_mixer(hn, swa_w_in[j], swa_w_out[j], swa_q_norm[j], swa_k_norm[j], swa_sinks[j])
        x = x + y
        x = x + 0.5 * swiglu(rmsnorm(x, ffn2_norm[layer]), ffn2_w_in[layer], ffn2_w_out[layer])
    return x
```

```python
import functools
import math

import numpy as np
import jax
import jax.numpy as jnp
from jax import lax
from jax.experimental import pallas as pl
from jax.experimental.pallas import tpu as pltpu

F32 = jnp.float32
BF16 = jnp.bfloat16
HIGHEST = lax.Precision.HIGHEST

EPS = 1e-6
NEG = -1e30
M_INIT = -1e29
LOG2E = 1.4426950408889634
FORCE = 1e9

HEAD_DIM = 64
LANES = 128
VMEM_LIMIT = 56 * 1024 * 1024

NSA_HQ, NSA_HKV = 16, 4
CMP_BLOCK, CMP_STRIDE = 32, 16
SEL_BLOCK, SEL_TOPK = 64, 16
NSA_WINDOW = 512
DIFF_H = 8
GDN_H, GDN_DK, GDN_DV, GDN_CONV, GDN_CHUNK = 8, 128, 128, 4, 64
SWA_HQ, SWA_HKV, SWA_WINDOW = 16, 4, 128


def _params(sem):
    return pltpu.CompilerParams(dimension_semantics=sem, vmem_limit_bytes=VMEM_LIMIT)


def _nt(a, b, precision=None):
    return lax.dot_general(a, b, (((1,), (1,)), ((), ())), precision=precision,
                           preferred_element_type=F32)


def _tn(a, b, precision=None):
    return lax.dot_general(a, b, (((0,), (0,)), ((), ())), precision=precision,
                           preferred_element_type=F32)


def _rms(x, g):
    return x * lax.rsqrt(jnp.mean(x * x, axis=-1, keepdims=True) + EPS) * g


def _sigmoid(x):
    return 1.0 / (1.0 + jnp.exp(-x))


def _alibi_slopes(n):
    return [2.0 ** (-8.0 * (i + 1) / n) for i in range(n)]


def _ffn_body(x_ref, g_ref, win_ref, wout_ref, o_ref, *, d_ff, chunk):
    x = x_ref[...]
    xn = _rms(x, g_ref[...]).astype(BF16)
    acc = jnp.zeros(x.shape, F32)
    for c in range(d_ff // chunk):
        a = jnp.dot(xn, win_ref[:, c * chunk:(c + 1) * chunk], preferred_element_type=F32)
        b = jnp.dot(xn, win_ref[:, d_ff + c * chunk:d_ff + (c + 1) * chunk],
                    preferred_element_type=F32)
        h = (a * _sigmoid(a) * b).astype(BF16)
        acc = acc + jnp.dot(h, wout_ref[c * chunk:(c + 1) * chunk, :], preferred_element_type=F32)
    o_ref[...] = x + 0.5 * acc


def _ffn(x, g, w_in, w_out, tm=512, chunk=256):
    t, d = x.shape
    d_ff = w_out.shape[0]
    return pl.pallas_call(
        functools.partial(_ffn_body, d_ff=d_ff, chunk=chunk),
        grid=(t // tm,),
        in_specs=[pl.BlockSpec((tm, d), lambda i: (i, 0)),
                  pl.BlockSpec((1, d), lambda i: (0, 0)),
                  pl.BlockSpec(w_in.shape, lambda i: (0, 0), pipeline_mode=pl.Buffered(1)),
                  pl.BlockSpec(w_out.shape, lambda i: (0, 0), pipeline_mode=pl.Buffered(1))],
        out_specs=pl.BlockSpec((tm, d), lambda i: (i, 0)),
        out_shape=jax.ShapeDtypeStruct((t, d), F32),
        compiler_params=_params(("parallel",)),
        name="ffn",
    )(x, g.reshape(1, d), w_in, w_out)


def _norm_proj_body(x_ref, g_ref, *refs, n):
    xn = _rms(x_ref[...], g_ref[...]).astype(BF16)
    for w_ref, o_ref in zip(refs[:n], refs[n:]):
        o_ref[...] = jnp.dot(xn, w_ref[...], preferred_element_type=F32).astype(o_ref.dtype)


def _norm_proj(x, g, ws, tm=512):
    t, d = x.shape
    n = len(ws)
    return pl.pallas_call(
        functools.partial(_norm_proj_body, n=n),
        grid=(t // tm,),
        in_specs=[pl.BlockSpec((tm, d), lambda i: (i, 0)),
                  pl.BlockSpec((1, d), lambda i: (0, 0))]
                 + [pl.BlockSpec(w.shape, lambda i: (0, 0)) for w in ws],
        out_specs=[pl.BlockSpec((tm, w.shape[1]), lambda i: (i, 0)) for w in ws],
        out_shape=[jax.ShapeDtypeStruct((t, w.shape[1]), F32) for w in ws],
        compiler_params=_params(("parallel",)),
        name="norm_proj",
    )(x, g.reshape(1, d), *ws)


def _proj_res_body(x_ref, a_ref, w_ref, o_ref):
    o_ref[...] = x_ref[...] + jnp.dot(a_ref[...].astype(BF16), w_ref[...],
                                      preferred_element_type=F32)


def _proj_residual(x, a, w, tm=512):
    t, d = x.shape
    k = a.shape[1]
    return pl.pallas_call(
        _proj_res_body,
        grid=(t // tm,),
        in_specs=[pl.BlockSpec((tm, d), lambda i: (i, 0)),
                  pl.BlockSpec((tm, k), lambda i: (i, 0)),
                  pl.BlockSpec((k, d), lambda i: (0, 0))],
        out_specs=pl.BlockSpec((tm, d), lambda i: (i, 0)),
        out_shape=jax.ShapeDtypeStruct((t, d), F32),
        compiler_params=_params(("parallel",)),
        name="proj_residual",
    )(x, a, w)


def _head_prep_body(y_ref, g_ref, o_ref, *, nh, norm, scale):
    for h in range(nh):
        yh = y_ref[0, :, h * HEAD_DIM:(h + 1) * HEAD_DIM]
        if norm:
            yh = _rms(yh, g_ref[...])
        if scale != 1.0:
            yh = yh * scale
        o_ref[0, h] = yh.astype(o_ref.dtype)


def _head_prep(y, gain, scale, out_dtype, ts=512):
    b, s, c = y.shape
    nh = c // HEAD_DIM
    norm = gain is not None
    g = (gain if norm else jnp.ones((HEAD_DIM,), F32)).reshape(1, HEAD_DIM)
    return pl.pallas_call(
        functools.partial(_head_prep_body, nh=nh, norm=norm, scale=scale),
        grid=(b, s // ts),
        in_specs=[pl.BlockSpec((1, ts, c), lambda bi, i: (bi, i, 0)),
                  pl.BlockSpec((1, HEAD_DIM), lambda bi, i: (0, 0))],
        out_specs=pl.BlockSpec((1, nh, ts, HEAD_DIM), lambda bi, i: (bi, 0, i, 0)),
        out_shape=jax.ShapeDtypeStruct((b, nh, s, HEAD_DIM), out_dtype),
        compiler_params=_params(("parallel", "parallel")),
        name="head_prep",
    )(y, g)


def _slope_rows(n_heads, n_kv, rows_per_head, mult=LOG2E):
    g = n_heads // n_kv
    sl = np.asarray(_alibi_slopes(n_heads), np.float32).reshape(n_kv, g, 1) * np.float32(mult)
    return jnp.asarray(np.repeat(sl, rows_per_head, axis=1).reshape(n_kv, g * rows_per_head, 1))


def _banded_body(*refs, g, tq, nprev, window, has_sink):
    q_ref = refs[0]
    k_refs = refs[1:2 + nprev]
    v_refs = refs[2 + nprev:3 + 2 * nprev]
    sl_ref = refs[3 + 2 * nprev]
    pos = 4 + 2 * nprev
    sink_ref = refs[pos] if has_sink else None
    o_ref = refs[pos + (1 if has_sink else 0)]
    i = pl.program_id(2)
    rows = g * tq
    span = (nprev + 1) * tq
    q = q_ref[0, 0].reshape(rows, HEAD_DIM)
    s = jnp.concatenate([_nt(q, k_ref[0, 0]) for k_ref in k_refs], axis=-1)
    r = lax.broadcasted_iota(jnp.int32, (rows, 1), 0) % tq
    c = lax.broadcasted_iota(jnp.int32, (1, span), 1)
    dist = r + nprev * tq - c
    valid = (dist >= 0) & (dist < window) & (c + (i - nprev) * tq >= 0)
    s = jnp.where(valid, s - sl_ref[0] * dist.astype(F32), NEG)
    m = jnp.max(s, axis=-1, keepdims=True)
    if has_sink:
        sink = sink_ref[0] * LOG2E
        m = jnp.maximum(m, sink)
    e = jnp.exp2(s - m)
    l = jnp.sum(e, axis=-1, keepdims=True)
    if has_sink:
        l = l + jnp.exp2(sink - m)
    p = (e / l).astype(BF16)
    o = jnp.zeros((rows, HEAD_DIM), F32)
    for j, v_ref in enumerate(v_refs):
        o = o + jnp.dot(p[:, j * tq:(j + 1) * tq], v_ref[0, 0], preferred_element_type=F32)
    o_ref[0] = jnp.concatenate([o[gi * tq:(gi + 1) * tq] for gi in range(g)], axis=-1)


def _banded_attention(q, k, v, n_heads, window, tq, sinks=None):
    b, hq, s, _ = q.shape
    hkv = k.shape[1]
    g = hq // hkv
    nprev = -(-window // tq)
    q5 = q.reshape(b, hkv, g, s, HEAD_DIM)
    kv_specs = [pl.BlockSpec((1, 1, tq, HEAD_DIM),
                             functools.partial(lambda bi, h, i, j: (bi, h, jnp.maximum(i - nprev + j, 0), 0), j=j))
                for j in range(nprev + 1)]
    in_specs = ([pl.BlockSpec((1, 1, g, tq, HEAD_DIM), lambda bi, h, i: (bi, h, 0, i, 0))]
                + kv_specs + kv_specs
                + [pl.BlockSpec((1, g * tq, 1), lambda bi, h, i: (h, 0, 0))])
    args = [q5] + [k] * (nprev + 1) + [v] * (nprev + 1) + [_slope_rows(n_heads, hkv, tq)]
    if sinks is not None:
        in_specs.append(pl.BlockSpec((1, g * tq, 1), lambda bi, h, i: (h, 0, 0)))
        args.append(jnp.repeat(sinks.astype(F32).reshape(hkv, g, 1), tq, axis=1).reshape(hkv, g * tq, 1))
    return pl.pallas_call(
        functools.partial(_banded_body, g=g, tq=tq, nprev=nprev, window=window,
                          has_sink=sinks is not None),
        grid=(b, hkv, s // tq),
        in_specs=in_specs,
        out_specs=pl.BlockSpec((1, tq, g * HEAD_DIM), lambda bi, h, i: (bi, i, h)),
        out_shape=jax.ShapeDtypeStruct((b, s, hq * HEAD_DIM), F32),
        compiler_params=_params(("parallel", "parallel", "parallel")),
        name="banded_attention",
    )(*args)


def _compress_body(t_ref, pt_ref, pb_ref, w1t_ref, w1b_ref, w2_ref, g_ref, o_ref, *, norm):
    sub = t_ref[0, 0]
    n_sub = sub.shape[0]
    a = jnp.dot((sub + pt_ref[...]).astype(BF16), w1t_ref[...], preferred_element_type=F32)
    bm = jnp.dot((sub + pb_ref[...]).astype(BF16), w1b_ref[...], preferred_element_type=F32)
    hid = a + pltpu.roll(bm, shift=n_sub - 1, axis=0)
    hid = 0.5 * hid * (1.0 + jnp.tanh(0.7978845608028654 * (hid + 0.044715 * hid * hid * hid)))
    out = jnp.dot(hid.astype(BF16), w2_ref[...], preferred_element_type=F32)
    if norm:
        out = _rms(out, g_ref[...])
    o_ref[0, 0] = out


def _compress(t, pos, w1, w2, gain):
    b, h, s, _ = t.shape
    n_sub = s // CMP_STRIDE
    half = CMP_STRIDE * HEAD_DIM
    hidden = w1.shape[1]
    norm = gain is not None
    g = (gain if norm else jnp.ones((HEAD_DIM,), F32)).reshape(1, HEAD_DIM)
    const = lambda shape: pl.BlockSpec(shape, lambda bi, hi: (0,) * len(shape))
    return pl.pallas_call(
        functools.partial(_compress_body, norm=norm),
        grid=(b, h),
        in_specs=[pl.BlockSpec((1, 1, n_sub, half), lambda bi, hi: (bi, hi, 0, 0)),
                  const((1, half)), const((1, half)),
                  const((half, hidden)), const((half, hidden)),
                  const((hidden, HEAD_DIM)), const((1, HEAD_DIM))],
        out_specs=pl.BlockSpec((1, 1, n_sub, HEAD_DIM), lambda bi, hi: (bi, hi, 0, 0)),
        out_shape=jax.ShapeDtypeStruct((b, h, n_sub, HEAD_DIM), F32),
        compiler_params=_params(("parallel", "parallel")),
        name="nsa_compress",
    )(t.reshape(b, h, n_sub, half),
      pos[:CMP_STRIDE].reshape(1, half), pos[CMP_STRIDE:].reshape(1, half),
      w1[:half].astype(BF16), w1[half:].astype(BF16), w2.astype(BF16), g)


def _cmp_select_body(q_ref, kc_ref, vc_ref, ov_ref, sl_ref, o_ref, sel_ref, *, g, tq, n_top):
    i = pl.program_id(2)
    rows = g * tq
    n_sub = kc_ref.shape[2]
    n_sel = ov_ref.shape[1]
    q = q_ref[0, 0].reshape(rows, HEAD_DIM)
    s = _nt(q, kc_ref[0, 0].astype(BF16))
    t_row = i * tq + lax.broadcasted_iota(jnp.int32, (rows, 1), 0) % tq
    cmp_end = lax.broadcasted_iota(jnp.int32, (1, n_sub), 1) * CMP_STRIDE + (CMP_BLOCK - 1)
    dist = t_row - cmp_end
    valid = dist >= 0
    s = jnp.where(valid, s - sl_ref[0] * dist.astype(F32), NEG)
    m = jnp.maximum(jnp.max(s, axis=-1, keepdims=True), M_INIT)
    e = jnp.exp2(s - m)
    l = jnp.sum(e, axis=-1, keepdims=True)
    p = e / jnp.where(l > 0.0, l, 1.0)
    o = jnp.dot(p.astype(BF16), vc_ref[0, 0].astype(BF16), preferred_element_type=F32)
    o_ref[0] = jnp.concatenate([o[gi * tq:(gi + 1) * tq] for gi in range(g)], axis=-1)

    psum = p[0:tq]
    for gi in range(1, g):
        psum = psum + p[gi * tq:(gi + 1) * tq]
    p_hi = psum.astype(BF16)
    r1 = psum - p_hi.astype(F32)
    p_mid = r1.astype(BF16)
    p_lo = (r1 - p_mid.astype(F32)).astype(BF16)
    ov = ov_ref[...]
    imp = (jnp.dot(p_hi, ov, preferred_element_type=F32)
           + jnp.dot(p_mid, ov, preferred_element_type=F32)
           + jnp.dot(p_lo, ov, preferred_element_type=F32))
    t_q = i * tq + lax.broadcasted_iota(jnp.int32, (tq, 1), 0)
    cur = t_q // SEL_BLOCK
    j = lax.broadcasted_iota(jnp.int32, (1, n_sel), 1)
    forced = (j == 0) | (j == cur) | (j == cur - 1)
    causal = j <= cur
    imp = jnp.where(forced, FORCE, jnp.where(causal, imp, -1.0))
    sel = jnp.zeros((tq, n_sel), F32)
    for _ in range(n_top):
        mx = jnp.max(imp, axis=-1, keepdims=True)
        first = jnp.min(jnp.where(imp == mx, j, n_sel), axis=-1, keepdims=True)
        hit = j == first
        sel = jnp.where(hit, 1.0, sel)
        imp = jnp.where(hit, -2.0, imp)
    sel_ref[0, 0] = jnp.where(causal, sel, 0.0).astype(sel_ref.dtype)


def _cmp_select(q, kc, vc, tq=128):
    b, hq, s, _ = q.shape
    hkv = kc.shape[1]
    g = hq // hkv
    n_sub = kc.shape[2]
    n_sel = s // SEL_BLOCK
    n_top = min(SEL_TOPK, n_sel)
    cs = np.arange(n_sub)[:, None] * CMP_STRIDE
    ss = np.arange(n_sel)[None, :] * SEL_BLOCK
    overlap = ((cs < ss + SEL_BLOCK) & (cs + CMP_BLOCK > ss)).astype(np.float32)
    return pl.pallas_call(
        functools.partial(_cmp_select_body, g=g, tq=tq, n_top=n_top),
        grid=(b, hkv, s // tq),
        in_specs=[pl.BlockSpec((1, 1, g, tq, HEAD_DIM), lambda bi, h, i: (bi, h, 0, i, 0)),
                  pl.BlockSpec((1, 1, n_sub, HEAD_DIM), lambda bi, h, i: (bi, h, 0, 0)),
                  pl.BlockSpec((1, 1, n_sub, HEAD_DIM), lambda bi, h, i: (bi, h, 0, 0)),
                  pl.BlockSpec((n_sub, n_sel), lambda bi, h, i: (0, 0)),
                  pl.BlockSpec((1, g * tq, 1), lambda bi, h, i: (h, 0, 0))],
        out_specs=[pl.BlockSpec((1, tq, g * HEAD_DIM), lambda bi, h, i: (bi, i, h)),
                   pl.BlockSpec((1, 1, tq, n_sel), lambda bi, h, i: (bi, h, i, 0))],
        out_shape=[jax.ShapeDtypeStruct((b, s, hq * HEAD_DIM), F32),
                   jax.ShapeDtypeStruct((b, hkv, s, n_sel), BF16)],
        compiler_params=_params(("parallel", "parallel", "parallel")),
        name="nsa_cmp_select",
    )(q.reshape(b, hkv, g, s, HEAD_DIM), kc, vc, jnp.asarray(overlap, BF16),
      _slope_rows(hq, hkv, tq))


def _sel_attn_body(q_ref, k_ref, v_ref, sel_ref, sl_ref, o_ref, *, g, tq, tk):
    i = pl.program_id(2)
    rows = g * tq
    n_sel = sel_ref.shape[3]
    per_chunk = tk // SEL_BLOCK
    q = q_ref[0, 0].reshape(rows, HEAD_DIM)
    sel = sel_ref[0, 0]
    slope = sl_ref[0]
    r = lax.broadcasted_iota(jnp.int32, (rows, 1), 0) % tq
    c = lax.broadcasted_iota(jnp.int32, (1, tk), 1)
    jrow = lax.broadcasted_iota(jnp.int32, (n_sel, 1), 0)
    jcol = lax.broadcasted_iota(jnp.int32, (1, tk), 1) // SEL_BLOCK

    def chunk(ci, carry):
        m, l, acc = carry
        start = pl.multiple_of(ci * tk, tk)
        k = k_ref[0, 0, pl.ds(start, tk), :]
        v = v_ref[0, 0, pl.ds(start, tk), :]
        s = _nt(q, k)
        expand = jnp.where(jrow == jcol + ci * per_chunk, 1.0, 0.0).astype(BF16)
        picked = jnp.dot(sel, expand, preferred_element_type=F32)
        picked = jnp.concatenate([picked] * g, axis=0)
        dist = r + (i * tq - ci * tk) - c
        valid = (dist >= 0) & (picked > 0.5)
        s = jnp.where(valid, s - slope * dist.astype(F32), NEG)
        m_new = jnp.maximum(m, jnp.max(s, axis=-1, keepdims=True))
        alpha = jnp.exp2(m - m_new)
        e = jnp.exp2(s - m_new)
        l = alpha * l + jnp.sum(e, axis=-1, keepdims=True)
        acc = alpha * acc + jnp.dot(e.astype(BF16), v, preferred_element_type=F32)
        return m_new, l, acc

    n_chunks = (i * tq + tq - 1) // tk + 1
    m0 = jnp.full((rows, 1), M_INIT, F32)
    _, l, acc = lax.fori_loop(0, n_chunks, chunk,
                              (m0, jnp.zeros((rows, 1), F32), jnp.zeros((rows, HEAD_DIM), F32)))
    o = acc / l
    o_ref[0] = jnp.concatenate([o[gi * tq:(gi + 1) * tq] for gi in range(g)], axis=-1)


def _sel_attention(q, k, v, sel, tq=128, tk=512):
    b, hq, s, _ = q.shape
    hkv = k.shape[1]
    g = hq // hkv
    n_sel = sel.shape[3]
    tk = min(tk, s)
    return pl.pallas_call(
        functools.partial(_sel_attn_body, g=g, tq=tq, tk=tk),
        grid=(b, hkv, s // tq),
        in_specs=[pl.BlockSpec((1, 1, g, tq, HEAD_DIM), lambda bi, h, i: (bi, h, 0, i, 0)),
                  pl.BlockSpec((1, 1, s, HEAD_DIM), lambda bi, h, i: (bi, h, 0, 0)),
                  pl.BlockSpec((1, 1, s, HEAD_DIM), lambda bi, h, i: (bi, h, 0, 0)),
                  pl.BlockSpec((1, 1, tq, n_sel), lambda bi, h, i: (bi, h, i, 0)),
                  pl.BlockSpec((1, g * tq, 1), lambda bi, h, i: (h, 0, 0))],
        out_specs=pl.BlockSpec((1, tq, g * HEAD_DIM), lambda bi, h, i: (bi, i, h)),
        out_shape=jax.ShapeDtypeStruct((b, s, hq * HEAD_DIM), F32),
        compiler_params=_params(("parallel", "parallel", "arbitrary")),
        name="nsa_selected_attention",
    )(q.reshape(b, hkv, g, s, HEAD_DIM), k, v, sel, _slope_rows(hq, hkv, tq))


def _nsa_out_body(x_ref, gt_ref, oc_ref, os_ref, ow_ref, ex_ref, w_ref, o_ref):
    sig = _sigmoid(gt_ref[...])
    mix = jnp.zeros(oc_ref.shape, F32)
    for j, br_ref in enumerate((oc_ref, os_ref, ow_ref)):
        gate = jnp.dot(sig, ex_ref[j], precision=HIGHEST, preferred_element_type=F32)
        mix = mix + gate * br_ref[...]
    o_ref[...] = x_ref[...] + jnp.dot(mix.astype(BF16), w_ref[...], preferred_element_type=F32)


def _nsa_out(x, gates, o_cmp, o_sel, o_win, w_out, tm=256):
    t, d = x.shape
    c = o_cmp.shape[1]
    expand = np.zeros((3, LANES, c), np.float32)
    for head in range(NSA_HQ):
        for j in range(3):
            expand[j, head * 3 + j, head * HEAD_DIM:(head + 1) * HEAD_DIM] = 1.0
    row = lambda width: pl.BlockSpec((tm, width), lambda i: (i, 0))
    return pl.pallas_call(
        _nsa_out_body,
        grid=(t // tm,),
        in_specs=[row(d), row(LANES), row(c), row(c), row(c),
                  pl.BlockSpec((3, LANES, c), lambda i: (0, 0, 0)),
                  pl.BlockSpec((c, d), lambda i: (0, 0))],
        out_specs=row(d),
        out_shape=jax.ShapeDtypeStruct((t, d), F32),
        compiler_params=_params(("parallel",)),
        name="nsa_out",
    )(x, gates, o_cmp, o_sel, o_win, jnp.asarray(expand), w_out)


def _nsa_layer(x, b, s, mix_norm, w_in, w_out, q_norm, k_norm, cmp_pos, cmp_w1, cmp_w2):
    dh = HEAD_DIM
    kvw = NSA_HKV * dh
    edges = np.cumsum([0, NSA_HQ * dh] + [kvw] * 6).tolist()
    ws = [w_in[:, edges[j]:edges[j + 1]].astype(BF16) for j in range(7)]
    n_gate = 3 * NSA_HQ
    ws.append(jnp.pad(w_in[:, edges[7]:edges[7] + n_gate], ((0, 0), (0, LANES - n_gate))).astype(BF16))
    q, kc, vc, ks, vs, kw, vw, gates = _norm_proj(x, mix_norm, ws)
    as3 = lambda y: y.reshape(b, s, y.shape[-1])
    qh = _head_prep(as3(q), q_norm, dh ** -0.5 * LOG2E, BF16)
    kch = _compress(_head_prep(as3(kc), None, 1.0, F32), cmp_pos[0], cmp_w1[0], cmp_w2[0], k_norm[0])
    vch = _compress(_head_prep(as3(vc), None, 1.0, F32), cmp_pos[1], cmp_w1[1], cmp_w2[1], None)
    ksh = _head_prep(as3(ks), k_norm[1], 1.0, BF16)
    vsh = _head_prep(as3(vs), None, 1.0, BF16)
    kwh = _head_prep(as3(kw), k_norm[2], 1.0, BF16)
    vwh = _head_prep(as3(vw), None, 1.0, BF16)
    o_cmp, sel = _cmp_select(qh, kch, vch)
    o_sel = _sel_attention(qh, ksh, vsh, sel)
    o_win = _banded_attention(qh, kwh, vwh, NSA_HQ, NSA_WINDOW, tq=min(512, s))
    flat = lambda y: y.reshape(b * s, y.shape[-1])
    return _nsa_out(x, gates, flat(o_cmp), flat(o_sel), flat(o_win), w_out.astype(BF16))


def _diff_prep_body(y_ref, gq_ref, gk_ref, q_ref, k_ref, v_ref, *, width, q_scale):
    lane = lax.broadcasted_iota(jnp.int32, (1, LANES), 1)
    low = lane < HEAD_DIM

    def pair_norm(ysl, gain):
        sq = ysl * ysl
        ss_lo = jnp.sum(jnp.where(low, sq, 0.0), axis=-1, keepdims=True)
        ss_hi = jnp.sum(jnp.where(low, 0.0, sq), axis=-1, keepdims=True)
        inv = jnp.where(low, lax.rsqrt(ss_lo / HEAD_DIM + EPS), lax.rsqrt(ss_hi / HEAD_DIM + EPS))
        return ysl * inv * gain

    for j in range(width // LANES):
        sl = slice(j * LANES, (j + 1) * LANES)
        q_ref[:, sl] = (pair_norm(y_ref[:, sl], gq_ref[...]) * q_scale).astype(BF16)
        k_ref[:, sl] = pair_norm(y_ref[:, width + j * LANES:width + (j + 1) * LANES],
                                 gk_ref[...]).astype(BF16)
        v_ref[:, sl] = y_ref[:, 2 * width + j * LANES:2 * width + (j + 1) * LANES].astype(BF16)


def _diff_prep(qkv, q_norm, k_norm, tm=512):
    t, w3 = qkv.shape
    width = w3 // 3
    pair = lambda gvec: jnp.concatenate([gvec, gvec]).reshape(1, LANES).astype(F32)
    out = jax.ShapeDtypeStruct((t, width), BF16)
    return pl.pallas_call(
        functools.partial(_diff_prep_body, width=width, q_scale=HEAD_DIM ** -0.5 * LOG2E),
        grid=(t // tm,),
        in_specs=[pl.BlockSpec((tm, w3), lambda i: (i, 0)),
                  pl.BlockSpec((1, LANES), lambda i: (0, 0)),
                  pl.BlockSpec((1, LANES), lambda i: (0, 0))],
        out_specs=[pl.BlockSpec((tm, width), lambda i: (i, 0))] * 3,
        out_shape=[out, out, out],
        compiler_params=_params(("parallel",)),
        name="diff_prep",
    )(qkv, pair(q_norm), pair(k_norm))


def _diff_flash_body(q_ref, k_ref, v_ref, sl_ref, lam_ref, sub_ref, o_ref, *, tq, lam_init):
    i = pl.program_id(2)
    q = q_ref[0]
    lane = lax.broadcasted_iota(jnp.int32, (1, LANES), 1)
    zero = jnp.zeros_like(q)
    q_parts = (jnp.where(lane < HEAD_DIM, q, zero), jnp.where(lane < HEAD_DIM, zero, q))
    slope = sl_ref[0]
    r = lax.broadcasted_iota(jnp.int32, (tq, 1), 0)
    c = lax.broadcasted_iota(jnp.int32, (1, tq), 1)
    local = (r - c).astype(F32)
    bias0 = -slope * local

    def step(ci, carry, diagonal):
        start = pl.multiple_of(ci * tq, tq)
        k = k_ref[0, pl.ds(start, tq), :]
        v = v_ref[0, pl.ds(start, tq), :]
        bias = bias0 - slope * ((i - ci) * tq).astype(F32)
        out = []
        for comp in range(2):
            m, l, acc = carry[comp]
            s = _nt(q_parts[comp], k) + bias
            if diagonal:
                s = jnp.where(local >= 0.0, s, NEG)
            m_new = jnp.maximum(m, jnp.max(s, axis=-1, keepdims=True))
            alpha = jnp.exp2(m - m_new)
            e = jnp.exp2(s - m_new)
            l = alpha * l + jnp.sum(e, axis=-1, keepdims=True)
            acc = alpha * acc + jnp.dot(e.astype(BF16), v, preferred_element_type=F32)
            out.append((m_new, l, acc))
        return tuple(out)

    init = tuple((jnp.full((tq, 1), M_INIT, F32), jnp.zeros((tq, 1), F32),
                  jnp.zeros((tq, LANES), F32)) for _ in range(2))
    carry = lax.fori_loop(0, i, lambda ci, cr: step(ci, cr, False), init)
    (_, l0, a0), (_, l1, a1) = step(i, carry, True)
    lam = lam_ref[...]
    lam01 = jnp.sum(lam[0:1] * lam[1:2], axis=-1, keepdims=True)
    lam23 = jnp.sum(lam[2:3] * lam[3:4], axis=-1, keepdims=True)
    lmbda = jnp.exp(lam01) - jnp.exp(lam23) + lam_init
    o = a0 / l0 - lmbda * (a1 / l1)
    o_ref[0] = _rms(o, sub_ref[...]) * (1.0 - lam_init)


def _diff_flash(qn, kn, vb, lam, subln, lam_init, tq=512):
    b, s, w = qn.shape
    n_heads = w // LANES
    tq = min(tq, s)
    slopes = jnp.asarray(np.asarray(_alibi_slopes(n_heads), np.float32).reshape(n_heads, 1, 1)
                         * np.float32(LOG2E))
    return pl.pallas_call(
        functools.partial(_diff_flash_body, tq=tq, lam_init=lam_init),
        grid=(b, n_heads, s // tq),
        in_specs=[pl.BlockSpec((1, tq, LANES), lambda bi, h, i: (bi, i, h)),
                  pl.BlockSpec((1, s, LANES), lambda bi, h, i: (bi, 0, h)),
                  pl.BlockSpec((1, s, LANES), lambda bi, h, i: (bi, 0, h)),
                  pl.BlockSpec((1, 1, 1), lambda bi, h, i: (h, 0, 0)),
                  pl.BlockSpec(lam.shape, lambda bi, h, i: (0, 0)),
                  pl.BlockSpec((1, LANES), lambda bi, h, i: (0, 0))],
        out_specs=pl.BlockSpec((1, tq, LANES), lambda bi, h, i: (bi, i, h)),
        out_shape=jax.ShapeDtypeStruct((b, s, w), F32),
        compiler_params=_params(("parallel", "parallel", "arbitrary")),
        name="diff_flash",
    )(qn, kn, vb, slopes, lam.astype(F32), subln.reshape(1, LANES).astype(F32))


def _diff_layer(x, b, s, mix_norm, w_in, w_out, q_norm, k_norm, lam, subln, lam_init):
    (qkv,) = _norm_proj(x, mix_norm, [w_in.astype(BF16)])
    qn, kn, vb = _diff_prep(qkv, q_norm, k_norm)
    as3 = lambda y: y.reshape(b, s, y.shape[-1])
    o = _diff_flash(as3(qn), as3(kn), as3(vb), lam, subln, lam_init)
    return _proj_residual(x, o.reshape(b * s, -1), w_out.astype(BF16))


def _gdn_prep_body(cur_ref, prev_ref, cw_ref, b_ref, a_ref, alog_ref, dt_ref,
                   q_ref, k_ref, v_ref, beta_ref, gc_ref, pad_ref, *, ts, width):
    i = pl.program_id(1)
    halo = prev_ref.shape[1]
    pad_ref[0:halo, :] = jnp.where(i > 0, prev_ref[0], 0.0)
    pad_ref[halo:, :] = cur_ref[0]
    y = jnp.zeros((ts, 3 * width), F32)
    for j in range(GDN_CONV):
        off = halo - (GDN_CONV - 1) + j
        y = y + cw_ref[j:j + 1, :] * pad_ref[off:off + ts, :]
    y = y * _sigmoid(y)
    for h in range(width // GDN_DK):
        sl = slice(h * GDN_DK, (h + 1) * GDN_DK)
        qh = y[:, sl]
        q_ref[0, :, sl] = qh * lax.rsqrt(jnp.sum(qh * qh, axis=-1, keepdims=True) + EPS) * (GDN_DK ** -0.5)
        kh = y[:, width + h * GDN_DK:width + (h + 1) * GDN_DK]
        k_ref[0, :, sl] = kh * lax.rsqrt(jnp.sum(kh * kh, axis=-1, keepdims=True) + EPS)
    v_ref[0] = y[:, 2 * width:]
    beta_ref[0] = _sigmoid(b_ref[0])
    z = a_ref[0] + dt_ref[...]
    softplus = jnp.maximum(z, 0.0) + jnp.log(1.0 + jnp.exp(-jnp.abs(z)))
    gl = -jnp.exp(alog_ref[...]) * softplus
    r = lax.broadcasted_iota(jnp.int32, (ts, ts), 0)
    c = lax.broadcasted_iota(jnp.int32, (ts, ts), 1)
    tri = jnp.where((r // GDN_CHUNK == c // GDN_CHUNK) & (c <= r), 1.0, 0.0)
    gc_ref[0] = jnp.dot(tri, gl, precision=HIGHEST, preferred_element_type=F32)


def _gdn_prep(qkv, b_pad, a_pad, conv_w, a_log, dt_bias, ts=256):
    b, s, w3 = qkv.shape
    width = w3 // 3
    halo = 8
    pad_lane = lambda vec: jnp.pad(vec.astype(F32), (0, LANES - vec.shape[0])).reshape(1, LANES)
    big = jax.ShapeDtypeStruct((b, s, width), F32)
    small = jax.ShapeDtypeStruct((b, s, LANES), F32)
    tok = lambda wd: pl.BlockSpec((1, ts, wd), lambda bi, i: (bi, i, 0))
    return pl.pallas_call(
        functools.partial(_gdn_prep_body, ts=ts, width=width),
        grid=(b, s // ts),
        in_specs=[tok(w3),
                  pl.BlockSpec((1, halo, w3), lambda bi, i: (bi, jnp.maximum(i * (ts // halo) - 1, 0), 0)),
                  pl.BlockSpec((GDN_CONV, w3), lambda bi, i: (0, 0)),
                  tok(LANES), tok(LANES),
                  pl.BlockSpec((1, LANES), lambda bi, i: (0, 0)),
                  pl.BlockSpec((1, LANES), lambda bi, i: (0, 0))],
        out_specs=[tok(width), tok(width), tok(width), tok(LANES), tok(LANES)],
        out_shape=[big, big, big, small, small],
        scratch_shapes=[pltpu.VMEM((ts + halo, w3), F32)],
        compiler_params=_params(("parallel", "parallel")),
        name="gdn_prep",
    )(qkv, qkv, conv_w.astype(F32), b_pad, a_pad, pad_lane(a_log), pad_lane(dt_bias))


def _unit_lower_inverse(a):
    n = a.shape[0]
    r = lax.broadcasted_iota(jnp.int32, (n, n), 0)
    c = lax.broadcasted_iota(jnp.int32, (n, n), 1)
    t = jnp.where(r == c, 1.0, 0.0)
    d = 1
    while d < n:
        off = jnp.where((r // (2 * d) == c // (2 * d)) & (r // d != c // d), a, 0.0)
        if d == 1:
            t = t - off
        else:
            ta = jnp.dot(t, off, precision=HIGHEST, preferred_element_type=F32)
            t = t - jnp.dot(ta, t, precision=HIGHEST, preferred_element_type=F32)
        d *= 2
    return t


def _gdn_chunk_body(q_ref, k_ref, v_ref, beta_ref, gc_ref, gr_ref, gate_ref, on_ref, o_ref,
                    state_ref, *, nc, n_heads):
    @pl.when(pl.program_id(1) == 0)
    def _():
        state_ref[...] = jnp.zeros(state_ref.shape, F32)

    cl = GDN_CHUNK
    r = lax.broadcasted_iota(jnp.int32, (cl, cl), 0)
    c = lax.broadcasted_iota(jnp.int32, (cl, cl), 1)
    for ch in range(nc):
        rows = slice(ch * cl, (ch + 1) * cl)
        for h in range(n_heads):
            lanes = slice(h * GDN_DK, (h + 1) * GDN_DK)
            qh = q_ref[0, rows, lanes]
            kh = k_ref[0, rows, lanes]
            vh = v_ref[0, rows, lanes]
            bcol = beta_ref[0, rows, h:h + 1]
            gcol = gc_ref[0, rows, h:h + 1]
            grow = gr_ref[0, ch, h:h + 1, :]
            glast = gc_ref[0, ch * cl + cl - 1:ch * cl + cl, h:h + 1]
            decay = jnp.exp(jnp.minimum(gcol - grow, 0.0))
            kb = kh * bcol
            a = jnp.where(r > c, _nt(kb, kh, HIGHEST) * decay, 0.0)
            t = _unit_lower_inverse(a)
            eg = jnp.exp(gcol)
            u = jnp.dot(t, vh * bcol, precision=HIGHEST, preferred_element_type=F32)
            w = jnp.dot(t, kb * eg, precision=HIGHEST, preferred_element_type=F32)
            attn = jnp.where(r >= c, _nt(qh, kh) * decay, 0.0)
            state = state_ref[h]
            v_new = u - jnp.dot(w, state, preferred_element_type=F32)
            o = (jnp.dot(qh * eg, state, preferred_element_type=F32)
                 + jnp.dot(attn, v_new, preferred_element_type=F32))
            state_ref[h] = state * jnp.exp(glast) + _tn(kh * jnp.exp(glast - gcol), v_new)
            gate = gate_ref[0, rows, lanes]
            o_ref[0, rows, lanes] = _rms(o, on_ref[...]) * (gate * _sigmoid(gate))


def _gdn_chunks(q, k, v, beta, gc, gc_rows, gate, o_norm, nc=1):
    b, s, w = q.shape
    n_heads = w // GDN_DK
    ts = nc * GDN_CHUNK
    tok = lambda wd: pl.BlockSpec((1, ts, wd), lambda bi, i: (bi, i, 0))
    return pl.pallas_call(
        functools.partial(_gdn_chunk_body, nc=nc, n_heads=n_heads),
        grid=(b, s // ts),
        in_specs=[tok(w), tok(w), tok(w), tok(LANES), tok(LANES),
                  pl.BlockSpec((1, nc, 8, GDN_CHUNK), lambda bi, i: (bi, i, 0, 0)),
                  tok(w),
                  pl.BlockSpec((1, GDN_DV), lambda bi, i: (0, 0))],
        out_specs=tok(w),
        out_shape=jax.ShapeDtypeStruct((b, s, w), F32),
        scratch_shapes=[pltpu.VMEM((n_heads, GDN_DK, GDN_DV), F32)],
        compiler_params=_params(("parallel", "arbitrary")),
        name="gdn_chunks",
    )(q, k, v, beta, gc, gc_rows, gate, o_norm.reshape(1, GDN_DV).astype(F32))


def _gdn_layer(x, b, s, mix_norm, w_in, w_out, conv_w, a_log, dt_bias, o_norm):
    qkv_w = GDN_H * (2 * GDN_DK + GDN_DV)
    lane_pad = lambda wcols: jnp.pad(wcols, ((0, 0), (0, LANES - wcols.shape[1]))).astype(BF16)
    ws = [w_in[:, :qkv_w].astype(BF16),
          lane_pad(w_in[:, qkv_w:qkv_w + GDN_H]),
          lane_pad(w_in[:, qkv_w + GDN_H:qkv_w + 2 * GDN_H]),
          w_in[:, qkv_w + 2 * GDN_H:].astype(BF16)]
    qkv, b_pad, a_pad, gate = _norm_proj(x, mix_norm, ws)
    as3 = lambda y: y.reshape(b, s, y.shape[-1])
    q, k, v, beta, gc = _gdn_prep(as3(qkv), as3(b_pad), as3(a_pad), conv_w, a_log, dt_bias,
                                  ts=min(256, s))
    gc_rows = gc[:, :, :8].reshape(b, s // GDN_CHUNK, GDN_CHUNK, 8).transpose(0, 1, 3, 2)
    o = _gdn_chunks(q, k, v, beta, gc, gc_rows, as3(gate), o_norm)
    return _proj_residual(x, o.reshape(b * s, -1), w_out.astype(BF16))


def _swa_layer(x, b, s, mix_norm, w_in, w_out, q_norm, k_norm, sinks):
    dh = HEAD_DIM
    e0, e1 = SWA_HQ * dh, (SWA_HQ + SWA_HKV) * dh
    ws = [w_in[:, :e0].astype(BF16), w_in[:, e0:e1].astype(BF16), w_in[:, e1:].astype(BF16)]
    q, k, v = _norm_proj(x, mix_norm, ws)
    as3 = lambda y: y.reshape(b, s, y.shape[-1])
    qh = _head_prep(as3(q), q_norm, dh ** -0.5 * LOG2E, BF16)
    kh = _head_prep(as3(k), k_norm, 1.0, BF16)
    vh = _head_prep(as3(v), None, 1.0, BF16)
    o = _banded_attention(qh, kh, vh, SWA_HQ, SWA_WINDOW, tq=min(256, s), sinks=sinks)
    return _proj_residual(x, o.reshape(b * s, -1), w_out.astype(BF16))


def kernel(x, ffn1_norm, ffn1_w_in, ffn1_w_out, mix_norm, ffn2_norm, ffn2_w_in, ffn2_w_out,
           nsa_w_in, nsa_w_out, nsa_q_norm, nsa_k_norm, nsa_cmp_pos, nsa_cmp_w1, nsa_cmp_w2,
           diff_w_in, diff_w_out, diff_q_norm, diff_k_norm, diff_lambda, diff_subln,
           gdn_w_in, gdn_w_out, gdn_conv_w, gdn_a_log, gdn_dt_bias, gdn_o_norm,
           swa_w_in, swa_w_out, swa_q_norm, swa_k_norm, swa_sinks):
    b, s, d = x.shape
    depth = ffn1_norm.shape[0]
    n_mixers = 4
    h = x.reshape(b * s, d)
    for layer in range(depth):
        kind, j = layer % n_mixers, layer // n_mixers
        h = _ffn(h, ffn1_norm[layer], ffn1_w_in[layer].astype(BF16), ffn1_w_out[layer].astype(BF16))
        if kind == 0:
            h = _nsa_layer(h, b, s, mix_norm[layer], nsa_w_in[j], nsa_w_out[j], nsa_q_norm[j],
                           nsa_k_norm[j], nsa_cmp_pos[j], nsa_cmp_w1[j], nsa_cmp_w2[j])
        elif kind == 1:
            lam_init = 0.8 - 0.6 * math.exp(-0.3 * layer)
            h = _diff_layer(h, b, s, mix_norm[layer], diff_w_in[j], diff_w_out[j], diff_q_norm[j],
                            diff_k_norm[j], diff_lambda[j], diff_subln[j], lam_init)
        elif kind == 2:
            h = _gdn_layer(h, b, s, mix_norm[layer], gdn_w_in[j], gdn_w_out[j], gdn_conv_w[j],
                           gdn_a_log[j], gdn_dt_bias[j], gdn_o_norm[j])
        else:
            h = _swa_layer(h, b, s, mix_norm[layer], swa_w_in[j], swa_w_out[j], swa_q_norm[j],
                           swa_k_norm[j], swa_sinks[j])
        h = _ffn(h, ffn2_norm[layer], ffn2_w_in[layer].astype(BF16), ffn2_w_out[layer].astype(BF16))
    return h.reshape(b, s, d)
```

```python
import functools
import math

import numpy as np
import jax
import jax.numpy as jnp
from jax import lax
from jax.experimental import pallas as pl
from jax.experimental.pallas import tpu as pltpu

F32 = jnp.float32
BF16 = jnp.bfloat16
HIGHEST = lax.Precision.HIGHEST

EPS = 1e-6
NEG = -1e30
M_INIT = -1e29
LOG2E = 1.4426950408889634
FORCE = 1e9

HEAD_DIM = 64
LANES = 128
VMEM_LIMIT = 56 * 1024 * 1024

NSA_HQ, NSA_HKV = 16, 4
CMP_BLOCK, CMP_STRIDE = 32, 16
SEL_BLOCK, SEL_TOPK = 64, 16
NSA_WINDOW = 512
DIFF_H = 8
GDN_H, GDN_DK, GDN_DV, GDN_CONV, GDN_CHUNK = 8, 128, 128, 4, 64
SWA_HQ, SWA_HKV, SWA_WINDOW = 16, 4, 128


def _params(sem):
    return pltpu.CompilerParams(dimension_semantics=sem, vmem_limit_bytes=VMEM_LIMIT)


def _nt(a, b, precision=None):
    return lax.dot_general(a, b, (((1,), (1,)), ((), ())), precision=precision,
                           preferred_element_type=F32)


def _tn(a, b, precision=None):
    return lax.dot_general(a, b, (((0,), (0,)), ((), ())), precision=precision,
                           preferred_element_type=F32)


def _rms(x, g):
    return x * lax.rsqrt(jnp.mean(x * x, axis=-1, keepdims=True) + EPS) * g


def _sigmoid(x):
    return 1.0 / (1.0 + jnp.exp(-x))


def _alibi_slopes(n):
    return [2.0 ** (-8.0 * (i + 1) / n) for i in range(n)]


def _ffn_body(x_ref, g_ref, win_ref, wout_ref, o_ref, *, d_ff, chunk):
    x = x_ref[...]
    xn = _rms(x, g_ref[...]).astype(BF16)
    acc = jnp.zeros(x.shape, F32)
    for c in range(d_ff // chunk):
        a = jnp.dot(xn, win_ref[:, c * chunk:(c + 1) * chunk], preferred_element_type=F32)
        b = jnp.dot(xn, win_ref[:, d_ff + c * chunk:d_ff + (c + 1) * chunk],
                    preferred_element_type=F32)
        h = (a * _sigmoid(a) * b).astype(BF16)
        acc = acc + jnp.dot(h, wout_ref[c * chunk:(c + 1) * chunk, :], preferred_element_type=F32)
    o_ref[...] = x + 0.5 * acc


def _ffn(x, g, w_in, w_out, tm=512, chunk=256):
    t, d = x.shape
    d_ff = w_out.shape[0]
    return pl.pallas_call(
        functools.partial(_ffn_body, d_ff=d_ff, chunk=chunk),
        grid=(t // tm,),
        in_specs=[pl.BlockSpec((tm, d), lambda i: (i, 0)),
                  pl.BlockSpec((1, d), lambda i: (0, 0)),
                  pl.BlockSpec(w_in.shape, lambda i: (0, 0), pipeline_mode=pl.Buffered(1)),
                  pl.BlockSpec(w_out.shape, lambda i: (0, 0), pipeline_mode=pl.Buffered(1))],
        out_specs=pl.BlockSpec((tm, d), lambda i: (i, 0)),
        out_shape=jax.ShapeDtypeStruct((t, d), F32),
        compiler_params=_params(("parallel",)),
        name="ffn",
    )(x, g.reshape(1, d), w_in, w_out)


def _norm_proj_body(x_ref, g_ref, *refs, n):
    xn = _rms(x_ref[...], g_ref[...]).astype(BF16)
    for w_ref, o_ref in zip(refs[:n], refs[n:]):
        o_ref[...] = jnp.dot(xn, w_ref[...], preferred_element_type=F32).astype(o_ref.dtype)


def _norm_proj(x, g, ws, tm=512):
    t, d = x.shape
    n = len(ws)
    return pl.pallas_call(
        functools.partial(_norm_proj_body, n=n),
        grid=(t // tm,),
        in_specs=[pl.BlockSpec((tm, d), lambda i: (i, 0)),
                  pl.BlockSpec((1, d), lambda i: (0, 0))]
                 + [pl.BlockSpec(w.shape, lambda i: (0, 0)) for w in ws],
        out_specs=[pl.BlockSpec((tm, w.shape[1]), lambda i: (i, 0)) for w in ws],
        out_shape=[jax.ShapeDtypeStruct((t, w.shape[1]), F32) for w in ws],
        compiler_params=_params(("parallel",)),
        name="norm_proj",
    )(x, g.reshape(1, d), *ws)


def _proj_res_body(x_ref, a_ref, w_ref, o_ref):
    o_ref[...] = x_ref[...] + jnp.dot(a_ref[...].astype(BF16), w_ref[...],
                                      preferred_element_type=F32)


def _proj_residual(x, a, w, tm=512):
    t, d = x.shape
    k = a.shape[1]
    return pl.pallas_call(
        _proj_res_body,
        grid=(t // tm,),
        in_specs=[pl.BlockSpec((tm, d), lambda i: (i, 0)),
                  pl.BlockSpec((tm, k), lambda i: (i, 0)),
                  pl.BlockSpec((k, d), lambda i: (0, 0))],
        out_specs=pl.BlockSpec((tm, d), lambda i: (i, 0)),
        out_shape=jax.ShapeDtypeStruct((t, d), F32),
        compiler_params=_params(("parallel",)),
        name="proj_residual",
    )(x, a, w)


def _head_prep_body(y_ref, g_ref, *refs, nh, norm, scale, has_extra):
    o_ref = refs[-1]
    for h in range(nh):
        yh = y_ref[0, :, h * HEAD_DIM:(h + 1) * HEAD_DIM]
        if norm:
            yh = _rms(yh, g_ref[...])
        if scale != 1.0:
            yh = yh * scale
        yh = yh.astype(o_ref.dtype)
        if has_extra:
            yh = jnp.concatenate([yh, refs[0][...]], axis=-1)
        o_ref[0, h] = yh


def _head_prep(y, gain, scale, out_dtype, extra=None, ts=512):
    b, s, c = y.shape
    nh = c // HEAD_DIM
    norm = gain is not None
    g = (gain if norm else jnp.ones((HEAD_DIM,), F32)).reshape(1, HEAD_DIM)
    width = HEAD_DIM if extra is None else 2 * HEAD_DIM
    in_specs = [pl.BlockSpec((1, ts, c), lambda bi, i: (bi, i, 0)),
                pl.BlockSpec((1, HEAD_DIM), lambda bi, i: (0, 0))]
    args = [y, g]
    if extra is not None:
        in_specs.append(pl.BlockSpec((ts, HEAD_DIM), lambda bi, i: (0, 0)))
        args.append(extra)
    return pl.pallas_call(
        functools.partial(_head_prep_body, nh=nh, norm=norm, scale=scale,
                          has_extra=extra is not None),
        grid=(b, s // ts),
        in_specs=in_specs,
        out_specs=pl.BlockSpec((1, nh, ts, width), lambda bi, i: (bi, 0, i, 0)),
        out_shape=jax.ShapeDtypeStruct((b, nh, s, width), out_dtype),
        compiler_params=_params(("parallel", "parallel")),
        name="head_prep",
    )(*args)


def _slope_rows(n_heads, n_kv, rows_per_head, mult=LOG2E):
    g = n_heads // n_kv
    sl = np.asarray(_alibi_slopes(n_heads), np.float32).reshape(n_kv, g, 1) * np.float32(mult)
    return jnp.asarray(np.repeat(sl, rows_per_head, axis=1).reshape(n_kv, g * rows_per_head, 1))


def _banded_body(*refs, g, tq, nprev, window, has_sink):
    q_ref = refs[0]
    k_refs = refs[1:2 + nprev]
    v_refs = refs[2 + nprev:3 + 2 * nprev]
    sl_ref = refs[3 + 2 * nprev]
    pos = 4 + 2 * nprev
    sink_ref = refs[pos] if has_sink else None
    o_ref = refs[pos + (1 if has_sink else 0)]
    i = pl.program_id(2)
    rows = g * tq
    span = (nprev + 1) * tq
    q = q_ref[0, 0].reshape(rows, HEAD_DIM)
    s = jnp.concatenate([_nt(q, k_ref[0, 0]) for k_ref in k_refs], axis=-1)
    r = lax.broadcasted_iota(jnp.int32, (rows, 1), 0) % tq
    c = lax.broadcasted_iota(jnp.int32, (1, span), 1)
    dist = r + nprev * tq - c
    valid = (dist >= 0) & (dist < window) & (c + (i - nprev) * tq >= 0)
    s = jnp.where(valid, s - sl_ref[0] * dist.astype(F32), NEG)
    m = jnp.max(s, axis=-1, keepdims=True)
    if has_sink:
        sink = sink_ref[0] * LOG2E
        m = jnp.maximum(m, sink)
    e = jnp.exp2(s - m)
    l = jnp.sum(e, axis=-1, keepdims=True)
    if has_sink:
        l = l + jnp.exp2(sink - m)
    p = (e / l).astype(BF16)
    o = jnp.zeros((rows, HEAD_DIM), F32)
    for j, v_ref in enumerate(v_refs):
        o = o + jnp.dot(p[:, j * tq:(j + 1) * tq], v_ref[0, 0], preferred_element_type=F32)
    o_ref[0] = jnp.concatenate([o[gi * tq:(gi + 1) * tq] for gi in range(g)], axis=-1)


def _banded_attention(q, k, v, n_heads, window, tq, sinks=None):
    b, hq, s, _ = q.shape
    hkv = k.shape[1]
    g = hq // hkv
    nprev = -(-window // tq)
    q5 = q.reshape(b, hkv, g, s, HEAD_DIM)
    kv_specs = [pl.BlockSpec((1, 1, tq, HEAD_DIM),
                             functools.partial(lambda bi, h, i, j: (bi, h, jnp.maximum(i - nprev + j, 0), 0), j=j))
                for j in range(nprev + 1)]
    in_specs = ([pl.BlockSpec((1, 1, g, tq, HEAD_DIM), lambda bi, h, i: (bi, h, 0, i, 0))]
                + kv_specs + kv_specs
                + [pl.BlockSpec((1, g * tq, 1), lambda bi, h, i: (h, 0, 0))])
    args = [q5] + [k] * (nprev + 1) + [v] * (nprev + 1) + [_slope_rows(n_heads, hkv, tq)]
    if sinks is not None:
        in_specs.append(pl.BlockSpec((1, g * tq, 1), lambda bi, h, i: (h, 0, 0)))
        args.append(jnp.repeat(sinks.astype(F32).reshape(hkv, g, 1), tq, axis=1).reshape(hkv, g * tq, 1))
    return pl.pallas_call(
        functools.partial(_banded_body, g=g, tq=tq, nprev=nprev, window=window,
                          has_sink=sinks is not None),
        grid=(b, hkv, s // tq),
        in_specs=in_specs,
        out_specs=pl.BlockSpec((1, tq, g * HEAD_DIM), lambda bi, h, i: (bi, i, h)),
        out_shape=jax.ShapeDtypeStruct((b, s, hq * HEAD_DIM), F32),
        compiler_params=_params(("parallel", "parallel", "parallel")),
        name="banded_attention",
    )(*args)


def _compress_body(t_ref, pt_ref, pb_ref, w1t_ref, w1b_ref, w2_ref, g_ref, o_ref, *, norm):
    sub = t_ref[0, 0]
    n_sub = sub.shape[0]
    a = jnp.dot((sub + pt_ref[...]).astype(BF16), w1t_ref[...], preferred_element_type=F32)
    bm = jnp.dot((sub + pb_ref[...]).astype(BF16), w1b_ref[...], preferred_element_type=F32)
    hid = a + pltpu.roll(bm, shift=n_sub - 1, axis=0)
    hid = 0.5 * hid * (1.0 + jnp.tanh(0.7978845608028654 * (hid + 0.044715 * hid * hid * hid)))
    out = jnp.dot(hid.astype(BF16), w2_ref[...], preferred_element_type=F32)
    if norm:
        out = _rms(out, g_ref[...])
    o_ref[0, 0] = out


def _compress(t, pos, w1, w2, gain):
    b, h, s, _ = t.shape
    n_sub = s // CMP_STRIDE
    half = CMP_STRIDE * HEAD_DIM
    hidden = w1.shape[1]
    norm = gain is not None
    g = (gain if norm else jnp.ones((HEAD_DIM,), F32)).reshape(1, HEAD_DIM)
    const = lambda shape: pl.BlockSpec(shape, lambda bi, hi: (0,) * len(shape))
    return pl.pallas_call(
        functools.partial(_compress_body, norm=norm),
        grid=(b, h),
        in_specs=[pl.BlockSpec((1, 1, n_sub, half), lambda bi, hi: (bi, hi, 0, 0)),
                  const((1, half)), const((1, half)),
                  const((half, hidden)), const((half, hidden)),
                  const((hidden, HEAD_DIM)), const((1, HEAD_DIM))],
        out_specs=pl.BlockSpec((1, 1, n_sub, HEAD_DIM), lambda bi, hi: (bi, hi, 0, 0)),
        out_shape=jax.ShapeDtypeStruct((b, h, n_sub, HEAD_DIM), F32),
        compiler_params=_params(("parallel", "parallel")),
        name="nsa_compress",
    )(t.reshape(b, h, n_sub, half),
      pos[:CMP_STRIDE].reshape(1, half), pos[CMP_STRIDE:].reshape(1, half),
      w1[:half].astype(BF16), w1[half:].astype(BF16), w2.astype(BF16), g)


def _cmp_select_body(q_ref, kc_ref, vc_ref, ov_ref, ch_ref, sl_ref, o_ref, sel_ref, need_ref,
                     *, g, tq, n_top):
    i = pl.program_id(2)
    rows = g * tq
    n_sub = kc_ref.shape[2]
    n_sel = ov_ref.shape[1]
    q = q_ref[0, 0].reshape(rows, HEAD_DIM)
    s = _nt(q, kc_ref[0, 0].astype(BF16))
    t_row = i * tq + lax.broadcasted_iota(jnp.int32, (rows, 1), 0) % tq
    cmp_end = lax.broadcasted_iota(jnp.int32, (1, n_sub), 1) * CMP_STRIDE + (CMP_BLOCK - 1)
    dist = t_row - cmp_end
    valid = dist >= 0
    s = jnp.where(valid, s - sl_ref[0] * dist.astype(F32), NEG)
    m = jnp.maximum(jnp.max(s, axis=-1, keepdims=True), M_INIT)
    e = jnp.exp2(s - m)
    l = jnp.sum(e, axis=-1, keepdims=True)
    p = e / jnp.where(l > 0.0, l, 1.0)
    o = jnp.dot(p.astype(BF16), vc_ref[0, 0].astype(BF16), preferred_element_type=F32)
    o_ref[0] = jnp.concatenate([o[gi * tq:(gi + 1) * tq] for gi in range(g)], axis=-1)

    psum = p[0:tq]
    for gi in range(1, g):
        psum = psum + p[gi * tq:(gi + 1) * tq]
    p_hi = psum.astype(BF16)
    r1 = psum - p_hi.astype(F32)
    p_mid = r1.astype(BF16)
    p_lo = (r1 - p_mid.astype(F32)).astype(BF16)
    ov = ov_ref[...]
    imp = (jnp.dot(p_hi, ov, preferred_element_type=F32)
           + jnp.dot(p_mid, ov, preferred_element_type=F32)
           + jnp.dot(p_lo, ov, preferred_element_type=F32))
    t_q = i * tq + lax.broadcasted_iota(jnp.int32, (tq, 1), 0)
    cur = t_q // SEL_BLOCK
    j = lax.broadcasted_iota(jnp.int32, (1, n_sel), 1)
    forced = (j == 0) | (j == cur) | (j == cur - 1)
    causal = j <= cur
    imp = jnp.where(forced, FORCE, jnp.where(causal, imp, -1.0))
    sel = jnp.zeros((tq, n_sel), F32)
    for _ in range(n_top):
        hit = j == jnp.argmax(imp, axis=-1, keepdims=True)
        sel = jnp.where(hit, 1.0, sel)
        imp = jnp.where(hit, -2.0, imp)
    sel = jnp.where(causal, sel, 0.0)
    sel_ref[0, 0] = sel.astype(sel_ref.dtype)
    any_row = jnp.broadcast_to(jnp.max(sel, axis=0, keepdims=True), (8, n_sel)).astype(BF16)
    hits = jnp.dot(any_row, ch_ref[...], preferred_element_type=F32)
    need_ref[0, 0, 0] = jnp.where(hits > 0.5, 1, 0).astype(jnp.int32)


def _cmp_select(q, kc, vc, tq, sel_tk):
    b, hq, s, _ = q.shape
    hkv = kc.shape[1]
    g = hq // hkv
    n_sub = kc.shape[2]
    n_sel = s // SEL_BLOCK
    n_top = min(SEL_TOPK, n_sel)
    assert s // sel_tk <= LANES
    cs = np.arange(n_sub)[:, None] * CMP_STRIDE
    ss = np.arange(n_sel)[None, :] * SEL_BLOCK
    overlap = ((cs < ss + SEL_BLOCK) & (cs + CMP_BLOCK > ss)).astype(np.float32)
    chunk_of = (np.arange(n_sel)[:, None] * SEL_BLOCK // sel_tk == np.arange(LANES)[None, :])
    return pl.pallas_call(
        functools.partial(_cmp_select_body, g=g, tq=tq, n_top=n_top),
        grid=(b, hkv, s // tq),
        in_specs=[pl.BlockSpec((1, 1, g, tq, HEAD_DIM), lambda bi, h, i: (bi, h, 0, i, 0)),
                  pl.BlockSpec((1, 1, n_sub, HEAD_DIM), lambda bi, h, i: (bi, h, 0, 0)),
                  pl.BlockSpec((1, 1, n_sub, HEAD_DIM), lambda bi, h, i: (bi, h, 0, 0)),
                  pl.BlockSpec((n_sub, n_sel), lambda bi, h, i: (0, 0)),
                  pl.BlockSpec((n_sel, LANES), lambda bi, h, i: (0, 0)),
                  pl.BlockSpec((1, g * tq, 1), lambda bi, h, i: (h, 0, 0))],
        out_specs=[pl.BlockSpec((1, tq, g * HEAD_DIM), lambda bi, h, i: (bi, i, h)),
                   pl.BlockSpec((1, 1, tq, n_sel), lambda bi, h, i: (bi, h, i, 0)),
                   pl.BlockSpec((1, 1, 1, 8, LANES), lambda bi, h, i: (bi, h, i, 0, 0))],
        out_shape=[jax.ShapeDtypeStruct((b, s, hq * HEAD_DIM), F32),
                   jax.ShapeDtypeStruct((b, hkv, s, n_sel), BF16),
                   jax.ShapeDtypeStruct((b, hkv, s // tq, 8, LANES), jnp.int32)],
        compiler_params=_params(("parallel", "parallel", "parallel")),
        name="nsa_cmp_select",
    )(q.reshape(b, hkv, g, s, HEAD_DIM), kc, vc, jnp.asarray(overlap, BF16),
      jnp.asarray(chunk_of.astype(np.float32), BF16), _slope_rows(hq, hkv, tq))


AUG_POS_HI, AUG_POS_LO, AUG_SEL = 0, 3, 6
MASK_BIG = 2.0 ** 100


def _sel_query_aug(n_heads, n_kv, tq):
    g = n_heads // n_kv
    sl = jnp.asarray(np.asarray(_alibi_slopes(n_heads), np.float32) * np.float32(LOG2E))
    hi = sl.astype(BF16).astype(F32)
    mid = (sl - hi).astype(BF16).astype(F32)
    lo = (sl - hi - mid).astype(BF16).astype(F32)
    pieces = jnp.stack([hi, mid, lo], axis=-1)
    cols = jnp.zeros((n_heads, HEAD_DIM), F32)
    cols = cols.at[:, AUG_POS_HI:AUG_POS_HI + 3].set(pieces * LANES)
    cols = cols.at[:, AUG_POS_LO:AUG_POS_LO + 3].set(pieces)
    cols = jnp.broadcast_to(cols.reshape(n_kv, g, 1, HEAD_DIM), (n_kv, g, tq, HEAD_DIM))
    return cols.reshape(n_kv, g * tq, HEAD_DIM).astype(BF16)


def _sel_key_aug(tk, rows):
    c = np.arange(rows) % tk
    cols = np.zeros((rows, HEAD_DIM), np.float32)
    cols[:, AUG_POS_HI:AUG_POS_HI + 3] = (c // 256 * 2)[:, None]
    cols[:, AUG_POS_LO:AUG_POS_LO + 3] = (c % 256)[:, None]
    cols[np.arange(rows), AUG_SEL + c // SEL_BLOCK] = MASK_BIG
    return jnp.asarray(cols, BF16)


def _sel_attn_body(need_ref, q_ref, qa_ref, k_ref, v_ref, sel_ref, o_ref,
                   qaug_ref, m_ref, acc_ref, *, g, tq, tk, n_chunks_total):
    bi, h, i = pl.program_id(0), pl.program_id(1), pl.program_id(2)
    n_sel = sel_ref.shape[3]
    per_chunk = tk // SEL_BLOCK
    rows = g * tq
    qaug_ref[...] = jnp.concatenate([q_ref[0, 0].reshape(rows, HEAD_DIM), qa_ref[0]], axis=-1)
    m_ref[...] = jnp.full(m_ref.shape, M_INIT, F32)
    acc_ref[...] = jnp.zeros(acc_ref.shape, F32)
    sel = sel_ref[0, 0]
    lane = lax.broadcasted_iota(jnp.int32, (1, LANES), 1)
    pos_lanes = (lane >= HEAD_DIM + AUG_POS_HI) & (lane < HEAD_DIM + AUG_POS_HI + 3)
    sel_lanes = (lane >= HEAD_DIM + AUG_SEL) & (lane < HEAD_DIM + AUG_SEL + per_chunk)
    jrow = lax.broadcasted_iota(jnp.int32, (n_sel, 1), 0)
    r = lax.broadcasted_iota(jnp.int32, (rows, 1), 0) % tq
    c = lax.broadcasted_iota(jnp.int32, (1, tk), 1)

    n_parts = 1

    def chunk(ci, diagonal):
        start = pl.multiple_of(ci * tk, tk)
        shift = (ci * (tk // LANES) - i * (tq // LANES)).astype(F32)
        k = k_ref[0, 0, pl.ds(start, tk), :] + jnp.where(pos_lanes, shift, 0.0).astype(BF16)
        v = v_ref[0, 0, pl.ds(start, tk), :]
        place = jnp.where(sel_lanes & (jrow == ci * per_chunk + lane - (HEAD_DIM + AUG_SEL)), 1.0, 0.0)
        picked = jnp.dot(sel, place.astype(BF16), preferred_element_type=F32)
        unpicked = jnp.where(sel_lanes, picked - 1.0, 0.0).astype(BF16)
        for part in range(n_parts):
            rs = slice(part * rows // n_parts, (part + 1) * rows // n_parts)
            s = _nt(qaug_ref[rs, :] + jnp.concatenate([unpicked] * (g // n_parts), axis=0), k)
            if diagonal:
                s = jnp.where(r[rs] + (i * tq - ci * tk) >= c, s, NEG)
            m_old = m_ref[rs, :]
            m_new = jnp.maximum(m_old, jnp.max(s, axis=-1, keepdims=True))
            e = jnp.exp2(s - pltpu.repeat(m_new, tk // LANES, axis=1))
            acc_ref[rs, :] = (jnp.exp2(m_old - m_new) * acc_ref[rs, :]
                              + jnp.dot(e.astype(BF16), v, preferred_element_type=F32))
            m_ref[rs, :] = m_new

    diag = (i * tq) // tk
    base = ((bi * pl.num_programs(1) + h) * pl.num_programs(2) + i) * n_chunks_total

    def maybe(ci, carry):
        @pl.when(need_ref[base + ci] != 0)
        def _():
            chunk(ci, False)
        return carry

    lax.fori_loop(0, diag, maybe, 0)
    chunk(diag, True)
    acc = acc_ref[...]
    o = acc[:, :HEAD_DIM] / acc[:, HEAD_DIM:HEAD_DIM + 1]
    o_ref[0] = jnp.concatenate([o[gi * tq:(gi + 1) * tq] for gi in range(g)], axis=-1)


def _sel_attention(q, k_aug, v, sel, need, tq, tk):
    b, hq, s, _ = q.shape
    hkv = k_aug.shape[1]
    g = hq // hkv
    n_sel = sel.shape[3]
    grid_spec = pltpu.PrefetchScalarGridSpec(
        num_scalar_prefetch=1,
        grid=(b, hkv, s // tq),
        in_specs=[pl.BlockSpec((1, 1, g, tq, HEAD_DIM), lambda bi, h, i, nd: (bi, h, 0, i, 0)),
                  pl.BlockSpec((1, g * tq, HEAD_DIM), lambda bi, h, i, nd: (h, 0, 0)),
                  pl.BlockSpec((1, 1, s, LANES), lambda bi, h, i, nd: (bi, h, 0, 0)),
                  pl.BlockSpec((1, 1, s, LANES), lambda bi, h, i, nd: (bi, h, 0, 0)),
                  pl.BlockSpec((1, 1, tq, n_sel), lambda bi, h, i, nd: (bi, h, i, 0))],
        out_specs=pl.BlockSpec((1, tq, g * HEAD_DIM), lambda bi, h, i, nd: (bi, i, h)),
        scratch_shapes=[pltpu.VMEM((g * tq, LANES), BF16),
                        pltpu.VMEM((g * tq, LANES), F32), pltpu.VMEM((g * tq, LANES), F32)])
    return pl.pallas_call(
        functools.partial(_sel_attn_body, g=g, tq=tq, tk=tk, n_chunks_total=s // tk),
        grid_spec=grid_spec,
        out_shape=jax.ShapeDtypeStruct((b, s, hq * HEAD_DIM), F32),
        compiler_params=_params(("parallel", "parallel", "arbitrary")),
        name="nsa_selected_attention",
    )(need, q.reshape(b, hkv, g, s, HEAD_DIM), _sel_query_aug(hq, hkv, tq), k_aug, v, sel)


def _nsa_out_body(x_ref, gt_ref, oc_ref, os_ref, ow_ref, ex_ref, w_ref, o_ref):
    sig = _sigmoid(gt_ref[...])
    mix = jnp.zeros(oc_ref.shape, F32)
    for j, br_ref in enumerate((oc_ref, os_ref, ow_ref)):
        gate = jnp.dot(sig, ex_ref[j], precision=HIGHEST, preferred_element_type=F32)
        mix = mix + gate * br_ref[...]
    o_ref[...] = x_ref[...] + jnp.dot(mix.astype(BF16), w_ref[...], preferred_element_type=F32)


def _nsa_out(x, gates, o_cmp, o_sel, o_win, w_out, tm=256):
    t, d = x.shape
    c = o_cmp.shape[1]
    expand = np.zeros((3, LANES, c), np.float32)
    for head in range(NSA_HQ):
        for j in range(3):
            expand[j, head * 3 + j, head * HEAD_DIM:(head + 1) * HEAD_DIM] = 1.0
    row = lambda width: pl.BlockSpec((tm, width), lambda i: (i, 0))
    return pl.pallas_call(
        _nsa_out_body,
        grid=(t // tm,),
        in_specs=[row(d), row(LANES), row(c), row(c), row(c),
                  pl.BlockSpec((3, LANES, c), lambda i: (0, 0, 0)),
                  pl.BlockSpec((c, d), lambda i: (0, 0))],
        out_specs=row(d),
        out_shape=jax.ShapeDtypeStruct((t, d), F32),
        compiler_params=_params(("parallel",)),
        name="nsa_out",
    )(x, gates, o_cmp, o_sel, o_win, jnp.asarray(expand), w_out)


def _nsa_layer(x, b, s, mix_norm, w_in, w_out, q_norm, k_norm, cmp_pos, cmp_w1, cmp_w2):
    dh = HEAD_DIM
    kvw = NSA_HKV * dh
    edges = np.cumsum([0, NSA_HQ * dh] + [kvw] * 6).tolist()
    ws = [w_in[:, edges[j]:edges[j + 1]].astype(BF16) for j in range(7)]
    n_gate = 3 * NSA_HQ
    ws.append(jnp.pad(w_in[:, edges[7]:edges[7] + n_gate], ((0, 0), (0, LANES - n_gate))).astype(BF16))
    q, kc, vc, ks, vs, kw, vw, gates = _norm_proj(x, mix_norm, ws)
    as3 = lambda y: y.reshape(b, s, y.shape[-1])
    qh = _head_prep(as3(q), q_norm, dh ** -0.5 * LOG2E, BF16)
    kch = _compress(_head_prep(as3(kc), None, 1.0, F32), cmp_pos[0], cmp_w1[0], cmp_w2[0], k_norm[0])
    vch = _compress(_head_prep(as3(vc), None, 1.0, F32), cmp_pos[1], cmp_w1[1], cmp_w2[1], None)
    sel_tq, sel_tk, prep_ts = 128, min(512, s), min(512, s)
    ksh = _head_prep(as3(ks), k_norm[1], 1.0, BF16, extra=_sel_key_aug(sel_tk, prep_ts), ts=prep_ts)
    ones_lane = jnp.zeros((prep_ts, HEAD_DIM), BF16).at[:, 0].set(1.0)
    vsh = _head_prep(as3(vs), None, 1.0, BF16, extra=ones_lane, ts=prep_ts)
    kwh = _head_prep(as3(kw), k_norm[2], 1.0, BF16)
    vwh = _head_prep(as3(vw), None, 1.0, BF16)
    o_cmp, sel, need = _cmp_select(qh, kch, vch, sel_tq, sel_tk)
    need = need[:, :, :, 0, :s // sel_tk].reshape(-1)
    o_sel = _sel_attention(qh, ksh, vsh, sel, need, sel_tq, sel_tk)
    o_win = _banded_attention(qh, kwh, vwh, NSA_HQ, NSA_WINDOW, tq=min(512, s))
    flat = lambda y: y.reshape(b * s, y.shape[-1])
    return _nsa_out(x, gates, flat(o_cmp), flat(o_sel), flat(o_win), w_out.astype(BF16))


def _diff_prep_body(y_ref, gq_ref, gk_ref, ka_ref, q_ref, k_ref, v_ref, *, width, q_scale):
    lane = lax.broadcasted_iota(jnp.int32, (1, LANES), 1)
    low = lane < HEAD_DIM

    def pair_norm(ysl, gain):
        sq = ysl * ysl
        ss_lo = jnp.sum(jnp.where(low, sq, 0.0), axis=-1, keepdims=True)
        ss_hi = jnp.sum(jnp.where(low, 0.0, sq), axis=-1, keepdims=True)
        inv = jnp.where(low, lax.rsqrt(ss_lo / HEAD_DIM + EPS), lax.rsqrt(ss_hi / HEAD_DIM + EPS))
        return ysl * inv * gain

    for j in range(width // LANES):
        sl = slice(j * LANES, (j + 1) * LANES)
        q_ref[:, sl] = (pair_norm(y_ref[:, sl], gq_ref[...]) * q_scale).astype(BF16)
        k_ref[:, 2 * j * LANES:(2 * j + 1) * LANES] = pair_norm(
            y_ref[:, width + j * LANES:width + (j + 1) * LANES], gk_ref[...]).astype(BF16)
        k_ref[:, (2 * j + 1) * LANES:(2 * j + 2) * LANES] = ka_ref[...]
        v_ref[:, sl] = y_ref[:, 2 * width + j * LANES:2 * width + (j + 1) * LANES].astype(BF16)


def _diff_prep(qkv, q_norm, k_norm, tk, tm=512):
    t, w3 = qkv.shape
    width = w3 // 3
    pair = lambda gvec: jnp.concatenate([gvec, gvec]).reshape(1, LANES).astype(F32)
    key_aug = jnp.pad(_sel_key_aug(tk, tm)[:, :AUG_SEL], ((0, 0), (0, LANES - AUG_SEL)))
    out = jax.ShapeDtypeStruct((t, width), BF16)
    return pl.pallas_call(
        functools.partial(_diff_prep_body, width=width, q_scale=HEAD_DIM ** -0.5 * LOG2E),
        grid=(t // tm,),
        in_specs=[pl.BlockSpec((tm, w3), lambda i: (i, 0)),
                  pl.BlockSpec((1, LANES), lambda i: (0, 0)),
                  pl.BlockSpec((1, LANES), lambda i: (0, 0)),
                  pl.BlockSpec((tm, LANES), lambda i: (0, 0))],
        out_specs=[pl.BlockSpec((tm, width), lambda i: (i, 0)),
                   pl.BlockSpec((tm, 2 * width), lambda i: (i, 0)),
                   pl.BlockSpec((tm, width), lambda i: (i, 0))],
        out_shape=[out, jax.ShapeDtypeStruct((t, 2 * width), BF16), out],
        compiler_params=_params(("parallel",)),
        name="diff_prep",
    )(qkv, pair(q_norm), pair(k_norm), key_aug)


def _diff_flash_body(reach_ref, q_ref, qa_ref, k_ref, v_ref, lam_ref, sub_ref, o_ref,
                     *, tq, tk, lam_init):
    h, i = pl.program_id(1), pl.program_id(2)
    q = q_ref[0]
    lane = lax.broadcasted_iota(jnp.int32, (1, LANES), 1)
    zero = jnp.zeros_like(q)
    q_both = jnp.concatenate([jnp.where(lane < HEAD_DIM, q, zero),
                              jnp.where(lane < HEAD_DIM, zero, q)], axis=0)
    q_aug = jnp.concatenate([q_both, qa_ref[0]], axis=-1)
    lane2 = lax.broadcasted_iota(jnp.int32, (1, 2 * LANES), 1)
    pos_lanes = (lane2 >= LANES + AUG_POS_HI) & (lane2 < LANES + AUG_POS_HI + 3)
    r = lax.broadcasted_iota(jnp.int32, (2 * tq, 1), 0) % tq
    c = lax.broadcasted_iota(jnp.int32, (1, tk), 1)

    def step(ci, carry, diagonal):
        m, l, acc = carry
        start = pl.multiple_of(ci * tk, tk)
        shift = (ci * (tk // LANES) - i * (tq // LANES)).astype(F32)
        k = k_ref[0, pl.ds(start, tk), :] + jnp.where(pos_lanes, shift, 0.0).astype(BF16)
        v = v_ref[0, pl.ds(start, tk), :]
        s = _nt(q_aug, k)
        if diagonal:
            s = jnp.where(r + (i * tq - ci * tk) >= c, s, NEG)
        m_new = jnp.maximum(m, jnp.max(s, axis=-1, keepdims=True))
        alpha = jnp.exp2(m - m_new)
        e = jnp.exp2(s - pltpu.repeat(m_new, tk // LANES, axis=1))
        l = alpha * l + jnp.sum(e, axis=-1, keepdims=True)
        acc = alpha * acc + jnp.dot(e.astype(BF16), v, preferred_element_type=F32)
        return m_new, l, acc

    init = (jnp.full((2 * tq, LANES), M_INIT, F32), jnp.zeros((2 * tq, LANES), F32),
            jnp.zeros((2 * tq, LANES), F32))
    diag = (i * tq) // tk
    first = jnp.maximum(i * tq - reach_ref[h], 0) // tk
    carry = lax.fori_loop(first, diag, lambda ci, cr: step(ci, cr, False), init)
    _, l, acc = step(diag, carry, True)
    lam = lam_ref[...]
    lam01 = jnp.sum(lam[0:1] * lam[1:2], axis=-1, keepdims=True)
    lam23 = jnp.sum(lam[2:3] * lam[3:4], axis=-1, keepdims=True)
    lmbda = jnp.exp(lam01) - jnp.exp(lam23) + lam_init
    o = acc / l
    o = o[:tq] - lmbda * o[tq:]
    o_ref[0] = _rms(o, sub_ref[...]) * (1.0 - lam_init)


def _diff_flash(qn, k_aug, vb, reach, lam, subln, lam_init, tq, tk):
    b, s, w = qn.shape
    n_heads = w // LANES
    sl = jnp.asarray(np.asarray(_alibi_slopes(n_heads), np.float32) * np.float32(LOG2E))
    hi = sl.astype(BF16).astype(F32)
    mid = (sl - hi).astype(BF16).astype(F32)
    lo = (sl - hi - mid).astype(BF16).astype(F32)
    pieces = jnp.stack([hi, mid, lo], axis=-1)
    q_aug = jnp.zeros((n_heads, LANES), F32)
    q_aug = q_aug.at[:, AUG_POS_HI:AUG_POS_HI + 3].set(pieces * LANES)
    q_aug = q_aug.at[:, AUG_POS_LO:AUG_POS_LO + 3].set(pieces)
    q_aug = jnp.broadcast_to(q_aug[:, None, :], (n_heads, 2 * tq, LANES)).astype(BF16)
    grid_spec = pltpu.PrefetchScalarGridSpec(
        num_scalar_prefetch=1,
        grid=(b, n_heads, s // tq),
        in_specs=[pl.BlockSpec((1, tq, LANES), lambda bi, h, i, rc: (bi, i, h)),
                  pl.BlockSpec((1, 2 * tq, LANES), lambda bi, h, i, rc: (h, 0, 0)),
                  pl.BlockSpec((1, s, 2 * LANES), lambda bi, h, i, rc: (bi, 0, h)),
                  pl.BlockSpec((1, s, LANES), lambda bi, h, i, rc: (bi, 0, h)),
                  pl.BlockSpec(lam.shape, lambda bi, h, i, rc: (0, 0)),
                  pl.BlockSpec((1, LANES), lambda bi, h, i, rc: (0, 0))],
        out_specs=pl.BlockSpec((1, tq, LANES), lambda bi, h, i, rc: (bi, i, h)))
    return pl.pallas_call(
        functools.partial(_diff_flash_body, tq=tq, tk=tk, lam_init=lam_init),
        grid_spec=grid_spec,
        out_shape=jax.ShapeDtypeStruct((b, s, w), F32),
        compiler_params=_params(("parallel", "parallel", "arbitrary")),
        name="diff_flash",
    )(reach, qn, q_aug, k_aug, vb, lam.astype(F32), subln.reshape(1, LANES).astype(F32))


def _diff_reach(q_norm, k_norm, n_heads, s):
    bound = (HEAD_DIM * jnp.max(jnp.abs(q_norm)) * jnp.max(jnp.abs(k_norm))
             * (HEAD_DIM ** -0.5 * LOG2E) * 1.02)
    slopes = jnp.asarray(np.asarray(_alibi_slopes(n_heads), np.float32) * np.float32(LOG2E))
    reach = jnp.ceil((2.0 * bound + 152.0) / slopes) + 1.0
    return jnp.minimum(reach, float(s)).astype(jnp.int32)


def _diff_layer(x, b, s, mix_norm, w_in, w_out, q_norm, k_norm, lam, subln, lam_init):
    tq, tk = min(256, s), min(512, s)
    (qkv,) = _norm_proj(x, mix_norm, [w_in.astype(BF16)])
    qn, k_aug, vb = _diff_prep(qkv, q_norm, k_norm, tk)
    as3 = lambda y: y.reshape(b, s, y.shape[-1])
    o = _diff_flash(as3(qn), as3(k_aug), as3(vb), _diff_reach(q_norm, k_norm, DIFF_H, s),
                    lam, subln, lam_init, tq, tk)
    return _proj_residual(x, o.reshape(b * s, -1), w_out.astype(BF16))


def _gdn_prep_body(cur_ref, prev_ref, cw_ref, b_ref, a_ref, alog_ref, dt_ref,
                   q_ref, k_ref, v_ref, beta_ref, gc_ref, pad_ref, *, ts, width):
    i = pl.program_id(1)
    halo = prev_ref.shape[1]
    pad_ref[0:halo, :] = jnp.where(i > 0, prev_ref[0], 0.0)
    pad_ref[halo:, :] = cur_ref[0]
    y = jnp.zeros((ts, 3 * width), F32)
    for j in range(GDN_CONV):
        off = halo - (GDN_CONV - 1) + j
        y = y + cw_ref[j:j + 1, :] * pad_ref[off:off + ts, :]
    y = y * _sigmoid(y)
    for h in range(width // GDN_DK):
        sl = slice(h * GDN_DK, (h + 1) * GDN_DK)
        qh = y[:, sl]
        q_ref[0, :, sl] = qh * lax.rsqrt(jnp.sum(qh * qh, axis=-1, keepdims=True) + EPS) * (GDN_DK ** -0.5)
        kh = y[:, width + h * GDN_DK:width + (h + 1) * GDN_DK]
        k_ref[0, :, sl] = kh * lax.rsqrt(jnp.sum(kh * kh, axis=-1, keepdims=True) + EPS)
    v_ref[0] = y[:, 2 * width:]
    beta_ref[0] = _sigmoid(b_ref[0])
    z = a_ref[0] + dt_ref[...]
    softplus = jnp.maximum(z, 0.0) + jnp.log(1.0 + jnp.exp(-jnp.abs(z)))
    gl = -jnp.exp(alog_ref[...]) * softplus
    r = lax.broadcasted_iota(jnp.int32, (ts, ts), 0)
    c = lax.broadcasted_iota(jnp.int32, (ts, ts), 1)
    tri = jnp.where((r // GDN_CHUNK == c // GDN_CHUNK) & (c <= r), 1.0, 0.0)
    gc_ref[0] = jnp.dot(tri, gl, precision=HIGHEST, preferred_element_type=F32)


def _gdn_prep(qkv, b_pad, a_pad, conv_w, a_log, dt_bias, ts=256):
    b, s, w3 = qkv.shape
    width = w3 // 3
    halo = 8
    pad_lane = lambda vec: jnp.pad(vec.astype(F32), (0, LANES - vec.shape[0])).reshape(1, LANES)
    big = jax.ShapeDtypeStruct((b, s, width), F32)
    small = jax.ShapeDtypeStruct((b, s, LANES), F32)
    tok = lambda wd: pl.BlockSpec((1, ts, wd), lambda bi, i: (bi, i, 0))
    return pl.pallas_call(
        functools.partial(_gdn_prep_body, ts=ts, width=width),
        grid=(b, s // ts),
        in_specs=[tok(w3),
                  pl.BlockSpec((1, halo, w3), lambda bi, i: (bi, jnp.maximum(i * (ts // halo) - 1, 0), 0)),
                  pl.BlockSpec((GDN_CONV, w3), lambda bi, i: (0, 0)),
                  tok(LANES), tok(LANES),
                  pl.BlockSpec((1, LANES), lambda bi, i: (0, 0)),
                  pl.BlockSpec((1, LANES), lambda bi, i: (0, 0))],
        out_specs=[tok(width), tok(width), tok(width), tok(LANES), tok(LANES)],
        out_shape=[big, big, big, small, small],
        scratch_shapes=[pltpu.VMEM((ts + halo, w3), F32)],
        compiler_params=_params(("parallel", "parallel")),
        name="gdn_prep",
    )(qkv, qkv, conv_w.astype(F32), b_pad, a_pad, pad_lane(a_log), pad_lane(dt_bias))


def _unit_lower_inverses(mats):
    n = mats[0].shape[0]
    r = lax.broadcasted_iota(jnp.int32, (n, n), 0)
    c = lax.broadcasted_iota(jnp.int32, (n, n), 1)
    eye = jnp.where(r == c, 1.0, 0.0)
    hdot = functools.partial(jnp.dot, precision=HIGHEST, preferred_element_type=F32)
    d = 1
    inv = None
    while d < n:
        below = (r // (2 * d) == c // (2 * d)) & (r // d != c // d)
        offs = [jnp.where(below, a, 0.0) for a in mats]
        if d == 1:
            inv = [eye - off for off in offs]
        else:
            left = [hdot(t, off) for t, off in zip(inv, offs)]
            inv = [t - hdot(lt, t) for t, lt in zip(inv, left)]
        d *= 2
    return inv


def _gdn_chunk_body(q_ref, k_ref, v_ref, beta_ref, gc_ref, gr_ref, gate_ref, on_ref, o_ref,
                    state_ref, *, nc, n_heads):
    @pl.when(pl.program_id(1) == 0)
    def _():
        state_ref[...] = jnp.zeros(state_ref.shape, F32)

    cl = GDN_CHUNK
    r = lax.broadcasted_iota(jnp.int32, (cl, cl), 0)
    c = lax.broadcasted_iota(jnp.int32, (cl, cl), 1)
    hdot = functools.partial(jnp.dot, precision=HIGHEST, preferred_element_type=F32)
    fdot = functools.partial(jnp.dot, preferred_element_type=F32)
    blocks = [(ch, h) for ch in range(nc) for h in range(n_heads)]
    rows = lambda ch: slice(ch * cl, (ch + 1) * cl)
    lanes = lambda h: slice(h * GDN_DK, (h + 1) * GDN_DK)

    qs = [q_ref[0, rows(ch), lanes(h)] for ch, h in blocks]
    ks = [k_ref[0, rows(ch), lanes(h)] for ch, h in blocks]
    bcol = [beta_ref[0, rows(ch), h:h + 1] for ch, h in blocks]
    gcol = [gc_ref[0, rows(ch), h:h + 1] for ch, h in blocks]
    glast = [gc_ref[0, ch * cl + cl - 1:ch * cl + cl, h:h + 1] for ch, h in blocks]
    decay = [jnp.exp(jnp.minimum(g - gr_ref[0, ch, h:h + 1, :], 0.0))
             for g, (ch, h) in zip(gcol, blocks)]
    kb = [k * b for k, b in zip(ks, bcol)]
    amat = [jnp.where(r > c, _nt(x, k, HIGHEST) * dc, 0.0) for x, k, dc in zip(kb, ks, decay)]
    inv = _unit_lower_inverses(amat)
    eg = [jnp.exp(g) for g in gcol]
    u = [hdot(t, v_ref[0, rows(ch), lanes(h)] * b) for t, b, (ch, h) in zip(inv, bcol, blocks)]
    w = [hdot(t, x * e) for t, x, e in zip(inv, kb, eg)]
    attn = [jnp.where(r >= c, _nt(q, k) * dc, 0.0) for q, k, dc in zip(qs, ks, decay)]

    for ch in range(nc):
        ids = range(ch * n_heads, (ch + 1) * n_heads)
        state = [state_ref[h] for h in range(n_heads)]
        v_new = [u[i] - fdot(w[i], st) for i, st in zip(ids, state)]
        o = [fdot(qs[i] * eg[i], st) + fdot(attn[i], vn) for i, st, vn in zip(ids, state, v_new)]
        for h, (i, st, vn) in enumerate(zip(ids, state, v_new)):
            state_ref[h] = st * jnp.exp(glast[i]) + _tn(ks[i] * jnp.exp(glast[i] - gcol[i]), vn)
        for h in range(n_heads):
            gate = gate_ref[0, rows(ch), lanes(h)]
            o_ref[0, rows(ch), lanes(h)] = _rms(o[h], on_ref[...]) * (gate * _sigmoid(gate))


def _gdn_chunks(q, k, v, beta, gc, gc_rows, gate, o_norm, nc=1):
    b, s, w = q.shape
    n_heads = w // GDN_DK
    ts = nc * GDN_CHUNK
    tok = lambda wd: pl.BlockSpec((1, ts, wd), lambda bi, i: (bi, i, 0))
    return pl.pallas_call(
        functools.partial(_gdn_chunk_body, nc=nc, n_heads=n_heads),
        grid=(b, s // ts),
        in_specs=[tok(w), tok(w), tok(w), tok(LANES), tok(LANES),
                  pl.BlockSpec((1, nc, 8, GDN_CHUNK), lambda bi, i: (bi, i, 0, 0)),
                  tok(w),
                  pl.BlockSpec((1, GDN_DV), lambda bi, i: (0, 0))],
        out_specs=tok(w),
        out_shape=jax.ShapeDtypeStruct((b, s, w), F32),
        scratch_shapes=[pltpu.VMEM((n_heads, GDN_DK, GDN_DV), F32)],
        compiler_params=_params(("parallel", "arbitrary")),
        name="gdn_chunks",
    )(q, k, v, beta, gc, gc_rows, gate, o_norm.reshape(1, GDN_DV).astype(F32))


def _gdn_layer(x, b, s, mix_norm, w_in, w_out, conv_w, a_log, dt_bias, o_norm):
    qkv_w = GDN_H * (2 * GDN_DK + GDN_DV)
    lane_pad = lambda wcols: jnp.pad(wcols, ((0, 0), (0, LANES - wcols.shape[1]))).astype(BF16)
    ws = [w_in[:, :qkv_w].astype(BF16),
          lane_pad(w_in[:, qkv_w:qkv_w + GDN_H]),
          lane_pad(w_in[:, qkv_w + GDN_H:qkv_w + 2 * GDN_H]),
          w_in[:, qkv_w + 2 * GDN_H:].astype(BF16)]
    qkv, b_pad, a_pad, gate = _norm_proj(x, mix_norm, ws)
    as3 = lambda y: y.reshape(b, s, y.shape[-1])
    q, k, v, beta, gc = _gdn_prep(as3(qkv), as3(b_pad), as3(a_pad), conv_w, a_log, dt_bias,
                                  ts=min(256, s))
    gc_rows = gc[:, :, :8].reshape(b, s // GDN_CHUNK, GDN_CHUNK, 8).transpose(0, 1, 3, 2)
    o = _gdn_chunks(q, k, v, beta, gc, gc_rows, as3(gate), o_norm)
    return _proj_residual(x, o.reshape(b * s, -1), w_out.astype(BF16))


def _swa_layer(x, b, s, mix_norm, w_in, w_out, q_norm, k_norm, sinks):
    dh = HEAD_DIM
    e0, e1 = SWA_HQ * dh, (SWA_HQ + SWA_HKV) * dh
    ws = [w_in[:, :e0].astype(BF16), w_in[:, e0:e1].astype(BF16), w_in[:, e1:].astype(BF16)]
    q, k, v = _norm_proj(x, mix_norm, ws)
    as3 = lambda y: y.reshape(b, s, y.shape[-1])
    qh = _head_prep(as3(q), q_norm, dh ** -0.5 * LOG2E, BF16)
    kh = _head_prep(as3(k), k_norm, 1.0, BF16)
    vh = _head_prep(as3(v), None, 1.0, BF16)
    o = _banded_attention(qh, kh, vh, SWA_HQ, SWA_WINDOW, tq=min(256, s), sinks=sinks)
    return _proj_residual(x, o.reshape(b * s, -1), w_out.astype(BF16))


def kernel(x, ffn1_norm, ffn1_w_in, ffn1_w_out, mix_norm, ffn2_norm, ffn2_w_in, ffn2_w_out,
           nsa_w_in, nsa_w_out, nsa_q_norm, nsa_k_norm, nsa_cmp_pos, nsa_cmp_w1, nsa_cmp_w2,
           diff_w_in, diff_w_out, diff_q_norm, diff_k_norm, diff_lambda, diff_subln,
           gdn_w_in, gdn_w_out, gdn_conv_w, gdn_a_log, gdn_dt_bias, gdn_o_norm,
           swa_w_in, swa_w_out, swa_q_norm, swa_k_norm, swa_sinks):
    b, s, d = x.shape
    depth = ffn1_norm.shape[0]
    n_mixers = 4
    h = x.reshape(b * s, d)
    for layer in range(depth):
        kind, j = layer % n_mixers, layer // n_mixers
        h = _ffn(h, ffn1_norm[layer], ffn1_w_in[layer].astype(BF16), ffn1_w_out[layer].astype(BF16))
        if kind == 0:
            h = _nsa_layer(h, b, s, mix_norm[layer], nsa_w_in[j], nsa_w_out[j], nsa_q_norm[j],
                           nsa_k_norm[j], nsa_cmp_pos[j], nsa_cmp_w1[j], nsa_cmp_w2[j])
        elif kind == 1:
            lam_init = 0.8 - 0.6 * math.exp(-0.3 * layer)
            h = _diff_layer(h, b, s, mix_norm[layer], diff_w_in[j], diff_w_out[j], diff_q_norm[j],
                            diff_k_norm[j], diff_lambda[j], diff_subln[j], lam_init)
        elif kind == 2:
            h = _gdn_layer(h, b, s, mix_norm[layer], gdn_w_in[j], gdn_w_out[j], gdn_conv_w[j],
                           gdn_a_log[j], gdn_dt_bias[j], gdn_o_norm[j])
        else:
            h = _swa_layer(h, b, s, mix_norm[layer], swa_w_in[j], swa_w_out[j], swa_q_norm[j],
                           swa_k_norm[j], swa_sinks[j])
        h = _ffn(h, ffn2_norm[layer], ffn2_w_in[layer].astype(BF16), ffn2_w_out[layer].astype(BF16))
    return h.reshape(b, s, d)
```

```python
import functools
import math

import numpy as np
import jax
import jax.numpy as jnp
from jax import lax
from jax.experimental import pallas as pl
from jax.experimental.pallas import tpu as pltpu

F32 = jnp.float32
BF16 = jnp.bfloat16
HIGHEST = lax.Precision.HIGHEST

EPS = 1e-6
NEG = -1e30
M_INIT = -1e29
LOG2E = 1.4426950408889634
FORCE = 1e9

HEAD_DIM = 64
LANES = 128
VMEM_LIMIT = 56 * 1024 * 1024

NSA_HQ, NSA_HKV = 16, 4
CMP_BLOCK, CMP_STRIDE = 32, 16
SEL_BLOCK, SEL_TOPK = 64, 16
NSA_WINDOW = 512
DIFF_H = 8
GDN_H, GDN_DK, GDN_DV, GDN_CONV, GDN_CHUNK = 8, 128, 128, 4, 64
SWA_HQ, SWA_HKV, SWA_WINDOW = 16, 4, 128


def _params(sem):
    return pltpu.CompilerParams(dimension_semantics=sem, vmem_limit_bytes=VMEM_LIMIT)


def _nt(a, b, precision=None):
    return lax.dot_general(a, b, (((1,), (1,)), ((), ())), precision=precision,
                           preferred_element_type=F32)


def _tn(a, b, precision=None):
    return lax.dot_general(a, b, (((0,), (0,)), ((), ())), precision=precision,
                           preferred_element_type=F32)


def _dot3(a, b, contract=((1,), (0,))):
    a_hi, b_hi = a.astype(BF16), b.astype(BF16)
    a_lo = (a - a_hi.astype(F32)).astype(BF16)
    b_lo = (b - b_hi.astype(F32)).astype(BF16)
    dot = lambda x, y: lax.dot_general(x, y, (contract, ((), ())), preferred_element_type=F32)
    return dot(a_hi, b_hi) + dot(a_hi, b_lo) + dot(a_lo, b_hi)


def _rms(x, g):
    return x * lax.rsqrt(jnp.mean(x * x, axis=-1, keepdims=True) + EPS) * g


def _sigmoid(x):
    return 1.0 / (1.0 + jnp.exp(-x))


def _alibi_slopes(n):
    return [2.0 ** (-8.0 * (i + 1) / n) for i in range(n)]


def _ffn_body(x_ref, g_ref, win_ref, wout_ref, o_ref, *, d_ff, chunk):
    x = x_ref[...]
    xn = _rms(x, g_ref[...]).astype(BF16)
    acc = jnp.zeros(x.shape, F32)
    for c in range(d_ff // chunk):
        a = jnp.dot(xn, win_ref[:, c * chunk:(c + 1) * chunk], preferred_element_type=F32)
        b = jnp.dot(xn, win_ref[:, d_ff + c * chunk:d_ff + (c + 1) * chunk],
                    preferred_element_type=F32)
        h = (a * _sigmoid(a) * b).astype(BF16)
        acc = acc + jnp.dot(h, wout_ref[c * chunk:(c + 1) * chunk, :], preferred_element_type=F32)
    o_ref[...] = x + 0.5 * acc


def _ffn(x, g, w_in, w_out, tm=512, chunk=256):
    t, d = x.shape
    d_ff = w_out.shape[0]
    return pl.pallas_call(
        functools.partial(_ffn_body, d_ff=d_ff, chunk=chunk),
        grid=(t // tm,),
        in_specs=[pl.BlockSpec((tm, d), lambda i: (i, 0)),
                  pl.BlockSpec((1, d), lambda i: (0, 0)),
                  pl.BlockSpec(w_in.shape, lambda i: (0, 0), pipeline_mode=pl.Buffered(1)),
                  pl.BlockSpec(w_out.shape, lambda i: (0, 0), pipeline_mode=pl.Buffered(1))],
        out_specs=pl.BlockSpec((tm, d), lambda i: (i, 0)),
        out_shape=jax.ShapeDtypeStruct((t, d), F32),
        compiler_params=_params(("parallel",)),
        name="ffn",
    )(x, g.reshape(1, d), w_in, w_out)


def _norm_proj_body(x_ref, g_ref, *refs, n):
    xn = _rms(x_ref[...], g_ref[...]).astype(BF16)
    for w_ref, o_ref in zip(refs[:n], refs[n:]):
        o_ref[...] = jnp.dot(xn, w_ref[...], preferred_element_type=F32).astype(o_ref.dtype)


def _norm_proj(x, g, ws, tm=512):
    t, d = x.shape
    n = len(ws)
    return pl.pallas_call(
        functools.partial(_norm_proj_body, n=n),
        grid=(t // tm,),
        in_specs=[pl.BlockSpec((tm, d), lambda i: (i, 0)),
                  pl.BlockSpec((1, d), lambda i: (0, 0))]
                 + [pl.BlockSpec(w.shape, lambda i: (0, 0)) for w in ws],
        out_specs=[pl.BlockSpec((tm, w.shape[1]), lambda i: (i, 0)) for w in ws],
        out_shape=[jax.ShapeDtypeStruct((t, w.shape[1]), F32) for w in ws],
        compiler_params=_params(("parallel",)),
        name="norm_proj",
    )(x, g.reshape(1, d), *ws)


def _proj_res_body(x_ref, a_ref, w_ref, o_ref):
    o_ref[...] = x_ref[...] + jnp.dot(a_ref[...].astype(BF16), w_ref[...],
                                      preferred_element_type=F32)


def _proj_residual(x, a, w, tm=512):
    t, d = x.shape
    k = a.shape[1]
    return pl.pallas_call(
        _proj_res_body,
        grid=(t // tm,),
        in_specs=[pl.BlockSpec((tm, d), lambda i: (i, 0)),
                  pl.BlockSpec((tm, k), lambda i: (i, 0)),
                  pl.BlockSpec((k, d), lambda i: (0, 0))],
        out_specs=pl.BlockSpec((tm, d), lambda i: (i, 0)),
        out_shape=jax.ShapeDtypeStruct((t, d), F32),
        compiler_params=_params(("parallel",)),
        name="proj_residual",
    )(x, a, w)


def _head_prep_body(y_ref, g_ref, *refs, nh, norm, scale, has_extra):
    o_ref = refs[-1]
    for h in range(nh):
        yh = y_ref[0, :, h * HEAD_DIM:(h + 1) * HEAD_DIM]
        if norm:
            yh = _rms(yh, g_ref[...])
        if scale != 1.0:
            yh = yh * scale
        yh = yh.astype(o_ref.dtype)
        if has_extra:
            yh = jnp.concatenate([yh, refs[0][...]], axis=-1)
        o_ref[0, h] = yh


def _head_prep(y, gain, scale, out_dtype, extra=None, ts=512):
    b, s, c = y.shape
    nh = c // HEAD_DIM
    norm = gain is not None
    g = (gain if norm else jnp.ones((HEAD_DIM,), F32)).reshape(1, HEAD_DIM)
    width = HEAD_DIM if extra is None else 2 * HEAD_DIM
    in_specs = [pl.BlockSpec((1, ts, c), lambda bi, i: (bi, i, 0)),
                pl.BlockSpec((1, HEAD_DIM), lambda bi, i: (0, 0))]
    args = [y, g]
    if extra is not None:
        in_specs.append(pl.BlockSpec((ts, HEAD_DIM), lambda bi, i: (0, 0)))
        args.append(extra)
    return pl.pallas_call(
        functools.partial(_head_prep_body, nh=nh, norm=norm, scale=scale,
                          has_extra=extra is not None),
        grid=(b, s // ts),
        in_specs=in_specs,
        out_specs=pl.BlockSpec((1, nh, ts, width), lambda bi, i: (bi, 0, i, 0)),
        out_shape=jax.ShapeDtypeStruct((b, nh, s, width), out_dtype),
        compiler_params=_params(("parallel", "parallel")),
        name="head_prep",
    )(*args)


def _slope_rows(n_heads, n_kv, rows_per_head, mult=LOG2E):
    g = n_heads // n_kv
    sl = np.asarray(_alibi_slopes(n_heads), np.float32).reshape(n_kv, g, 1) * np.float32(mult)
    return jnp.asarray(np.repeat(sl, rows_per_head, axis=1).reshape(n_kv, g * rows_per_head, 1))


def _banded_body(*refs, g, tq, nprev, has_sink):
    q_ref, qa_ref = refs[0], refs[1]
    k_refs = refs[2:3 + nprev]
    v_refs = refs[3 + nprev:4 + 2 * nprev]
    mask_ref = refs[4 + 2 * nprev]
    o_ref = refs[-1]
    i = pl.program_id(2)
    rows = g * tq
    q_aug = jnp.concatenate([q_ref[0, 0].reshape(rows, HEAD_DIM), qa_ref[0]], axis=-1)
    lane = lax.broadcasted_iota(jnp.int32, (1, LANES), 1)
    pos_lanes = (lane >= HEAD_DIM + AUG_POS_HI) & (lane < HEAD_DIM + AUG_POS_HI + 3)
    parts = []
    for j, k_ref in enumerate(k_refs):
        shift = float((j - nprev) * (tq // LANES))
        k = k_ref[0, 0] + jnp.where(pos_lanes, shift, 0.0).astype(BF16)
        bias = mask_ref[j] + jnp.where(i - nprev + j >= 0, 0.0, NEG)
        parts.append(_nt(q_aug, k) + jnp.concatenate([bias] * g, axis=0))
    s = jnp.concatenate(parts, axis=-1)
    m = jnp.max(s, axis=-1, keepdims=True)
    if has_sink:
        sl_ref, sink_ref = refs[5 + 2 * nprev], refs[6 + 2 * nprev]
        r = (lax.broadcasted_iota(jnp.int32, (rows, 1), 0) % tq).astype(F32)
        sink = sink_ref[0] * LOG2E + sl_ref[0] * r
        m = jnp.maximum(m, sink)
    e = jnp.exp2(s - m).astype(BF16)
    acc = jnp.zeros((rows, LANES), F32)
    for j, v_ref in enumerate(v_refs):
        acc = acc + jnp.dot(e[:, j * tq:(j + 1) * tq], v_ref[0, 0], preferred_element_type=F32)
    l = acc[:, HEAD_DIM:HEAD_DIM + 1]
    if has_sink:
        l = l + jnp.exp2(sink - m)
    o = acc[:, :HEAD_DIM] / l
    o_ref[0] = jnp.concatenate([o[gi * tq:(gi + 1) * tq] for gi in range(g)], axis=-1)


def _key_position_lanes(period, rows):
    c = np.arange(rows) % period
    cols = np.zeros((rows, HEAD_DIM), np.float32)
    cols[:, AUG_POS_HI:AUG_POS_HI + 3] = (c // 256 * 2)[:, None]
    cols[:, AUG_POS_LO:AUG_POS_LO + 3] = (c % 256)[:, None]
    return jnp.asarray(cols, BF16)


def _ones_lane(rows):
    return jnp.zeros((rows, HEAD_DIM), BF16).at[:, 0].set(1.0)


def _banded_attention(q, k_aug, v_aug, n_heads, window, tq, sinks=None):
    b, hq, s, _ = q.shape
    hkv = k_aug.shape[1]
    g = hq // hkv
    nprev = -(-window // tq)
    r = np.arange(tq)[:, None]
    c = np.arange(tq)[None, :]
    dist = np.stack([r + (nprev - j) * tq - c for j in range(nprev + 1)])
    mask = np.where((dist >= 0) & (dist < window), 0.0, NEG).astype(np.float32)
    q5 = q.reshape(b, hkv, g, s, HEAD_DIM)
    kv_specs = [pl.BlockSpec((1, 1, tq, LANES),
                             functools.partial(lambda bi, h, i, j: (bi, h, jnp.maximum(i - nprev + j, 0), 0), j=j))
                for j in range(nprev + 1)]
    in_specs = ([pl.BlockSpec((1, 1, g, tq, HEAD_DIM), lambda bi, h, i: (bi, h, 0, i, 0)),
                 pl.BlockSpec((1, g * tq, HEAD_DIM), lambda bi, h, i: (h, 0, 0))]
                + kv_specs + kv_specs
                + [pl.BlockSpec((nprev + 1, tq, tq), lambda bi, h, i: (0, 0, 0))])
    args = ([q5, _sel_query_aug(n_heads, hkv, tq)] + [k_aug] * (nprev + 1) + [v_aug] * (nprev + 1)
            + [jnp.asarray(mask)])
    if sinks is not None:
        in_specs += [pl.BlockSpec((1, g * tq, 1), lambda bi, h, i: (h, 0, 0))] * 2
        args += [_slope_rows(n_heads, hkv, tq),
                 jnp.repeat(sinks.astype(F32).reshape(hkv, g, 1), tq, axis=1).reshape(hkv, g * tq, 1)]
    return pl.pallas_call(
        functools.partial(_banded_body, g=g, tq=tq, nprev=nprev, has_sink=sinks is not None),
        grid=(b, hkv, s // tq),
        in_specs=in_specs,
        out_specs=pl.BlockSpec((1, tq, g * HEAD_DIM), lambda bi, h, i: (bi, i, h)),
        out_shape=jax.ShapeDtypeStruct((b, s, hq * HEAD_DIM), F32),
        compiler_params=_params(("parallel", "parallel", "parallel")),
        name="banded_attention",
    )(*args)


def _compress_body(t_ref, pt_ref, pb_ref, w1t_ref, w1b_ref, w2_ref, g_ref, *refs, norm):
    o_ref = refs[-1]
    sub = t_ref[0, 0]
    n_sub = sub.shape[0]
    a = jnp.dot((sub + pt_ref[...]).astype(BF16), w1t_ref[...], preferred_element_type=F32)
    bm = jnp.dot((sub + pb_ref[...]).astype(BF16), w1b_ref[...], preferred_element_type=F32)
    hid = a + pltpu.roll(bm, shift=n_sub - 1, axis=0)
    hid = 0.5 * hid * (1.0 + jnp.tanh(0.7978845608028654 * (hid + 0.044715 * hid * hid * hid)))
    out = jnp.dot(hid.astype(BF16), w2_ref[...], preferred_element_type=F32)
    if norm:
        out = _rms(out, g_ref[...])
    out = out.astype(BF16)
    if len(refs) == 2:
        out = jnp.concatenate([out, refs[0][...]], axis=-1)
    o_ref[0, 0] = out


def _compress(t, pos, w1, w2, gain, extra=None):
    b, h, s, _ = t.shape
    n_sub = s // CMP_STRIDE
    half = CMP_STRIDE * HEAD_DIM
    hidden = w1.shape[1]
    norm = gain is not None
    g = (gain if norm else jnp.ones((HEAD_DIM,), F32)).reshape(1, HEAD_DIM)
    const = lambda shape: pl.BlockSpec(shape, lambda bi, hi: (0,) * len(shape))
    width = HEAD_DIM if extra is None else 2 * HEAD_DIM
    extras = [] if extra is None else [extra]
    return pl.pallas_call(
        functools.partial(_compress_body, norm=norm),
        grid=(b, h),
        in_specs=[pl.BlockSpec((1, 1, n_sub, half), lambda bi, hi: (bi, hi, 0, 0)),
                  const((1, half)), const((1, half)),
                  const((half, hidden)), const((half, hidden)),
                  const((hidden, HEAD_DIM)), const((1, HEAD_DIM))]
                 + [const((n_sub, HEAD_DIM)) for _ in extras],
        out_specs=pl.BlockSpec((1, 1, n_sub, width), lambda bi, hi: (bi, hi, 0, 0)),
        out_shape=jax.ShapeDtypeStruct((b, h, n_sub, width), BF16),
        compiler_params=_params(("parallel", "parallel")),
        name="nsa_compress",
    )(t.reshape(b, h, n_sub, half),
      pos[:CMP_STRIDE].reshape(1, half), pos[CMP_STRIDE:].reshape(1, half),
      w1[:half].astype(BF16), w1[half:].astype(BF16), w2.astype(BF16), g, *extras)


def _cmp_key_lanes(n_sub):
    n = np.arange(n_sub)
    cols = np.zeros((n_sub, HEAD_DIM), np.float32)
    cols[:, AUG_POS_HI:AUG_POS_HI + 3] = (n // 8)[:, None]
    cols[:, AUG_POS_LO:AUG_POS_LO + 3] = (CMP_STRIDE * (n % 8) + CMP_BLOCK - 1)[:, None]
    return jnp.asarray(cols, BF16)


def _cmp_attn_body(q_ref, qa_ref, kc_ref, vc_ref, ov_ref, o_ref, imp_ref, *, g, tq, widths):
    i = pl.program_id(2)
    rows = g * tq
    q_aug = jnp.concatenate([q_ref[0, 0].reshape(rows, HEAD_DIM), qa_ref[0]], axis=-1)
    lane = lax.broadcasted_iota(jnp.int32, (1, LANES), 1)
    pos_lanes = (lane >= HEAD_DIM + AUG_POS_HI) & (lane < HEAD_DIM + AUG_POS_HI + 3)
    shift = jnp.where(pos_lanes, (-i * (tq // LANES)).astype(F32), 0.0).astype(BF16)
    t_row = i * tq + lax.broadcasted_iota(jnp.int32, (rows, 1), 0) % tq
    n_visible = (i * tq + tq) // CMP_STRIDE - 1

    def attend(n_cols):
        s = _nt(q_aug, kc_ref[0, 0, :n_cols, :] + shift)
        cmp_end = lax.broadcasted_iota(jnp.int32, (1, n_cols), 1) * CMP_STRIDE + (CMP_BLOCK - 1)
        s = jnp.where(t_row >= cmp_end, s, NEG)
        m = jnp.maximum(jnp.max(s, axis=-1, keepdims=True), M_INIT)
        e = jnp.exp2(s - m)
        l = jnp.sum(e, axis=-1, keepdims=True)
        p = e / jnp.where(l > 0.0, l, 1.0)
        o = jnp.dot(p.astype(BF16), vc_ref[0, 0, :n_cols, :], preferred_element_type=F32)
        o_ref[0] = jnp.concatenate([o[gi * tq:(gi + 1) * tq] for gi in range(g)], axis=-1)
        psum = p[0:tq]
        for gi in range(1, g):
            psum = psum + p[gi * tq:(gi + 1) * tq]
        p_hi = psum.astype(BF16)
        r1 = psum - p_hi.astype(F32)
        p_mid = r1.astype(BF16)
        p_lo = (r1 - p_mid.astype(F32)).astype(BF16)
        ov = ov_ref[:n_cols, :]
        imp_ref[0, 0] = (jnp.dot(p_hi, ov, preferred_element_type=F32)
                         + jnp.dot(p_mid, ov, preferred_element_type=F32)
                         + jnp.dot(p_lo, ov, preferred_element_type=F32))

    for lo, hi in zip((0,) + widths[:-1], widths):
        pl.when((n_visible > lo) & (n_visible <= hi) if lo else n_visible <= hi)(
            functools.partial(attend, hi))


def _cmp_attention(q, kc_aug, vc, tq):
    b, hq, s, _ = q.shape
    hkv = kc_aug.shape[1]
    g = hq // hkv
    n_sub = kc_aug.shape[2]
    n_sel = s // SEL_BLOCK
    step = 2 * LANES
    widths = tuple(range(step, n_sub, step)) + (n_sub,) if n_sub >= 2 * step else (n_sub,)
    cs = np.arange(n_sub)[:, None] * CMP_STRIDE
    ss = np.arange(n_sel)[None, :] * SEL_BLOCK
    overlap = ((cs < ss + SEL_BLOCK) & (cs + CMP_BLOCK > ss)).astype(np.float32)
    return pl.pallas_call(
        functools.partial(_cmp_attn_body, g=g, tq=tq, widths=widths),
        grid=(b, hkv, s // tq),
        in_specs=[pl.BlockSpec((1, 1, g, tq, HEAD_DIM), lambda bi, h, i: (bi, h, 0, i, 0)),
                  pl.BlockSpec((1, g * tq, HEAD_DIM), lambda bi, h, i: (h, 0, 0)),
                  pl.BlockSpec((1, 1, n_sub, LANES), lambda bi, h, i: (bi, h, 0, 0)),
                  pl.BlockSpec((1, 1, n_sub, HEAD_DIM), lambda bi, h, i: (bi, h, 0, 0)),
                  pl.BlockSpec((n_sub, n_sel), lambda bi, h, i: (0, 0))],
        out_specs=[pl.BlockSpec((1, tq, g * HEAD_DIM), lambda bi, h, i: (bi, i, h)),
                   pl.BlockSpec((1, 1, tq, n_sel), lambda bi, h, i: (bi, h, i, 0))],
        out_shape=[jax.ShapeDtypeStruct((b, s, hq * HEAD_DIM), F32),
                   jax.ShapeDtypeStruct((b, hkv, s, n_sel), F32)],
        compiler_params=_params(("parallel", "parallel", "parallel")),
        name="nsa_cmp_attention",
    )(q.reshape(b, hkv, g, s, HEAD_DIM), _sel_query_aug(hq, hkv, tq), kc_aug, vc,
      jnp.asarray(overlap, BF16))


def _topk_body(imp_ref, ch_ref, sel_ref, need_ref, *, tq, n_top, sub):
    i = pl.program_id(2)
    n_sel = imp_ref.shape[3]
    t_q = i * tq + lax.broadcasted_iota(jnp.int32, (tq, 1), 0)
    cur = t_q // SEL_BLOCK
    j = lax.broadcasted_iota(jnp.int32, (1, n_sel), 1)
    forced = (j == 0) | (j == cur) | (j == cur - 1)
    causal = j <= cur
    imp = jnp.where(forced, FORCE, jnp.where(causal, imp_ref[0, 0], -1.0))
    sel = jnp.zeros((tq, n_sel), F32)
    for _ in range(n_top):
        mx = jnp.max(imp, axis=-1, keepdims=True)
        first = jnp.min(jnp.where(imp == mx, j, n_sel), axis=-1, keepdims=True)
        hit = j == first
        sel = jnp.where(hit, 1.0, sel)
        imp = jnp.where(hit, -2.0, imp)
    sel = jnp.where(causal, sel, 0.0)
    sel_ref[0, 0] = sel.astype(sel_ref.dtype)
    for u in range(tq // sub):
        any_row = jnp.max(sel[u * sub:(u + 1) * sub], axis=0, keepdims=True)
        hits = jnp.dot(jnp.broadcast_to(any_row, (8, n_sel)).astype(BF16), ch_ref[...],
                       preferred_element_type=F32)
        need_ref[0, 0, u] = jnp.where(hits > 0.5, 1, 0).astype(jnp.int32)


def _topk_select(imp, sub, sel_tk, tq=512):
    b, hkv, s, n_sel = imp.shape
    tq = min(tq, s)
    n_top = min(SEL_TOPK, n_sel)
    assert s // sel_tk <= LANES
    chunk_of = (np.arange(n_sel)[:, None] * SEL_BLOCK // sel_tk == np.arange(LANES)[None, :])
    return pl.pallas_call(
        functools.partial(_topk_body, tq=tq, n_top=n_top, sub=sub),
        grid=(b, hkv, s // tq),
        in_specs=[pl.BlockSpec((1, 1, tq, n_sel), lambda bi, h, i: (bi, h, i, 0)),
                  pl.BlockSpec((n_sel, LANES), lambda bi, h, i: (0, 0))],
        out_specs=[pl.BlockSpec((1, 1, tq, n_sel), lambda bi, h, i: (bi, h, i, 0)),
                   pl.BlockSpec((1, 1, tq // sub, 8, LANES), lambda bi, h, i: (bi, h, i, 0, 0))],
        out_shape=[jax.ShapeDtypeStruct((b, hkv, s, n_sel), BF16),
                   jax.ShapeDtypeStruct((b, hkv, s // sub, 8, LANES), jnp.int32)],
        compiler_params=_params(("parallel", "parallel", "parallel")),
        name="nsa_topk_select",
    )(imp, jnp.asarray(chunk_of.astype(np.float32), BF16))


AUG_POS_HI, AUG_POS_LO, AUG_SEL = 0, 3, 6
MASK_BIG = 2.0 ** 100


def _sel_query_aug(n_heads, n_kv, tq):
    g = n_heads // n_kv
    sl = jnp.asarray(np.asarray(_alibi_slopes(n_heads), np.float32) * np.float32(LOG2E))
    hi = sl.astype(BF16).astype(F32)
    mid = (sl - hi).astype(BF16).astype(F32)
    lo = (sl - hi - mid).astype(BF16).astype(F32)
    pieces = jnp.stack([hi, mid, lo], axis=-1)
    cols = jnp.zeros((n_heads, HEAD_DIM), F32)
    cols = cols.at[:, AUG_POS_HI:AUG_POS_HI + 3].set(pieces * LANES)
    cols = cols.at[:, AUG_POS_LO:AUG_POS_LO + 3].set(pieces)
    cols = jnp.broadcast_to(cols.reshape(n_kv, g, 1, HEAD_DIM), (n_kv, g, tq, HEAD_DIM))
    return cols.reshape(n_kv, g * tq, HEAD_DIM).astype(BF16)


def _sel_key_aug(tk, rows):
    c = np.arange(rows) % tk
    cols = np.zeros((rows, HEAD_DIM), np.float32)
    cols[:, AUG_POS_HI:AUG_POS_HI + 3] = (c // 256 * 2)[:, None]
    cols[:, AUG_POS_LO:AUG_POS_LO + 3] = (c % 256)[:, None]
    cols[np.arange(rows), AUG_SEL + c // SEL_BLOCK] = MASK_BIG
    return jnp.asarray(cols, BF16)


def _sel_attn_body(need_ref, q_ref, qa_ref, k_ref, v_ref, sel_ref, o_ref,
                   qaug_ref, m_ref, acc_ref, *, g, tq, tk, n_chunks_total):
    bi, h, i = pl.program_id(0), pl.program_id(1), pl.program_id(2)
    n_sel = sel_ref.shape[3]
    per_chunk = tk // SEL_BLOCK
    rows = g * tq
    qaug_ref[...] = jnp.concatenate([q_ref[0, 0].reshape(rows, HEAD_DIM), qa_ref[0]], axis=-1)
    m_ref[...] = jnp.full(m_ref.shape, M_INIT, F32)
    acc_ref[...] = jnp.zeros(acc_ref.shape, F32)
    sel = sel_ref[0, 0]
    lane = lax.broadcasted_iota(jnp.int32, (1, LANES), 1)
    pos_lanes = (lane >= HEAD_DIM + AUG_POS_HI) & (lane < HEAD_DIM + AUG_POS_HI + 3)
    sel_lanes = (lane >= HEAD_DIM + AUG_SEL) & (lane < HEAD_DIM + AUG_SEL + per_chunk)
    jrow = lax.broadcasted_iota(jnp.int32, (n_sel, 1), 0)
    r = lax.broadcasted_iota(jnp.int32, (rows, 1), 0) % tq
    c = lax.broadcasted_iota(jnp.int32, (1, tk), 1)

    n_parts = 1

    def chunk(ci, diagonal):
        start = pl.multiple_of(ci * tk, tk)
        shift = (ci * (tk // LANES) - i * (tq // LANES)).astype(F32)
        k = k_ref[0, 0, pl.ds(start, tk), :] + jnp.where(pos_lanes, shift, 0.0).astype(BF16)
        v = v_ref[0, 0, pl.ds(start, tk), :]
        place = jnp.where(sel_lanes & (jrow == ci * per_chunk + lane - (HEAD_DIM + AUG_SEL)), 1.0, 0.0)
        picked = jnp.dot(sel, place.astype(BF16), preferred_element_type=F32)
        unpicked = jnp.where(sel_lanes, picked - 1.0, 0.0).astype(BF16)
        for part in range(n_parts):
            rs = slice(part * rows // n_parts, (part + 1) * rows // n_parts)
            s = _nt(qaug_ref[rs, :] + jnp.concatenate([unpicked] * (g // n_parts), axis=0), k)
            if diagonal:
                s = jnp.where(r[rs] + (i * tq - ci * tk) >= c, s, NEG)
            m_old = m_ref[rs, :]
            m_new = jnp.maximum(m_old, jnp.max(s, axis=-1, keepdims=True))
            e = jnp.exp2(s - jnp.concatenate([m_new] * (tk // LANES), axis=1))
            acc_ref[rs, :] = (jnp.exp2(m_old - m_new) * acc_ref[rs, :]
                              + jnp.dot(e.astype(BF16), v, preferred_element_type=F32))
            m_ref[rs, :] = m_new

    diag = (i * tq) // tk
    base = ((bi * pl.num_programs(1) + h) * pl.num_programs(2) + i) * n_chunks_total

    def maybe(ci, carry):
        @pl.when(need_ref[base + ci] != 0)
        def _():
            chunk(ci, False)
        return carry

    lax.fori_loop(0, diag, maybe, 0)
    chunk(diag, True)
    acc = acc_ref[...]
    o = acc[:, :HEAD_DIM] / acc[:, HEAD_DIM:HEAD_DIM + 1]
    o_ref[0] = jnp.concatenate([o[gi * tq:(gi + 1) * tq] for gi in range(g)], axis=-1)


def _sel_attention(q, k_aug, v, sel, need, tq, tk):
    b, hq, s, _ = q.shape
    hkv = k_aug.shape[1]
    g = hq // hkv
    n_sel = sel.shape[3]
    grid_spec = pltpu.PrefetchScalarGridSpec(
        num_scalar_prefetch=1,
        grid=(b, hkv, s // tq),
        in_specs=[pl.BlockSpec((1, 1, g, tq, HEAD_DIM), lambda bi, h, i, nd: (bi, h, 0, i, 0)),
                  pl.BlockSpec((1, g * tq, HEAD_DIM), lambda bi, h, i, nd: (h, 0, 0)),
                  pl.BlockSpec((1, 1, s, LANES), lambda bi, h, i, nd: (bi, h, 0, 0)),
                  pl.BlockSpec((1, 1, s, LANES), lambda bi, h, i, nd: (bi, h, 0, 0)),
                  pl.BlockSpec((1, 1, tq, n_sel), lambda bi, h, i, nd: (bi, h, i, 0))],
        out_specs=pl.BlockSpec((1, tq, g * HEAD_DIM), lambda bi, h, i, nd: (bi, i, h)),
        scratch_shapes=[pltpu.VMEM((g * tq, LANES), BF16),
                        pltpu.VMEM((g * tq, LANES), F32), pltpu.VMEM((g * tq, LANES), F32)])
    return pl.pallas_call(
        functools.partial(_sel_attn_body, g=g, tq=tq, tk=tk, n_chunks_total=s // tk),
        grid_spec=grid_spec,
        out_shape=jax.ShapeDtypeStruct((b, s, hq * HEAD_DIM), F32),
        compiler_params=_params(("parallel", "parallel", "arbitrary")),
        name="nsa_selected_attention",
    )(need, q.reshape(b, hkv, g, s, HEAD_DIM), _sel_query_aug(hq, hkv, tq), k_aug, v, sel)


def _nsa_out_body(x_ref, gt_ref, oc_ref, os_ref, ow_ref, ex_ref, w_ref, o_ref):
    sig = _sigmoid(gt_ref[...])
    mix = jnp.zeros(oc_ref.shape, F32)
    for j, br_ref in enumerate((oc_ref, os_ref, ow_ref)):
        gate = jnp.dot(sig, ex_ref[j], precision=HIGHEST, preferred_element_type=F32)
        mix = mix + gate * br_ref[...]
    o_ref[...] = x_ref[...] + jnp.dot(mix.astype(BF16), w_ref[...], preferred_element_type=F32)


def _nsa_out(x, gates, o_cmp, o_sel, o_win, w_out, tm=256):
    t, d = x.shape
    c = o_cmp.shape[1]
    expand = np.zeros((3, LANES, c), np.float32)
    for head in range(NSA_HQ):
        for j in range(3):
            expand[j, head * 3 + j, head * HEAD_DIM:(head + 1) * HEAD_DIM] = 1.0
    row = lambda width: pl.BlockSpec((tm, width), lambda i: (i, 0))
    return pl.pallas_call(
        _nsa_out_body,
        grid=(t // tm,),
        in_specs=[row(d), row(LANES), row(c), row(c), row(c),
                  pl.BlockSpec((3, LANES, c), lambda i: (0, 0, 0)),
                  pl.BlockSpec((c, d), lambda i: (0, 0))],
        out_specs=row(d),
        out_shape=jax.ShapeDtypeStruct((t, d), F32),
        compiler_params=_params(("parallel",)),
        name="nsa_out",
    )(x, gates, o_cmp, o_sel, o_win, jnp.asarray(expand), w_out)


def _nsa_layer(x, b, s, mix_norm, w_in, w_out, q_norm, k_norm, cmp_pos, cmp_w1, cmp_w2):
    dh = HEAD_DIM
    kvw = NSA_HKV * dh
    edges = np.cumsum([0, NSA_HQ * dh] + [kvw] * 6).tolist()
    ws = [w_in[:, edges[j]:edges[j + 1]].astype(BF16) for j in range(7)]
    n_gate = 3 * NSA_HQ
    ws.append(jnp.pad(w_in[:, edges[7]:edges[7] + n_gate], ((0, 0), (0, LANES - n_gate))).astype(BF16))
    q, kc, vc, ks, vs, kw, vw, gates = _norm_proj(x, mix_norm, ws)
    as3 = lambda y: y.reshape(b, s, y.shape[-1])
    qh = _head_prep(as3(q), q_norm, dh ** -0.5 * LOG2E, BF16)
    kch = _compress(_head_prep(as3(kc), None, 1.0, F32), cmp_pos[0], cmp_w1[0], cmp_w2[0], k_norm[0],
                    extra=_cmp_key_lanes(s // CMP_STRIDE))
    vch = _compress(_head_prep(as3(vc), None, 1.0, F32), cmp_pos[1], cmp_w1[1], cmp_w2[1], None)
    sel_tq, sel_tk, prep_ts = 128, min(512, s), min(512, s)
    ksh = _head_prep(as3(ks), k_norm[1], 1.0, BF16, extra=_sel_key_aug(sel_tk, prep_ts), ts=prep_ts)
    win_tq = min(512, s)
    vsh = _head_prep(as3(vs), None, 1.0, BF16, extra=_ones_lane(prep_ts), ts=prep_ts)
    kwh = _head_prep(as3(kw), k_norm[2], 1.0, BF16, extra=_key_position_lanes(win_tq, prep_ts), ts=prep_ts)
    vwh = _head_prep(as3(vw), None, 1.0, BF16, extra=_ones_lane(prep_ts), ts=prep_ts)
    o_cmp, imp = _cmp_attention(qh, kch, vch, tq=128)
    sel, need = _topk_select(imp, sel_tq, sel_tk)
    need = need[:, :, :, 0, :s // sel_tk].reshape(-1)
    o_sel = _sel_attention(qh, ksh, vsh, sel, need, sel_tq, sel_tk)
    o_win = _banded_attention(qh, kwh, vwh, NSA_HQ, NSA_WINDOW, tq=win_tq)
    flat = lambda y: y.reshape(b * s, y.shape[-1])
    return _nsa_out(x, gates, flat(o_cmp), flat(o_sel), flat(o_win), w_out.astype(BF16))


def _diff_prep_body(y_ref, gq_ref, gk_ref, ka_ref, q_ref, k_ref, v_ref, *, width, q_scale):
    lane = lax.broadcasted_iota(jnp.int32, (1, LANES), 1)
    low = lane < HEAD_DIM

    def pair_norm(ysl, gain):
        sq = ysl * ysl
        ss_lo = jnp.sum(jnp.where(low, sq, 0.0), axis=-1, keepdims=True)
        ss_hi = jnp.sum(jnp.where(low, 0.0, sq), axis=-1, keepdims=True)
        inv = jnp.where(low, lax.rsqrt(ss_lo / HEAD_DIM + EPS), lax.rsqrt(ss_hi / HEAD_DIM + EPS))
        return ysl * inv * gain

    for j in range(width // LANES):
        sl = slice(j * LANES, (j + 1) * LANES)
        q_ref[:, sl] = (pair_norm(y_ref[:, sl], gq_ref[...]) * q_scale).astype(BF16)
        k_ref[:, 2 * j * LANES:(2 * j + 1) * LANES] = pair_norm(
            y_ref[:, width + j * LANES:width + (j + 1) * LANES], gk_ref[...]).astype(BF16)
        k_ref[:, (2 * j + 1) * LANES:(2 * j + 2) * LANES] = ka_ref[...]
        v_ref[:, sl] = y_ref[:, 2 * width + j * LANES:2 * width + (j + 1) * LANES].astype(BF16)


def _diff_prep(qkv, q_norm, k_norm, tk, tm=512):
    t, w3 = qkv.shape
    width = w3 // 3
    pair = lambda gvec: jnp.concatenate([gvec, gvec]).reshape(1, LANES).astype(F32)
    key_aug = jnp.pad(_sel_key_aug(tk, tm)[:, :AUG_SEL], ((0, 0), (0, LANES - AUG_SEL)))
    out = jax.ShapeDtypeStruct((t, width), BF16)
    return pl.pallas_call(
        functools.partial(_diff_prep_body, width=width, q_scale=HEAD_DIM ** -0.5 * LOG2E),
        grid=(t // tm,),
        in_specs=[pl.BlockSpec((tm, w3), lambda i: (i, 0)),
                  pl.BlockSpec((1, LANES), lambda i: (0, 0)),
                  pl.BlockSpec((1, LANES), lambda i: (0, 0)),
                  pl.BlockSpec((tm, LANES), lambda i: (0, 0))],
        out_specs=[pl.BlockSpec((tm, width), lambda i: (i, 0)),
                   pl.BlockSpec((tm, 2 * width), lambda i: (i, 0)),
                   pl.BlockSpec((tm, width), lambda i: (i, 0))],
        out_shape=[out, jax.ShapeDtypeStruct((t, 2 * width), BF16), out],
        compiler_params=_params(("parallel",)),
        name="diff_prep",
    )(qkv, pair(q_norm), pair(k_norm), key_aug)


def _diff_flash_body(reach_ref, q_ref, qa_ref, k_ref, v_ref, lam_ref, sub_ref, o_ref,
                     *, tq, tk, lam_init):
    h, i = pl.program_id(1), pl.program_id(2)
    q = q_ref[0]
    lane = lax.broadcasted_iota(jnp.int32, (1, LANES), 1)
    zero = jnp.zeros_like(q)
    q_both = jnp.concatenate([jnp.where(lane < HEAD_DIM, q, zero),
                              jnp.where(lane < HEAD_DIM, zero, q)], axis=0)
    q_aug = jnp.concatenate([q_both, qa_ref[0]], axis=-1)
    lane2 = lax.broadcasted_iota(jnp.int32, (1, 2 * LANES), 1)
    pos_lanes = (lane2 >= LANES + AUG_POS_HI) & (lane2 < LANES + AUG_POS_HI + 3)
    r = lax.broadcasted_iota(jnp.int32, (2 * tq, 1), 0) % tq
    c = lax.broadcasted_iota(jnp.int32, (1, tk), 1)

    def step(ci, carry, diagonal):
        m, l, acc = carry
        start = pl.multiple_of(ci * tk, tk)
        shift = (ci * (tk // LANES) - i * (tq // LANES)).astype(F32)
        k = k_ref[0, pl.ds(start, tk), :] + jnp.where(pos_lanes, shift, 0.0).astype(BF16)
        v = v_ref[0, pl.ds(start, tk), :]
        s = _nt(q_aug, k)
        if diagonal:
            s = jnp.where(r + (i * tq - ci * tk) >= c, s, NEG)
        m_new = jnp.maximum(m, jnp.max(s, axis=-1, keepdims=True))
        alpha = jnp.exp2(m - m_new)
        e = jnp.exp2(s - jnp.concatenate([m_new] * (tk // LANES), axis=1))
        l = alpha * l + jnp.sum(e, axis=-1, keepdims=True)
        acc = alpha * acc + jnp.dot(e.astype(BF16), v, preferred_element_type=F32)
        return m_new, l, acc

    init = (jnp.full((2 * tq, LANES), M_INIT, F32), jnp.zeros((2 * tq, LANES), F32),
            jnp.zeros((2 * tq, LANES), F32))
    diag = (i * tq) // tk
    first = jnp.maximum(i * tq - reach_ref[h], 0) // tk
    carry = lax.fori_loop(first, diag, lambda ci, cr: step(ci, cr, False), init)
    _, l, acc = step(diag, carry, True)
    lam = lam_ref[...]
    lam01 = jnp.sum(lam[0:1] * lam[1:2], axis=-1, keepdims=True)
    lam23 = jnp.sum(lam[2:3] * lam[3:4], axis=-1, keepdims=True)
    lmbda = jnp.exp(lam01) - jnp.exp(lam23) + lam_init
    o = acc / l
    o = o[:tq] - lmbda * o[tq:]
    o_ref[0] = _rms(o, sub_ref[...]) * (1.0 - lam_init)


def _diff_flash(qn, k_aug, vb, reach, lam, subln, lam_init, tq, tk):
    b, s, w = qn.shape
    n_heads = w // LANES
    sl = jnp.asarray(np.asarray(_alibi_slopes(n_heads), np.float32) * np.float32(LOG2E))
    hi = sl.astype(BF16).astype(F32)
    mid = (sl - hi).astype(BF16).astype(F32)
    lo = (sl - hi - mid).astype(BF16).astype(F32)
    pieces = jnp.stack([hi, mid, lo], axis=-1)
    q_aug = jnp.zeros((n_heads, LANES), F32)
    q_aug = q_aug.at[:, AUG_POS_HI:AUG_POS_HI + 3].set(pieces * LANES)
    q_aug = q_aug.at[:, AUG_POS_LO:AUG_POS_LO + 3].set(pieces)
    q_aug = jnp.broadcast_to(q_aug[:, None, :], (n_heads, 2 * tq, LANES)).astype(BF16)
    grid_spec = pltpu.PrefetchScalarGridSpec(
        num_scalar_prefetch=1,
        grid=(b, n_heads, s // tq),
        in_specs=[pl.BlockSpec((1, tq, LANES), lambda bi, h, i, rc: (bi, i, h)),
                  pl.BlockSpec((1, 2 * tq, LANES), lambda bi, h, i, rc: (h, 0, 0)),
                  pl.BlockSpec((1, s, 2 * LANES), lambda bi, h, i, rc: (bi, 0, h)),
                  pl.BlockSpec((1, s, LANES), lambda bi, h, i, rc: (bi, 0, h)),
                  pl.BlockSpec(lam.shape, lambda bi, h, i, rc: (0, 0)),
                  pl.BlockSpec((1, LANES), lambda bi, h, i, rc: (0, 0))],
        out_specs=pl.BlockSpec((1, tq, LANES), lambda bi, h, i, rc: (bi, i, h)))
    return pl.pallas_call(
        functools.partial(_diff_flash_body, tq=tq, tk=tk, lam_init=lam_init),
        grid_spec=grid_spec,
        out_shape=jax.ShapeDtypeStruct((b, s, w), F32),
        compiler_params=_params(("parallel", "parallel", "arbitrary")),
        name="diff_flash",
    )(reach, qn, q_aug, k_aug, vb, lam.astype(F32), subln.reshape(1, LANES).astype(F32))


def _diff_reach(q_norm, k_norm, n_heads, s):
    bound = (HEAD_DIM * jnp.max(jnp.abs(q_norm)) * jnp.max(jnp.abs(k_norm))
             * (HEAD_DIM ** -0.5 * LOG2E) * 1.02)
    slopes = jnp.asarray(np.asarray(_alibi_slopes(n_heads), np.float32) * np.float32(LOG2E))
    reach = jnp.ceil((2.0 * bound + 152.0) / slopes) + 1.0
    return jnp.minimum(reach, float(s)).astype(jnp.int32)


def _diff_layer(x, b, s, mix_norm, w_in, w_out, q_norm, k_norm, lam, subln, lam_init):
    tq, tk = min(512, s), min(512, s)
    (qkv,) = _norm_proj(x, mix_norm, [w_in.astype(BF16)])
    qn, k_aug, vb = _diff_prep(qkv, q_norm, k_norm, tk)
    as3 = lambda y: y.reshape(b, s, y.shape[-1])
    o = _diff_flash(as3(qn), as3(k_aug), as3(vb), _diff_reach(q_norm, k_norm, DIFF_H, s),
                    lam, subln, lam_init, tq, tk)
    return _proj_residual(x, o.reshape(b * s, -1), w_out.astype(BF16))


def _gdn_prep_body(cur_ref, prev_ref, cw_ref, b_ref, a_ref, alog_ref, dt_ref,
                   q_ref, k_ref, v_ref, beta_ref, gc_ref, pad_ref, *, ts, width):
    i = pl.program_id(1)
    halo = prev_ref.shape[1]
    pad_ref[0:halo, :] = jnp.where(i > 0, prev_ref[0], 0.0)
    pad_ref[halo:, :] = cur_ref[0]
    y = jnp.zeros((ts, 3 * width), F32)
    for j in range(GDN_CONV):
        off = halo - (GDN_CONV - 1) + j
        y = y + cw_ref[j:j + 1, :] * pad_ref[off:off + ts, :]
    y = y * _sigmoid(y)
    for h in range(width // GDN_DK):
        sl = slice(h * GDN_DK, (h + 1) * GDN_DK)
        qh = y[:, sl]
        q_ref[0, :, sl] = qh * lax.rsqrt(jnp.sum(qh * qh, axis=-1, keepdims=True) + EPS) * (GDN_DK ** -0.5)
        kh = y[:, width + h * GDN_DK:width + (h + 1) * GDN_DK]
        k_ref[0, :, sl] = kh * lax.rsqrt(jnp.sum(kh * kh, axis=-1, keepdims=True) + EPS)
    v_ref[0] = y[:, 2 * width:]
    beta_ref[0] = _sigmoid(b_ref[0])
    z = a_ref[0] + dt_ref[...]
    softplus = jnp.maximum(z, 0.0) + jnp.log(1.0 + jnp.exp(-jnp.abs(z)))
    gl = -jnp.exp(alog_ref[...]) * softplus
    r = lax.broadcasted_iota(jnp.int32, (ts, ts), 0)
    c = lax.broadcasted_iota(jnp.int32, (ts, ts), 1)
    tri = jnp.where((r // GDN_CHUNK == c // GDN_CHUNK) & (c <= r), 1.0, 0.0)
    gc_ref[0] = jnp.dot(tri, gl, precision=HIGHEST, preferred_element_type=F32)


def _gdn_prep(qkv, b_pad, a_pad, conv_w, a_log, dt_bias, ts=256):
    b, s, w3 = qkv.shape
    width = w3 // 3
    halo = 8
    pad_lane = lambda vec: jnp.pad(vec.astype(F32), (0, LANES - vec.shape[0])).reshape(1, LANES)
    big = jax.ShapeDtypeStruct((b, s, width), F32)
    small = jax.ShapeDtypeStruct((b, s, LANES), F32)
    tok = lambda wd: pl.BlockSpec((1, ts, wd), lambda bi, i: (bi, i, 0))
    return pl.pallas_call(
        functools.partial(_gdn_prep_body, ts=ts, width=width),
        grid=(b, s // ts),
        in_specs=[tok(w3),
                  pl.BlockSpec((1, halo, w3), lambda bi, i: (bi, jnp.maximum(i * (ts // halo) - 1, 0), 0)),
                  pl.BlockSpec((GDN_CONV, w3), lambda bi, i: (0, 0)),
                  tok(LANES), tok(LANES),
                  pl.BlockSpec((1, LANES), lambda bi, i: (0, 0)),
                  pl.BlockSpec((1, LANES), lambda bi, i: (0, 0))],
        out_specs=[tok(width), tok(width), tok(width), tok(LANES), tok(LANES)],
        out_shape=[big, big, big, small, small],
        scratch_shapes=[pltpu.VMEM((ts + halo, w3), F32)],
        compiler_params=_params(("parallel", "parallel")),
        name="gdn_prep",
    )(qkv, qkv, conv_w.astype(F32), b_pad, a_pad, pad_lane(a_log), pad_lane(dt_bias))


def _unit_lower_inverses(mats):
    n = mats[0].shape[0]
    r = lax.broadcasted_iota(jnp.int32, (n, n), 0)
    c = lax.broadcasted_iota(jnp.int32, (n, n), 1)
    eye = jnp.where(r == c, 1.0, 0.0)
    d = 1
    inv = None
    while d < n:
        below = (r // (2 * d) == c // (2 * d)) & (r // d != c // d)
        offs = [jnp.where(below, a, 0.0) for a in mats]
        if d == 1:
            inv = [eye - off for off in offs]
        else:
            left = [_dot3(t, off) for t, off in zip(inv, offs)]
            inv = [t - _dot3(lt, t) for t, lt in zip(inv, left)]
        d *= 2
    return inv


def _gdn_chunk_body(q_ref, k_ref, v_ref, beta_ref, gc_ref, gr_ref, gate_ref, on_ref, o_ref,
                    state_ref, *, nc, n_heads):
    @pl.when(pl.program_id(1) == 0)
    def _():
        state_ref[...] = jnp.zeros(state_ref.shape, F32)

    cl = GDN_CHUNK
    r = lax.broadcasted_iota(jnp.int32, (cl, cl), 0)
    c = lax.broadcasted_iota(jnp.int32, (cl, cl), 1)
    fdot = functools.partial(jnp.dot, preferred_element_type=F32)
    blocks = [(ch, h) for ch in range(nc) for h in range(n_heads)]
    rows = lambda ch: slice(ch * cl, (ch + 1) * cl)
    lanes = lambda h: slice(h * GDN_DK, (h + 1) * GDN_DK)

    qs = [q_ref[0, rows(ch), lanes(h)] for ch, h in blocks]
    ks = [k_ref[0, rows(ch), lanes(h)] for ch, h in blocks]
    bcol = [beta_ref[0, rows(ch), h:h + 1] for ch, h in blocks]
    gcol = [gc_ref[0, rows(ch), h:h + 1] for ch, h in blocks]
    glast = [gc_ref[0, ch * cl + cl - 1:ch * cl + cl, h:h + 1] for ch, h in blocks]
    decay = [jnp.exp(jnp.minimum(g - gr_ref[0, ch, h:h + 1, :], 0.0))
             for g, (ch, h) in zip(gcol, blocks)]
    kb = [k * b for k, b in zip(ks, bcol)]
    amat = [jnp.where(r > c, _dot3(x, k, ((1,), (1,))) * dc, 0.0) for x, k, dc in zip(kb, ks, decay)]
    inv = _unit_lower_inverses(amat)
    eg = [jnp.exp(g) for g in gcol]
    u = [_dot3(t, v_ref[0, rows(ch), lanes(h)] * b) for t, b, (ch, h) in zip(inv, bcol, blocks)]
    w = [_dot3(t, x * e) for t, x, e in zip(inv, kb, eg)]
    attn = [jnp.where(r >= c, _nt(q, k) * dc, 0.0) for q, k, dc in zip(qs, ks, decay)]

    for ch in range(nc):
        ids = range(ch * n_heads, (ch + 1) * n_heads)
        state = [state_ref[h] for h in range(n_heads)]
        v_new = [u[i] - fdot(w[i], st) for i, st in zip(ids, state)]
        o = [fdot(qs[i] * eg[i], st) + fdot(attn[i], vn) for i, st, vn in zip(ids, state, v_new)]
        for h, (i, st, vn) in enumerate(zip(ids, state, v_new)):
            state_ref[h] = st * jnp.exp(glast[i]) + _tn(ks[i] * jnp.exp(glast[i] - gcol[i]), vn)
        for h in range(n_heads):
            gate = gate_ref[0, rows(ch), lanes(h)]
            o_ref[0, rows(ch), lanes(h)] = _rms(o[h], on_ref[...]) * (gate * _sigmoid(gate))


def _gdn_chunks(q, k, v, beta, gc, gc_rows, gate, o_norm, nc=1):
    b, s, w = q.shape
    n_heads = w // GDN_DK
    ts = nc * GDN_CHUNK
    tok = lambda wd: pl.BlockSpec((1, ts, wd), lambda bi, i: (bi, i, 0))
    return pl.pallas_call(
        functools.partial(_gdn_chunk_body, nc=nc, n_heads=n_heads),
        grid=(b, s // ts),
        in_specs=[tok(w), tok(w), tok(w), tok(LANES), tok(LANES),
                  pl.BlockSpec((1, nc, 8, GDN_CHUNK), lambda bi, i: (bi, i, 0, 0)),
                  tok(w),
                  pl.BlockSpec((1, GDN_DV), lambda bi, i: (0, 0))],
        out_specs=tok(w),
        out_shape=jax.ShapeDtypeStruct((b, s, w), F32),
        scratch_shapes=[pltpu.VMEM((n_heads, GDN_DK, GDN_DV), F32)],
        compiler_params=_params(("parallel", "arbitrary")),
        name="gdn_chunks",
    )(q, k, v, beta, gc, gc_rows, gate, o_norm.reshape(1, GDN_DV).astype(F32))


def _gdn_layer(x, b, s, mix_norm, w_in, w_out, conv_w, a_log, dt_bias, o_norm):
    qkv_w = GDN_H * (2 * GDN_DK + GDN_DV)
    lane_pad = lambda wcols: jnp.pad(wcols, ((0, 0), (0, LANES - wcols.shape[1]))).astype(BF16)
    ws = [w_in[:, :qkv_w].astype(BF16),
          lane_pad(w_in[:, qkv_w:qkv_w + GDN_H]),
          lane_pad(w_in[:, qkv_w + GDN_H:qkv_w + 2 * GDN_H]),
          w_in[:, qkv_w + 2 * GDN_H:].astype(BF16)]
    qkv, b_pad, a_pad, gate = _norm_proj(x, mix_norm, ws)
    as3 = lambda y: y.reshape(b, s, y.shape[-1])
    q, k, v, beta, gc = _gdn_prep(as3(qkv), as3(b_pad), as3(a_pad), conv_w, a_log, dt_bias,
                                  ts=min(256, s))
    gc_rows = gc[:, :, :8].reshape(b, s // GDN_CHUNK, GDN_CHUNK, 8).transpose(0, 1, 3, 2)
    o = _gdn_chunks(q, k, v, beta, gc, gc_rows, as3(gate), o_norm)
    return _proj_residual(x, o.reshape(b * s, -1), w_out.astype(BF16))


def _swa_layer(x, b, s, mix_norm, w_in, w_out, q_norm, k_norm, sinks):
    dh = HEAD_DIM
    e0, e1 = SWA_HQ * dh, (SWA_HQ + SWA_HKV) * dh
    ws = [w_in[:, :e0].astype(BF16), w_in[:, e0:e1].astype(BF16), w_in[:, e1:].astype(BF16)]
    q, k, v = _norm_proj(x, mix_norm, ws)
    as3 = lambda y: y.reshape(b, s, y.shape[-1])
    qh = _head_prep(as3(q), q_norm, dh ** -0.5 * LOG2E, BF16)
    tq, prep_ts = min(256, s), min(512, s)
    kh = _head_prep(as3(k), k_norm, 1.0, BF16, extra=_key_position_lanes(tq, prep_ts), ts=prep_ts)
    vh = _head_prep(as3(v), None, 1.0, BF16, extra=_ones_lane(prep_ts), ts=prep_ts)
    o = _banded_attention(qh, kh, vh, SWA_HQ, SWA_WINDOW, tq=tq, sinks=sinks)
    return _proj_residual(x, o.reshape(b * s, -1), w_out.astype(BF16))


def kernel(x, ffn1_norm, ffn1_w_in, ffn1_w_out, mix_norm, ffn2_norm, ffn2_w_in, ffn2_w_out,
           nsa_w_in, nsa_w_out, nsa_q_norm, nsa_k_norm, nsa_cmp_pos, nsa_cmp_w1, nsa_cmp_w2,
           diff_w_in, diff_w_out, diff_q_norm, diff_k_norm, diff_lambda, diff_subln,
           gdn_w_in, gdn_w_out, gdn_conv_w, gdn_a_log, gdn_dt_bias, gdn_o_norm,
           swa_w_in, swa_w_out, swa_q_norm, swa_k_norm, swa_sinks):
    b, s, d = x.shape
    depth = ffn1_norm.shape[0]
    n_mixers = 4
    h = x.reshape(b * s, d)
    for layer in range(depth):
        kind, j = layer % n_mixers, layer // n_mixers
        h = _ffn(h, ffn1_norm[layer], ffn1_w_in[layer].astype(BF16), ffn1_w_out[layer].astype(BF16))
        if kind == 0:
            h = _nsa_layer(h, b, s, mix_norm[layer], nsa_w_in[j], nsa_w_out[j], nsa_q_norm[j],
                           nsa_k_norm[j], nsa_cmp_pos[j], nsa_cmp_w1[j], nsa_cmp_w2[j])
        elif kind == 1:
            lam_init = 0.8 - 0.6 * math.exp(-0.3 * layer)
            h = _diff_layer(h, b, s, mix_norm[layer], diff_w_in[j], diff_w_out[j], diff_q_norm[j],
                            diff_k_norm[j], diff_lambda[j], diff_subln[j], lam_init)
        elif kind == 2:
            h = _gdn_layer(h, b, s, mix_norm[layer], gdn_w_in[j], gdn_w_out[j], gdn_conv_w[j],
                           gdn_a_log[j], gdn_dt_bias[j], gdn_o_norm[j])
        else:
            h = _swa_layer(h, b, s, mix_norm[layer], swa_w_in[j], swa_w_out[j], swa_q_norm[j],
                           swa_k_norm[j], swa_sinks[j])
        h = _ffn(h, ffn2_norm[layer], ffn2_w_in[layer].astype(BF16), ffn2_w_out[layer].astype(BF16))
    return h.reshape(b, s, d)
```

```python
import functools
import math

import numpy as np
import jax
import jax.numpy as jnp
from jax import lax
from jax.experimental import pallas as pl
from jax.experimental.pallas import tpu as pltpu

F32 = jnp.float32
BF16 = jnp.bfloat16
HIGHEST = lax.Precision.HIGHEST

EPS = 1e-6
NEG = -1e30
M_INIT = -1e29
LOG2E = 1.4426950408889634
FORCE = 1e9

HEAD_DIM = 64
LANES = 128
VMEM_LIMIT = 56 * 1024 * 1024

NSA_HQ, NSA_HKV = 16, 4
CMP_BLOCK, CMP_STRIDE = 32, 16
SEL_BLOCK, SEL_TOPK = 64, 16
NSA_WINDOW = 512
DIFF_H = 8
GDN_H, GDN_DK, GDN_DV, GDN_CONV, GDN_CHUNK = 8, 128, 128, 4, 64
SWA_HQ, SWA_HKV, SWA_WINDOW = 16, 4, 128


def _params(sem):
    return pltpu.CompilerParams(dimension_semantics=sem, vmem_limit_bytes=VMEM_LIMIT)


def _nt(a, b, precision=None):
    return lax.dot_general(a, b, (((1,), (1,)), ((), ())), precision=precision,
                           preferred_element_type=F32)


def _tn(a, b, precision=None):
    return lax.dot_general(a, b, (((0,), (0,)), ((), ())), precision=precision,
                           preferred_element_type=F32)


def _dot3(a, b, contract=((1,), (0,))):
    a_hi, b_hi = a.astype(BF16), b.astype(BF16)
    a_lo = (a - a_hi.astype(F32)).astype(BF16)
    b_lo = (b - b_hi.astype(F32)).astype(BF16)
    dot = lambda x, y: lax.dot_general(x, y, (contract, ((), ())), preferred_element_type=F32)
    return dot(a_hi, b_hi) + dot(a_hi, b_lo) + dot(a_lo, b_hi)


def _rms(x, g):
    return x * lax.rsqrt(jnp.mean(x * x, axis=-1, keepdims=True) + EPS) * g


def _sigmoid(x):
    return 1.0 / (1.0 + jnp.exp(-x))


def _alibi_slopes(n):
    return [2.0 ** (-8.0 * (i + 1) / n) for i in range(n)]


def _ffn_body(x_ref, g_ref, win_ref, wout_ref, o_ref, *, d_ff, chunk):
    x = x_ref[...]
    xn = _rms(x, g_ref[...]).astype(BF16)
    acc = jnp.zeros(x.shape, F32)
    for c in range(d_ff // chunk):
        a = jnp.dot(xn, win_ref[:, c * chunk:(c + 1) * chunk], preferred_element_type=F32)
        b = jnp.dot(xn, win_ref[:, d_ff + c * chunk:d_ff + (c + 1) * chunk],
                    preferred_element_type=F32)
        h = (a * _sigmoid(a) * b).astype(BF16)
        acc = acc + jnp.dot(h, wout_ref[c * chunk:(c + 1) * chunk, :], preferred_element_type=F32)
    o_ref[...] = x + 0.5 * acc


def _ffn(x, g, w_in, w_out, tm=512, chunk=256):
    t, d = x.shape
    d_ff = w_out.shape[0]
    return pl.pallas_call(
        functools.partial(_ffn_body, d_ff=d_ff, chunk=chunk),
        grid=(t // tm,),
        in_specs=[pl.BlockSpec((tm, d), lambda i: (i, 0)),
                  pl.BlockSpec((1, d), lambda i: (0, 0)),
                  pl.BlockSpec(w_in.shape, lambda i: (0, 0), pipeline_mode=pl.Buffered(1)),
                  pl.BlockSpec(w_out.shape, lambda i: (0, 0), pipeline_mode=pl.Buffered(1))],
        out_specs=pl.BlockSpec((tm, d), lambda i: (i, 0)),
        out_shape=jax.ShapeDtypeStruct((t, d), F32),
        compiler_params=_params(("parallel",)),
        name="ffn",
    )(x, g.reshape(1, d), w_in, w_out)


def _norm_proj_body(x_ref, g_ref, *refs, n):
    xn = _rms(x_ref[...], g_ref[...]).astype(BF16)
    for w_ref, o_ref in zip(refs[:n], refs[n:]):
        o_ref[...] = jnp.dot(xn, w_ref[...], preferred_element_type=F32).astype(o_ref.dtype)


def _norm_proj(x, g, ws, tm=512):
    t, d = x.shape
    n = len(ws)
    return pl.pallas_call(
        functools.partial(_norm_proj_body, n=n),
        grid=(t // tm,),
        in_specs=[pl.BlockSpec((tm, d), lambda i: (i, 0)),
                  pl.BlockSpec((1, d), lambda i: (0, 0))]
                 + [pl.BlockSpec(w.shape, lambda i: (0, 0)) for w in ws],
        out_specs=[pl.BlockSpec((tm, w.shape[1]), lambda i: (i, 0)) for w in ws],
        out_shape=[jax.ShapeDtypeStruct((t, w.shape[1]), F32) for w in ws],
        compiler_params=_params(("parallel",)),
        name="norm_proj",
    )(x, g.reshape(1, d), *ws)


def _proj_res_body(x_ref, a_ref, w_ref, o_ref):
    o_ref[...] = x_ref[...] + jnp.dot(a_ref[...].astype(BF16), w_ref[...],
                                      preferred_element_type=F32)


def _proj_residual(x, a, w, tm=512):
    t, d = x.shape
    k = a.shape[1]
    return pl.pallas_call(
        _proj_res_body,
        grid=(t // tm,),
        in_specs=[pl.BlockSpec((tm, d), lambda i: (i, 0)),
                  pl.BlockSpec((tm, k), lambda i: (i, 0)),
                  pl.BlockSpec((k, d), lambda i: (0, 0))],
        out_specs=pl.BlockSpec((tm, d), lambda i: (i, 0)),
        out_shape=jax.ShapeDtypeStruct((t, d), F32),
        compiler_params=_params(("parallel",)),
        name="proj_residual",
    )(x, a, w)


def _head_prep_body(y_ref, g_ref, *refs, nh, norm, scale, has_extra):
    o_ref = refs[-1]
    for h in range(nh):
        yh = y_ref[0, :, h * HEAD_DIM:(h + 1) * HEAD_DIM]
        if norm:
            yh = _rms(yh, g_ref[...])
        if scale != 1.0:
            yh = yh * scale
        yh = yh.astype(o_ref.dtype)
        if has_extra:
            yh = jnp.concatenate([yh, refs[0][...]], axis=-1)
        o_ref[0, h] = yh


def _head_prep(y, gain, scale, out_dtype, extra=None, ts=512):
    b, s, c = y.shape
    nh = c // HEAD_DIM
    norm = gain is not None
    g = (gain if norm else jnp.ones((HEAD_DIM,), F32)).reshape(1, HEAD_DIM)
    width = HEAD_DIM if extra is None else 2 * HEAD_DIM
    in_specs = [pl.BlockSpec((1, ts, c), lambda bi, i: (bi, i, 0)),
                pl.BlockSpec((1, HEAD_DIM), lambda bi, i: (0, 0))]
    args = [y, g]
    if extra is not None:
        in_specs.append(pl.BlockSpec((ts, HEAD_DIM), lambda bi, i: (0, 0)))
        args.append(extra)
    return pl.pallas_call(
        functools.partial(_head_prep_body, nh=nh, norm=norm, scale=scale,
                          has_extra=extra is not None),
        grid=(b, s // ts),
        in_specs=in_specs,
        out_specs=pl.BlockSpec((1, nh, ts, width), lambda bi, i: (bi, 0, i, 0)),
        out_shape=jax.ShapeDtypeStruct((b, nh, s, width), out_dtype),
        compiler_params=_params(("parallel", "parallel")),
        name="head_prep",
    )(*args)


def _slope_rows(n_heads, n_kv, rows_per_head, mult=LOG2E):
    g = n_heads // n_kv
    sl = np.asarray(_alibi_slopes(n_heads), np.float32).reshape(n_kv, g, 1) * np.float32(mult)
    return jnp.asarray(np.repeat(sl, rows_per_head, axis=1).reshape(n_kv, g * rows_per_head, 1))


BAND_SUB = 128


def _banded_body(*refs, g, tq, window, has_sink):
    q_ref, qa_ref, kp_ref, kc_ref, vp_ref, vc_ref, mask_ref = refs[:7]
    o_ref = refs[-1]
    i = pl.program_id(2)
    sub = BAND_SUB
    rows = g * sub
    span = window + sub
    lane = lax.broadcasted_iota(jnp.int32, (1, LANES), 1)
    pos_lanes = (lane >= HEAD_DIM + AUG_POS_HI) & (lane < HEAD_DIM + AUG_POS_HI + 3)
    shifts = jnp.concatenate(
        [jnp.broadcast_to(jnp.where(pos_lanes, float(bk - window // sub), 0.0), (sub, LANES))
         for bk in range(span // sub)], axis=0).astype(BF16)
    k_win = jnp.concatenate([kp_ref[0, 0], kc_ref[0, 0]], axis=0)
    v_win = jnp.concatenate([vp_ref[0, 0], vc_ref[0, 0]], axis=0)
    col = lax.broadcasted_iota(jnp.int32, (1, span), 1)
    before_start = jnp.where(i > 0, 0.0, NEG)
    if has_sink:
        sl_ref, sink_ref = refs[7], refs[8]
        r = (lax.broadcasted_iota(jnp.int32, (rows, LANES), 0) % sub).astype(F32)
        sink = sink_ref[0] * LOG2E + sl_ref[0] * r
    units = range(tq // sub)
    q_aug = [jnp.concatenate([q_ref[0, 0, :, u * sub:(u + 1) * sub, :].reshape(rows, HEAD_DIM),
                              qa_ref[0]], axis=-1) for u in units]
    bias = [jnp.concatenate([mask_ref[...] + jnp.where(col < window - u * sub, before_start, 0.0)] * g,
                            axis=0) for u in units]
    s = [_nt(q_aug[u], k_win[u * sub:u * sub + span] + shifts) + bias[u] for u in units]
    m = [jnp.broadcast_to(jnp.max(su, axis=-1, keepdims=True), (rows, LANES)) for su in s]
    if has_sink:
        m = [jnp.maximum(mu, sink) for mu in m]
    e = [jnp.exp2(su - jnp.concatenate([mu] * (span // LANES), axis=1)).astype(BF16)
         for su, mu in zip(s, m)]
    acc = [jnp.dot(e[u], v_win[u * sub:u * sub + span], preferred_element_type=F32) for u in units]
    for u in units:
        if has_sink:
            acc[u] = acc[u] + jnp.where(lane >= HEAD_DIM, jnp.exp2(sink - m[u]), 0.0)
        o = (acc[u] / pltpu.roll(acc[u], HEAD_DIM, 1))[:, :HEAD_DIM]
        o_ref[0, u * sub:(u + 1) * sub, :] = jnp.concatenate(
            [o[gi * sub:(gi + 1) * sub] for gi in range(g)], axis=-1)


def _key_position_lanes(period, rows):
    c = np.arange(rows) % period
    cols = np.zeros((rows, HEAD_DIM), np.float32)
    cols[:, AUG_POS_HI:AUG_POS_HI + 3] = (c // 256 * 2)[:, None]
    cols[:, AUG_POS_LO:AUG_POS_LO + 3] = (c % 256)[:, None]
    return jnp.asarray(cols, BF16)


def _ones_lanes(rows):
    return jnp.ones((rows, HEAD_DIM), BF16)


def _banded_attention(q, k_aug, v_aug, n_heads, window, tq, sinks=None):
    b, hq, s, _ = q.shape
    hkv = k_aug.shape[1]
    g = hq // hkv
    sub = BAND_SUB
    assert tq % window == 0 and window % sub == 0
    span = window + sub
    dist = np.arange(sub)[:, None] + window - np.arange(span)[None, :]
    mask = np.where((dist >= 0) & (dist < window), 0.0, NEG).astype(np.float32)
    per = tq // window
    prev_spec = pl.BlockSpec((1, 1, window, LANES), lambda bi, h, i: (bi, h, jnp.maximum(i * per - 1, 0), 0))
    cur_spec = pl.BlockSpec((1, 1, tq, LANES), lambda bi, h, i: (bi, h, i, 0))
    in_specs = [pl.BlockSpec((1, 1, g, tq, HEAD_DIM), lambda bi, h, i: (bi, h, 0, i, 0)),
                pl.BlockSpec((1, g * sub, HEAD_DIM), lambda bi, h, i: (h, 0, 0)),
                prev_spec, cur_spec, prev_spec, cur_spec,
                pl.BlockSpec((sub, span), lambda bi, h, i: (0, 0))]
    args = [q.reshape(b, hkv, g, s, HEAD_DIM), _sel_query_aug(n_heads, hkv, sub),
            k_aug, k_aug, v_aug, v_aug, jnp.asarray(mask)]
    if sinks is not None:
        in_specs += [pl.BlockSpec((1, g * sub, 1), lambda bi, h, i: (h, 0, 0))] * 2
        args += [_slope_rows(n_heads, hkv, sub),
                 jnp.repeat(sinks.astype(F32).reshape(hkv, g, 1), sub, axis=1).reshape(hkv, g * sub, 1)]
    return pl.pallas_call(
        functools.partial(_banded_body, g=g, tq=tq, window=window, has_sink=sinks is not None),
        grid=(b, hkv, s // tq),
        in_specs=in_specs,
        out_specs=pl.BlockSpec((1, tq, g * HEAD_DIM), lambda bi, h, i: (bi, i, h)),
        out_shape=jax.ShapeDtypeStruct((b, s, hq * HEAD_DIM), F32),
        compiler_params=_params(("parallel", "parallel", "parallel")),
        name="banded_attention",
    )(*args)


def _compress_body(t_ref, pt_ref, pb_ref, w1t_ref, w1b_ref, w2_ref, g_ref, *refs, norm):
    o_ref = refs[-1]
    sub = t_ref[0, 0]
    n_sub = sub.shape[0]
    a = jnp.dot((sub + pt_ref[...]).astype(BF16), w1t_ref[...], preferred_element_type=F32)
    bm = jnp.dot((sub + pb_ref[...]).astype(BF16), w1b_ref[...], preferred_element_type=F32)
    hid = a + pltpu.roll(bm, shift=n_sub - 1, axis=0)
    hid = 0.5 * hid * (1.0 + jnp.tanh(0.7978845608028654 * (hid + 0.044715 * hid * hid * hid)))
    out = jnp.dot(hid.astype(BF16), w2_ref[...], preferred_element_type=F32)
    if norm:
        out = _rms(out, g_ref[...])
    out = out.astype(BF16)
    if len(refs) == 2:
        out = jnp.concatenate([out, refs[0][...]], axis=-1)
    o_ref[0, 0] = out


def _compress(t, pos, w1, w2, gain, extra=None):
    b, h, s, _ = t.shape
    n_sub = s // CMP_STRIDE
    half = CMP_STRIDE * HEAD_DIM
    hidden = w1.shape[1]
    norm = gain is not None
    g = (gain if norm else jnp.ones((HEAD_DIM,), F32)).reshape(1, HEAD_DIM)
    const = lambda shape: pl.BlockSpec(shape, lambda bi, hi: (0,) * len(shape))
    width = HEAD_DIM if extra is None else 2 * HEAD_DIM
    extras = [] if extra is None else [extra]
    return pl.pallas_call(
        functools.partial(_compress_body, norm=norm),
        grid=(b, h),
        in_specs=[pl.BlockSpec((1, 1, n_sub, half), lambda bi, hi: (bi, hi, 0, 0)),
                  const((1, half)), const((1, half)),
                  const((half, hidden)), const((half, hidden)),
                  const((hidden, HEAD_DIM)), const((1, HEAD_DIM))]
                 + [const((n_sub, HEAD_DIM)) for _ in extras],
        out_specs=pl.BlockSpec((1, 1, n_sub, width), lambda bi, hi: (bi, hi, 0, 0)),
        out_shape=jax.ShapeDtypeStruct((b, h, n_sub, width), BF16),
        compiler_params=_params(("parallel", "parallel")),
        name="nsa_compress",
    )(t.reshape(b, h, n_sub, half),
      pos[:CMP_STRIDE].reshape(1, half), pos[CMP_STRIDE:].reshape(1, half),
      w1[:half].astype(BF16), w1[half:].astype(BF16), w2.astype(BF16), g, *extras)


def _cmp_key_lanes(n_sub):
    n = np.arange(n_sub)
    cols = np.zeros((n_sub, HEAD_DIM), np.float32)
    cols[:, AUG_POS_HI:AUG_POS_HI + 3] = (n // 8)[:, None]
    cols[:, AUG_POS_LO:AUG_POS_LO + 3] = (CMP_STRIDE * (n % 8) + CMP_BLOCK - 1)[:, None]
    return jnp.asarray(cols, BF16)


def _cmp_attn_body(q_ref, qa_ref, kc_ref, vc_ref, ov_ref, o_ref, imp_ref, *, g, tq, widths):
    i = pl.program_id(2)
    rows = g * tq
    q_aug = jnp.concatenate([q_ref[0, 0].reshape(rows, HEAD_DIM), qa_ref[0]], axis=-1)
    lane = lax.broadcasted_iota(jnp.int32, (1, LANES), 1)
    pos_lanes = (lane >= HEAD_DIM + AUG_POS_HI) & (lane < HEAD_DIM + AUG_POS_HI + 3)
    shift = jnp.where(pos_lanes, (-i * (tq // LANES)).astype(F32), 0.0).astype(BF16)
    t_row = i * tq + lax.broadcasted_iota(jnp.int32, (rows, 1), 0) % tq
    n_visible = (i * tq + tq) // CMP_STRIDE - 1

    def attend(n_cols):
        s = _nt(q_aug, kc_ref[0, 0, :n_cols, :] + shift)
        cmp_end = lax.broadcasted_iota(jnp.int32, (1, n_cols), 1) * CMP_STRIDE + (CMP_BLOCK - 1)
        s = jnp.where(t_row >= cmp_end, s, NEG)
        wide = lambda x: jnp.concatenate([jnp.broadcast_to(x, (rows, LANES))] * (n_cols // LANES), axis=1)
        e = jnp.exp2(s - wide(jnp.maximum(jnp.max(s, axis=-1, keepdims=True), M_INIT)))
        l = jnp.sum(e, axis=-1, keepdims=True)
        p = e * wide(1.0 / jnp.where(l > 0.0, l, 1.0))
        o = jnp.dot(p.astype(BF16), vc_ref[0, 0, :n_cols, :], preferred_element_type=F32)
        o_ref[0] = jnp.concatenate([o[gi * tq:(gi + 1) * tq] for gi in range(g)], axis=-1)
        psum = p[0:tq]
        for gi in range(1, g):
            psum = psum + p[gi * tq:(gi + 1) * tq]
        p_hi = psum.astype(BF16)
        r1 = psum - p_hi.astype(F32)
        p_mid = r1.astype(BF16)
        p_lo = (r1 - p_mid.astype(F32)).astype(BF16)
        ov = ov_ref[:n_cols, :]
        imp_ref[0, 0] = (jnp.dot(p_hi, ov, preferred_element_type=F32)
                         + jnp.dot(p_mid, ov, preferred_element_type=F32)
                         + jnp.dot(p_lo, ov, preferred_element_type=F32))

    for lo, hi in zip((0,) + widths[:-1], widths):
        pl.when((n_visible > lo) & (n_visible <= hi) if lo else n_visible <= hi)(
            functools.partial(attend, hi))


def _cmp_attention(q, kc_aug, vc, tq):
    b, hq, s, _ = q.shape
    hkv = kc_aug.shape[1]
    g = hq // hkv
    n_sub = kc_aug.shape[2]
    n_sel = s // SEL_BLOCK
    step = 2 * LANES
    widths = tuple(range(step, n_sub, step)) + (n_sub,) if n_sub >= 2 * step else (n_sub,)
    cs = np.arange(n_sub)[:, None] * CMP_STRIDE
    ss = np.arange(n_sel)[None, :] * SEL_BLOCK
    overlap = ((cs < ss + SEL_BLOCK) & (cs + CMP_BLOCK > ss)).astype(np.float32)
    return pl.pallas_call(
        functools.partial(_cmp_attn_body, g=g, tq=tq, widths=widths),
        grid=(b, hkv, s // tq),
        in_specs=[pl.BlockSpec((1, 1, g, tq, HEAD_DIM), lambda bi, h, i: (bi, h, 0, i, 0)),
                  pl.BlockSpec((1, g * tq, HEAD_DIM), lambda bi, h, i: (h, 0, 0)),
                  pl.BlockSpec((1, 1, n_sub, LANES), lambda bi, h, i: (bi, h, 0, 0)),
                  pl.BlockSpec((1, 1, n_sub, HEAD_DIM), lambda bi, h, i: (bi, h, 0, 0)),
                  pl.BlockSpec((n_sub, n_sel), lambda bi, h, i: (0, 0))],
        out_specs=[pl.BlockSpec((1, tq, g * HEAD_DIM), lambda bi, h, i: (bi, i, h)),
                   pl.BlockSpec((1, 1, tq, n_sel), lambda bi, h, i: (bi, h, i, 0))],
        out_shape=[jax.ShapeDtypeStruct((b, s, hq * HEAD_DIM), F32),
                   jax.ShapeDtypeStruct((b, hkv, s, n_sel), F32)],
        compiler_params=_params(("parallel", "parallel", "parallel")),
        name="nsa_cmp_attention",
    )(q.reshape(b, hkv, g, s, HEAD_DIM), _sel_query_aug(hq, hkv, tq), kc_aug, vc,
      jnp.asarray(overlap, BF16))


def _topk_body(imp_ref, ch_ref, sel_ref, need_ref, *, tq, n_top, sub):
    i = pl.program_id(2)
    n_sel = imp_ref.shape[3]
    t_q = i * tq + lax.broadcasted_iota(jnp.int32, (tq, 1), 0)
    cur = t_q // SEL_BLOCK
    j = lax.broadcasted_iota(jnp.int32, (1, n_sel), 1)
    forced = (j == 0) | (j == cur) | (j == cur - 1)
    causal = j <= cur
    imp = jnp.where(forced, FORCE, jnp.where(causal, imp_ref[0, 0], -1.0))
    sel = jnp.zeros((tq, n_sel), F32)
    for _ in range(n_top):
        mx = jnp.max(imp, axis=-1, keepdims=True)
        first = jnp.min(jnp.where(imp == mx, j, n_sel), axis=-1, keepdims=True)
        hit = j == first
        sel = jnp.where(hit, 1.0, sel)
        imp = jnp.where(hit, -2.0, imp)
    sel = jnp.where(causal, sel, 0.0)
    sel_ref[0, 0] = sel.astype(sel_ref.dtype)
    for u in range(tq // sub):
        any_row = jnp.max(sel[u * sub:(u + 1) * sub], axis=0, keepdims=True)
        hits = jnp.dot(jnp.broadcast_to(any_row, (8, n_sel)).astype(BF16), ch_ref[...],
                       preferred_element_type=F32)
        need_ref[0, 0, u] = jnp.where(hits > 0.5, 1, 0).astype(jnp.int32)


def _topk_select(imp, sub, sel_tk, tq=512):
    b, hkv, s, n_sel = imp.shape
    tq = min(tq, s)
    n_top = min(SEL_TOPK, n_sel)
    assert s // sel_tk <= LANES
    chunk_of = (np.arange(n_sel)[:, None] * SEL_BLOCK // sel_tk == np.arange(LANES)[None, :])
    return pl.pallas_call(
        functools.partial(_topk_body, tq=tq, n_top=n_top, sub=sub),
        grid=(b, hkv, s // tq),
        in_specs=[pl.BlockSpec((1, 1, tq, n_sel), lambda bi, h, i: (bi, h, i, 0)),
                  pl.BlockSpec((n_sel, LANES), lambda bi, h, i: (0, 0))],
        out_specs=[pl.BlockSpec((1, 1, tq, n_sel), lambda bi, h, i: (bi, h, i, 0)),
                   pl.BlockSpec((1, 1, tq // sub, 8, LANES), lambda bi, h, i: (bi, h, i, 0, 0))],
        out_shape=[jax.ShapeDtypeStruct((b, hkv, s, n_sel), BF16),
                   jax.ShapeDtypeStruct((b, hkv, s // sub, 8, LANES), jnp.int32)],
        compiler_params=_params(("parallel", "parallel", "parallel")),
        name="nsa_topk_select",
    )(imp, jnp.asarray(chunk_of.astype(np.float32), BF16))


AUG_POS_HI, AUG_POS_LO, AUG_SEL = 0, 3, 6
MASK_BIG = 2.0 ** 100


def _sel_query_aug(n_heads, n_kv, tq):
    g = n_heads // n_kv
    sl = jnp.asarray(np.asarray(_alibi_slopes(n_heads), np.float32) * np.float32(LOG2E))
    hi = sl.astype(BF16).astype(F32)
    mid = (sl - hi).astype(BF16).astype(F32)
    lo = (sl - hi - mid).astype(BF16).astype(F32)
    pieces = jnp.stack([hi, mid, lo], axis=-1)
    cols = jnp.zeros((n_heads, HEAD_DIM), F32)
    cols = cols.at[:, AUG_POS_HI:AUG_POS_HI + 3].set(pieces * LANES)
    cols = cols.at[:, AUG_POS_LO:AUG_POS_LO + 3].set(pieces)
    cols = jnp.broadcast_to(cols.reshape(n_kv, g, 1, HEAD_DIM), (n_kv, g, tq, HEAD_DIM))
    return cols.reshape(n_kv, g * tq, HEAD_DIM).astype(BF16)


def _sel_key_aug(tk, rows):
    c = np.arange(rows) % tk
    cols = np.zeros((rows, HEAD_DIM), np.float32)
    cols[:, AUG_POS_HI:AUG_POS_HI + 3] = (c // 256 * 2)[:, None]
    cols[:, AUG_POS_LO:AUG_POS_LO + 3] = (c % 256)[:, None]
    cols[np.arange(rows), AUG_SEL + c // SEL_BLOCK] = MASK_BIG
    return jnp.asarray(cols, BF16)


def _sel_attn_body(need_ref, q_ref, qa_ref, k_ref, v_ref, sel_ref, o_ref,
                   qaug_ref, m_ref, acc_ref, *, g, tq, tk, n_chunks_total):
    bi, h, i = pl.program_id(0), pl.program_id(1), pl.program_id(2)
    n_sel = sel_ref.shape[3]
    per_chunk = tk // SEL_BLOCK
    rows = g * tq
    qaug_ref[...] = jnp.concatenate([q_ref[0, 0].reshape(rows, HEAD_DIM), qa_ref[0]], axis=-1)
    m_ref[...] = jnp.full(m_ref.shape, M_INIT, F32)
    acc_ref[...] = jnp.zeros(acc_ref.shape, F32)
    sel = sel_ref[0, 0]
    lane = lax.broadcasted_iota(jnp.int32, (1, LANES), 1)
    pos_lanes = (lane >= HEAD_DIM + AUG_POS_HI) & (lane < HEAD_DIM + AUG_POS_HI + 3)
    sel_lanes = (lane >= HEAD_DIM + AUG_SEL) & (lane < HEAD_DIM + AUG_SEL + per_chunk)
    jrow = lax.broadcasted_iota(jnp.int32, (n_sel, 1), 0)
    r = lax.broadcasted_iota(jnp.int32, (rows, 1), 0) % tq
    c = lax.broadcasted_iota(jnp.int32, (1, tk), 1)

    n_parts = 1

    def chunk(ci, diagonal):
        start = pl.multiple_of(ci * tk, tk)
        shift = (ci * (tk // LANES) - i * (tq // LANES)).astype(F32)
        k = k_ref[0, 0, pl.ds(start, tk), :] + jnp.where(pos_lanes, shift, 0.0).astype(BF16)
        v = v_ref[0, 0, pl.ds(start, tk), :]
        place = jnp.where(sel_lanes & (jrow == ci * per_chunk + lane - (HEAD_DIM + AUG_SEL)), 1.0, 0.0)
        picked = jnp.dot(sel, place.astype(BF16), preferred_element_type=F32)
        unpicked = jnp.where(sel_lanes, picked - 1.0, 0.0).astype(BF16)
        for part in range(n_parts):
            rs = slice(part * rows // n_parts, (part + 1) * rows // n_parts)
            s = _nt(qaug_ref[rs, :] + jnp.concatenate([unpicked] * (g // n_parts), axis=0), k)
            if diagonal:
                s = jnp.where(r[rs] + (i * tq - ci * tk) >= c, s, NEG)
            m_old = m_ref[rs, :]
            m_new = jnp.maximum(m_old, jnp.max(s, axis=-1, keepdims=True))
            e = jnp.exp2(s - jnp.concatenate([m_new] * (tk // LANES), axis=1))
            acc_ref[rs, :] = (jnp.exp2(m_old - m_new) * acc_ref[rs, :]
                              + jnp.dot(e.astype(BF16), v, preferred_element_type=F32))
            m_ref[rs, :] = m_new

    diag = (i * tq) // tk
    base = ((bi * pl.num_programs(1) + h) * pl.num_programs(2) + i) * n_chunks_total

    def maybe(ci, carry):
        @pl.when(need_ref[base + ci] != 0)
        def _():
            chunk(ci, False)
        return carry

    lax.fori_loop(0, diag, maybe, 0)
    chunk(diag, True)
    acc = acc_ref[...]
    o = (acc / pltpu.roll(acc, HEAD_DIM, 1))[:, :HEAD_DIM]
    o_ref[0] = jnp.concatenate([o[gi * tq:(gi + 1) * tq] for gi in range(g)], axis=-1)


def _sel_attention(q, k_aug, v, sel, need, tq, tk):
    b, hq, s, _ = q.shape
    hkv = k_aug.shape[1]
    g = hq // hkv
    n_sel = sel.shape[3]
    grid_spec = pltpu.PrefetchScalarGridSpec(
        num_scalar_prefetch=1,
        grid=(b, hkv, s // tq),
        in_specs=[pl.BlockSpec((1, 1, g, tq, HEAD_DIM), lambda bi, h, i, nd: (bi, h, 0, i, 0)),
                  pl.BlockSpec((1, g * tq, HEAD_DIM), lambda bi, h, i, nd: (h, 0, 0)),
                  pl.BlockSpec((1, 1, s, LANES), lambda bi, h, i, nd: (bi, h, 0, 0)),
                  pl.BlockSpec((1, 1, s, LANES), lambda bi, h, i, nd: (bi, h, 0, 0)),
                  pl.BlockSpec((1, 1, tq, n_sel), lambda bi, h, i, nd: (bi, h, i, 0))],
        out_specs=pl.BlockSpec((1, tq, g * HEAD_DIM), lambda bi, h, i, nd: (bi, i, h)),
        scratch_shapes=[pltpu.VMEM((g * tq, LANES), BF16),
                        pltpu.VMEM((g * tq, LANES), F32), pltpu.VMEM((g * tq, LANES), F32)])
    return pl.pallas_call(
        functools.partial(_sel_attn_body, g=g, tq=tq, tk=tk, n_chunks_total=s // tk),
        grid_spec=grid_spec,
        out_shape=jax.ShapeDtypeStruct((b, s, hq * HEAD_DIM), F32),
        compiler_params=_params(("parallel", "parallel", "arbitrary")),
        name="nsa_selected_attention",
    )(need, q.reshape(b, hkv, g, s, HEAD_DIM), _sel_query_aug(hq, hkv, tq), k_aug, v, sel)


def _nsa_out_body(x_ref, gt_ref, oc_ref, os_ref, ow_ref, ex_ref, w_ref, o_ref):
    sig = _sigmoid(gt_ref[...])
    mix = jnp.zeros(oc_ref.shape, F32)
    for j, br_ref in enumerate((oc_ref, os_ref, ow_ref)):
        gate = jnp.dot(sig, ex_ref[j], precision=HIGHEST, preferred_element_type=F32)
        mix = mix + gate * br_ref[...]
    o_ref[...] = x_ref[...] + jnp.dot(mix.astype(BF16), w_ref[...], preferred_element_type=F32)


def _nsa_out(x, gates, o_cmp, o_sel, o_win, w_out, tm=256):
    t, d = x.shape
    c = o_cmp.shape[1]
    expand = np.zeros((3, LANES, c), np.float32)
    for head in range(NSA_HQ):
        for j in range(3):
            expand[j, head * 3 + j, head * HEAD_DIM:(head + 1) * HEAD_DIM] = 1.0
    row = lambda width: pl.BlockSpec((tm, width), lambda i: (i, 0))
    return pl.pallas_call(
        _nsa_out_body,
        grid=(t // tm,),
        in_specs=[row(d), row(LANES), row(c), row(c), row(c),
                  pl.BlockSpec((3, LANES, c), lambda i: (0, 0, 0)),
                  pl.BlockSpec((c, d), lambda i: (0, 0))],
        out_specs=row(d),
        out_shape=jax.ShapeDtypeStruct((t, d), F32),
        compiler_params=_params(("parallel",)),
        name="nsa_out",
    )(x, gates, o_cmp, o_sel, o_win, jnp.asarray(expand), w_out)


def _nsa_layer(x, b, s, mix_norm, w_in, w_out, q_norm, k_norm, cmp_pos, cmp_w1, cmp_w2):
    dh = HEAD_DIM
    kvw = NSA_HKV * dh
    edges = np.cumsum([0, NSA_HQ * dh] + [kvw] * 6).tolist()
    ws = [w_in[:, edges[j]:edges[j + 1]].astype(BF16) for j in range(7)]
    n_gate = 3 * NSA_HQ
    ws.append(jnp.pad(w_in[:, edges[7]:edges[7] + n_gate], ((0, 0), (0, LANES - n_gate))).astype(BF16))
    q, kc, vc, ks, vs, kw, vw, gates = _norm_proj(x, mix_norm, ws)
    as3 = lambda y: y.reshape(b, s, y.shape[-1])
    qh = _head_prep(as3(q), q_norm, dh ** -0.5 * LOG2E, BF16)
    kch = _compress(_head_prep(as3(kc), None, 1.0, F32), cmp_pos[0], cmp_w1[0], cmp_w2[0], k_norm[0],
                    extra=_cmp_key_lanes(s // CMP_STRIDE))
    vch = _compress(_head_prep(as3(vc), None, 1.0, F32), cmp_pos[1], cmp_w1[1], cmp_w2[1], None)
    sel_tq, sel_tk, prep_ts = 256, min(512, s), min(512, s)
    ksh = _head_prep(as3(ks), k_norm[1], 1.0, BF16, extra=_sel_key_aug(sel_tk, prep_ts), ts=prep_ts)
    win_tq = min(512, s)
    vsh = _head_prep(as3(vs), None, 1.0, BF16, extra=_ones_lanes(prep_ts), ts=prep_ts)
    kwh = _head_prep(as3(kw), k_norm[2], 1.0, BF16, extra=_key_position_lanes(BAND_SUB, prep_ts), ts=prep_ts)
    vwh = _head_prep(as3(vw), None, 1.0, BF16, extra=_ones_lanes(prep_ts), ts=prep_ts)
    o_cmp, imp = _cmp_attention(qh, kch, vch, tq=128)
    sel, need = _topk_select(imp, sel_tq, sel_tk)
    need = need[:, :, :, 0, :s // sel_tk].reshape(-1)
    o_sel = _sel_attention(qh, ksh, vsh, sel, need, sel_tq, sel_tk)
    o_win = _banded_attention(qh, kwh, vwh, NSA_HQ, NSA_WINDOW, tq=win_tq)
    flat = lambda y: y.reshape(b * s, y.shape[-1])
    return _nsa_out(x, gates, flat(o_cmp), flat(o_sel), flat(o_win), w_out.astype(BF16))


def _diff_prep_body(y_ref, gq_ref, gk_ref, ka_ref, q_ref, k_ref, v_ref, *, width, q_scale):
    lane = lax.broadcasted_iota(jnp.int32, (1, LANES), 1)
    low = lane < HEAD_DIM

    def pair_norm(ysl, gain):
        sq = ysl * ysl
        ss_lo = jnp.sum(jnp.where(low, sq, 0.0), axis=-1, keepdims=True)
        ss_hi = jnp.sum(jnp.where(low, 0.0, sq), axis=-1, keepdims=True)
        inv = jnp.where(low, lax.rsqrt(ss_lo / HEAD_DIM + EPS), lax.rsqrt(ss_hi / HEAD_DIM + EPS))
        return ysl * inv * gain

    for j in range(width // LANES):
        sl = slice(j * LANES, (j + 1) * LANES)
        q_ref[:, sl] = (pair_norm(y_ref[:, sl], gq_ref[...]) * q_scale).astype(BF16)
        k_ref[:, 2 * j * LANES:(2 * j + 1) * LANES] = pair_norm(
            y_ref[:, width + j * LANES:width + (j + 1) * LANES], gk_ref[...]).astype(BF16)
        k_ref[:, (2 * j + 1) * LANES:(2 * j + 2) * LANES] = ka_ref[...]
        v_ref[:, sl] = y_ref[:, 2 * width + j * LANES:2 * width + (j + 1) * LANES].astype(BF16)


def _diff_prep(qkv, q_norm, k_norm, tk, tm=512):
    t, w3 = qkv.shape
    width = w3 // 3
    pair = lambda gvec: jnp.concatenate([gvec, gvec]).reshape(1, LANES).astype(F32)
    key_aug = jnp.pad(_sel_key_aug(tk, tm)[:, :AUG_SEL], ((0, 0), (0, LANES - AUG_SEL)))
    out = jax.ShapeDtypeStruct((t, width), BF16)
    return pl.pallas_call(
        functools.partial(_diff_prep_body, width=width, q_scale=HEAD_DIM ** -0.5 * LOG2E),
        grid=(t // tm,),
        in_specs=[pl.BlockSpec((tm, w3), lambda i: (i, 0)),
                  pl.BlockSpec((1, LANES), lambda i: (0, 0)),
                  pl.BlockSpec((1, LANES), lambda i: (0, 0)),
                  pl.BlockSpec((tm, LANES), lambda i: (0, 0))],
        out_specs=[pl.BlockSpec((tm, width), lambda i: (i, 0)),
                   pl.BlockSpec((tm, 2 * width), lambda i: (i, 0)),
                   pl.BlockSpec((tm, width), lambda i: (i, 0))],
        out_shape=[out, jax.ShapeDtypeStruct((t, 2 * width), BF16), out],
        compiler_params=_params(("parallel",)),
        name="diff_prep",
    )(qkv, pair(q_norm), pair(k_norm), key_aug)


def _diff_flash_body(reach_ref, q_ref, qa_ref, k_ref, v_ref, lam_ref, sub_ref, o_ref,
                     *, tq, tk, lam_init):
    h, i = pl.program_id(1), pl.program_id(2)
    q = q_ref[0]
    lane = lax.broadcasted_iota(jnp.int32, (1, LANES), 1)
    zero = jnp.zeros_like(q)
    q_both = jnp.concatenate([jnp.where(lane < HEAD_DIM, q, zero),
                              jnp.where(lane < HEAD_DIM, zero, q)], axis=0)
    q_aug = jnp.concatenate([q_both, qa_ref[0]], axis=-1)
    lane2 = lax.broadcasted_iota(jnp.int32, (1, 2 * LANES), 1)
    pos_lanes = (lane2 >= LANES + AUG_POS_HI) & (lane2 < LANES + AUG_POS_HI + 3)
    r = lax.broadcasted_iota(jnp.int32, (2 * tq, 1), 0) % tq
    c = lax.broadcasted_iota(jnp.int32, (1, tk), 1)

    def step(ci, carry, diagonal):
        m, l, acc = carry
        start = pl.multiple_of(ci * tk, tk)
        shift = (ci * (tk // LANES) - i * (tq // LANES)).astype(F32)
        k = k_ref[0, pl.ds(start, tk), :] + jnp.where(pos_lanes, shift, 0.0).astype(BF16)
        v = v_ref[0, pl.ds(start, tk), :]
        s = _nt(q_aug, k)
        if diagonal:
            s = jnp.where(r + (i * tq - ci * tk) >= c, s, NEG)
        m_new = jnp.maximum(m, jnp.max(s, axis=-1, keepdims=True))
        alpha = jnp.exp2(m - m_new)
        e = jnp.exp2(s - jnp.concatenate([m_new] * (tk // LANES), axis=1))
        l = alpha * l + jnp.sum(e, axis=-1, keepdims=True)
        acc = alpha * acc + jnp.dot(e.astype(BF16), v, preferred_element_type=F32)
        return m_new, l, acc

    init = (jnp.full((2 * tq, LANES), M_INIT, F32), jnp.zeros((2 * tq, LANES), F32),
            jnp.zeros((2 * tq, LANES), F32))
    diag = (i * tq) // tk
    first = jnp.maximum(i * tq - reach_ref[h], 0) // tk
    carry = lax.fori_loop(first, diag, lambda ci, cr: step(ci, cr, False), init)
    _, l, acc = step(diag, carry, True)
    lam = lam_ref[...]
    lam01 = jnp.sum(lam[0:1] * lam[1:2], axis=-1, keepdims=True)
    lam23 = jnp.sum(lam[2:3] * lam[3:4], axis=-1, keepdims=True)
    lmbda = jnp.exp(lam01) - jnp.exp(lam23) + lam_init
    o = acc / l
    o = o[:tq] - lmbda * o[tq:]
    o_ref[0] = _rms(o, sub_ref[...]) * (1.0 - lam_init)


def _diff_flash(qn, k_aug, vb, reach, lam, subln, lam_init, tq, tk):
    b, s, w = qn.shape
    n_heads = w // LANES
    sl = jnp.asarray(np.asarray(_alibi_slopes(n_heads), np.float32) * np.float32(LOG2E))
    hi = sl.astype(BF16).astype(F32)
    mid = (sl - hi).astype(BF16).astype(F32)
    lo = (sl - hi - mid).astype(BF16).astype(F32)
    pieces = jnp.stack([hi, mid, lo], axis=-1)
    q_aug = jnp.zeros((n_heads, LANES), F32)
    q_aug = q_aug.at[:, AUG_POS_HI:AUG_POS_HI + 3].set(pieces * LANES)
    q_aug = q_aug.at[:, AUG_POS_LO:AUG_POS_LO + 3].set(pieces)
    q_aug = jnp.broadcast_to(q_aug[:, None, :], (n_heads, 2 * tq, LANES)).astype(BF16)
    grid_spec = pltpu.PrefetchScalarGridSpec(
        num_scalar_prefetch=1,
        grid=(b, n_heads, s // tq),
        in_specs=[pl.BlockSpec((1, tq, LANES), lambda bi, h, i, rc: (bi, i, h)),
                  pl.BlockSpec((1, 2 * tq, LANES), lambda bi, h, i, rc: (h, 0, 0)),
                  pl.BlockSpec((1, s, 2 * LANES), lambda bi, h, i, rc: (bi, 0, h)),
                  pl.BlockSpec((1, s, LANES), lambda bi, h, i, rc: (bi, 0, h)),
                  pl.BlockSpec(lam.shape, lambda bi, h, i, rc: (0, 0)),
                  pl.BlockSpec((1, LANES), lambda bi, h, i, rc: (0, 0))],
        out_specs=pl.BlockSpec((1, tq, LANES), lambda bi, h, i, rc: (bi, i, h)))
    return pl.pallas_call(
        functools.partial(_diff_flash_body, tq=tq, tk=tk, lam_init=lam_init),
        grid_spec=grid_spec,
        out_shape=jax.ShapeDtypeStruct((b, s, w), F32),
        compiler_params=_params(("parallel", "parallel", "arbitrary")),
        name="diff_flash",
    )(reach, qn, q_aug, k_aug, vb, lam.astype(F32), subln.reshape(1, LANES).astype(F32))


def _diff_reach(q_norm, k_norm, n_heads, s):
    bound = (HEAD_DIM * jnp.max(jnp.abs(q_norm)) * jnp.max(jnp.abs(k_norm))
             * (HEAD_DIM ** -0.5 * LOG2E) * 1.02)
    slopes = jnp.asarray(np.asarray(_alibi_slopes(n_heads), np.float32) * np.float32(LOG2E))
    reach = jnp.ceil((2.0 * bound + 152.0) / slopes) + 1.0
    return jnp.minimum(reach, float(s)).astype(jnp.int32)


def _diff_layer(x, b, s, mix_norm, w_in, w_out, q_norm, k_norm, lam, subln, lam_init):
    tq, tk = min(512, s), min(512, s)
    (qkv,) = _norm_proj(x, mix_norm, [w_in.astype(BF16)])
    qn, k_aug, vb = _diff_prep(qkv, q_norm, k_norm, tk)
    as3 = lambda y: y.reshape(b, s, y.shape[-1])
    o = _diff_flash(as3(qn), as3(k_aug), as3(vb), _diff_reach(q_norm, k_norm, DIFF_H, s),
                    lam, subln, lam_init, tq, tk)
    return _proj_residual(x, o.reshape(b * s, -1), w_out.astype(BF16))


def _gdn_prep_body(cur_ref, prev_ref, cw_ref, b_ref, a_ref, alog_ref, dt_ref,
                   q_ref, k_ref, v_ref, beta_ref, gc_ref, pad_ref, *, ts, width):
    i = pl.program_id(1)
    halo = prev_ref.shape[1]
    pad_ref[0:halo, :] = jnp.where(i > 0, prev_ref[0], 0.0)
    pad_ref[halo:, :] = cur_ref[0]
    y = jnp.zeros((ts, 3 * width), F32)
    for j in range(GDN_CONV):
        off = halo - (GDN_CONV - 1) + j
        y = y + cw_ref[j:j + 1, :] * pad_ref[off:off + ts, :]
    y = y * _sigmoid(y)
    for h in range(width // GDN_DK):
        sl = slice(h * GDN_DK, (h + 1) * GDN_DK)
        qh = y[:, sl]
        q_ref[0, :, sl] = qh * lax.rsqrt(jnp.sum(qh * qh, axis=-1, keepdims=True) + EPS) * (GDN_DK ** -0.5)
        kh = y[:, width + h * GDN_DK:width + (h + 1) * GDN_DK]
        k_ref[0, :, sl] = kh * lax.rsqrt(jnp.sum(kh * kh, axis=-1, keepdims=True) + EPS)
    v_ref[0] = y[:, 2 * width:]
    beta_ref[0] = _sigmoid(b_ref[0])
    z = a_ref[0] + dt_ref[...]
    softplus = jnp.maximum(z, 0.0) + jnp.log(1.0 + jnp.exp(-jnp.abs(z)))
    gl = -jnp.exp(alog_ref[...]) * softplus
    r = lax.broadcasted_iota(jnp.int32, (ts, ts), 0)
    c = lax.broadcasted_iota(jnp.int32, (ts, ts), 1)
    tri = jnp.where((r // GDN_CHUNK == c // GDN_CHUNK) & (c <= r), 1.0, 0.0)
    gc_ref[0] = jnp.dot(tri, gl, precision=HIGHEST, preferred_element_type=F32)


def _gdn_prep(qkv, b_pad, a_pad, conv_w, a_log, dt_bias, ts=256):
    b, s, w3 = qkv.shape
    width = w3 // 3
    halo = 8
    pad_lane = lambda vec: jnp.pad(vec.astype(F32), (0, LANES - vec.shape[0])).reshape(1, LANES)
    big = jax.ShapeDtypeStruct((b, s, width), F32)
    small = jax.ShapeDtypeStruct((b, s, LANES), F32)
    tok = lambda wd: pl.BlockSpec((1, ts, wd), lambda bi, i: (bi, i, 0))
    return pl.pallas_call(
        functools.partial(_gdn_prep_body, ts=ts, width=width),
        grid=(b, s // ts),
        in_specs=[tok(w3),
                  pl.BlockSpec((1, halo, w3), lambda bi, i: (bi, jnp.maximum(i * (ts // halo) - 1, 0), 0)),
                  pl.BlockSpec((GDN_CONV, w3), lambda bi, i: (0, 0)),
                  tok(LANES), tok(LANES),
                  pl.BlockSpec((1, LANES), lambda bi, i: (0, 0)),
                  pl.BlockSpec((1, LANES), lambda bi, i: (0, 0))],
        out_specs=[tok(width), tok(width), tok(width), tok(LANES), tok(LANES)],
        out_shape=[big, big, big, small, small],
        scratch_shapes=[pltpu.VMEM((ts + halo, w3), F32)],
        compiler_params=_params(("parallel", "parallel")),
        name="gdn_prep",
    )(qkv, qkv, conv_w.astype(F32), b_pad, a_pad, pad_lane(a_log), pad_lane(dt_bias))


def _unit_lower_inverses(mats):
    n = mats[0].shape[0]
    r = lax.broadcasted_iota(jnp.int32, (n, n), 0)
    c = lax.broadcasted_iota(jnp.int32, (n, n), 1)
    eye = jnp.where(r == c, 1.0, 0.0)
    d = 1
    inv = None
    while d < n:
        below = (r // (2 * d) == c // (2 * d)) & (r // d != c // d)
        offs = [jnp.where(below, a, 0.0) for a in mats]
        if d == 1:
            inv = [eye - off for off in offs]
        else:
            left = [_dot3(t, off) for t, off in zip(inv, offs)]
            inv = [t - _dot3(lt, t) for t, lt in zip(inv, left)]
        d *= 2
    return inv


def _gdn_chunk_body(q_ref, k_ref, v_ref, beta_ref, gc_ref, gr_ref, gate_ref, on_ref, o_ref,
                    state_ref, *, nc, n_heads):
    @pl.when(pl.program_id(1) == 0)
    def _():
        state_ref[...] = jnp.zeros(state_ref.shape, F32)

    cl = GDN_CHUNK
    r = lax.broadcasted_iota(jnp.int32, (cl, cl), 0)
    c = lax.broadcasted_iota(jnp.int32, (cl, cl), 1)
    fdot = functools.partial(jnp.dot, preferred_element_type=F32)
    blocks = [(ch, h) for ch in range(nc) for h in range(n_heads)]
    rows = lambda ch: slice(ch * cl, (ch + 1) * cl)
    lanes = lambda h: slice(h * GDN_DK, (h + 1) * GDN_DK)

    qs = [q_ref[0, rows(ch), lanes(h)] for ch, h in blocks]
    ks = [k_ref[0, rows(ch), lanes(h)] for ch, h in blocks]
    bcol = [beta_ref[0, rows(ch), h:h + 1] for ch, h in blocks]
    gcol = [gc_ref[0, rows(ch), h:h + 1] for ch, h in blocks]
    glast = [gc_ref[0, ch * cl + cl - 1:ch * cl + cl, h:h + 1] for ch, h in blocks]
    decay = [jnp.exp(jnp.minimum(g - gr_ref[0, ch, h:h + 1, :], 0.0))
             for g, (ch, h) in zip(gcol, blocks)]
    kb = [k * b for k, b in zip(ks, bcol)]
    amat = [jnp.where(r > c, _dot3(x, k, ((1,), (1,))) * dc, 0.0) for x, k, dc in zip(kb, ks, decay)]
    inv = _unit_lower_inverses(amat)
    eg = [jnp.exp(g) for g in gcol]
    u = [_dot3(t, v_ref[0, rows(ch), lanes(h)] * b) for t, b, (ch, h) in zip(inv, bcol, blocks)]
    w = [_dot3(t, x * e) for t, x, e in zip(inv, kb, eg)]
    attn = [jnp.where(r >= c, _nt(q, k) * dc, 0.0) for q, k, dc in zip(qs, ks, decay)]

    for ch in range(nc):
        ids = range(ch * n_heads, (ch + 1) * n_heads)
        state = [state_ref[h] for h in range(n_heads)]
        v_new = [u[i] - fdot(w[i], st) for i, st in zip(ids, state)]
        o = [fdot(qs[i] * eg[i], st) + fdot(attn[i], vn) for i, st, vn in zip(ids, state, v_new)]
        for h, (i, st, vn) in enumerate(zip(ids, state, v_new)):
            state_ref[h] = st * jnp.exp(glast[i]) + _tn(ks[i] * jnp.exp(glast[i] - gcol[i]), vn)
        for h in range(n_heads):
            gate = gate_ref[0, rows(ch), lanes(h)]
            o_ref[0, rows(ch), lanes(h)] = _rms(o[h], on_ref[...]) * (gate * _sigmoid(gate))


def _gdn_chunks(q, k, v, beta, gc, gc_rows, gate, o_norm, nc=1):
    b, s, w = q.shape
    n_heads = w // GDN_DK
    ts = nc * GDN_CHUNK
    tok = lambda wd: pl.BlockSpec((1, ts, wd), lambda bi, i: (bi, i, 0))
    return pl.pallas_call(
        functools.partial(_gdn_chunk_body, nc=nc, n_heads=n_heads),
        grid=(b, s // ts),
        in_specs=[tok(w), tok(w), tok(w), tok(LANES), tok(LANES),
                  pl.BlockSpec((1, nc, 8, GDN_CHUNK), lambda bi, i: (bi, i, 0, 0)),
                  tok(w),
                  pl.BlockSpec((1, GDN_DV), lambda bi, i: (0, 0))],
        out_specs=tok(w),
        out_shape=jax.ShapeDtypeStruct((b, s, w), F32),
        scratch_shapes=[pltpu.VMEM((n_heads, GDN_DK, GDN_DV), F32)],
        compiler_params=_params(("parallel", "arbitrary")),
        name="gdn_chunks",
    )(q, k, v, beta, gc, gc_rows, gate, o_norm.reshape(1, GDN_DV).astype(F32))


def _gdn_layer(x, b, s, mix_norm, w_in, w_out, conv_w, a_log, dt_bias, o_norm):
    qkv_w = GDN_H * (2 * GDN_DK + GDN_DV)
    lane_pad = lambda wcols: jnp.pad(wcols, ((0, 0), (0, LANES - wcols.shape[1]))).astype(BF16)
    ws = [w_in[:, :qkv_w].astype(BF16),
          lane_pad(w_in[:, qkv_w:qkv_w + GDN_H]),
          lane_pad(w_in[:, qkv_w + GDN_H:qkv_w + 2 * GDN_H]),
          w_in[:, qkv_w + 2 * GDN_H:].astype(BF16)]
    qkv, b_pad, a_pad, gate = _norm_proj(x, mix_norm, ws)
    as3 = lambda y: y.reshape(b, s, y.shape[-1])
    q, k, v, beta, gc = _gdn_prep(as3(qkv), as3(b_pad), as3(a_pad), conv_w, a_log, dt_bias,
                                  ts=min(256, s))
    gc_rows = gc[:, :, :8].reshape(b, s // GDN_CHUNK, GDN_CHUNK, 8).transpose(0, 1, 3, 2)
    o = _gdn_chunks(q, k, v, beta, gc, gc_rows, as3(gate), o_norm)
    return _proj_residual(x, o.reshape(b * s, -1), w_out.astype(BF16))


def _swa_layer(x, b, s, mix_norm, w_in, w_out, q_norm, k_norm, sinks):
    dh = HEAD_DIM
    e0, e1 = SWA_HQ * dh, (SWA_HQ + SWA_HKV) * dh
    ws = [w_in[:, :e0].astype(BF16), w_in[:, e0:e1].astype(BF16), w_in[:, e1:].astype(BF16)]
    q, k, v = _norm_proj(x, mix_norm, ws)
    as3 = lambda y: y.reshape(b, s, y.shape[-1])
    qh = _head_prep(as3(q), q_norm, dh ** -0.5 * LOG2E, BF16)
    tq, prep_ts = min(512, s), min(512, s)
    kh = _head_prep(as3(k), k_norm, 1.0, BF16, extra=_key_position_lanes(BAND_SUB, prep_ts), ts=prep_ts)
    vh = _head_prep(as3(v), None, 1.0, BF16, extra=_ones_lanes(prep_ts), ts=prep_ts)
    o = _banded_attention(qh, kh, vh, SWA_HQ, SWA_WINDOW, tq=tq, sinks=sinks)
    return _proj_residual(x, o.reshape(b * s, -1), w_out.astype(BF16))


def kernel(x, ffn1_norm, ffn1_w_in, ffn1_w_out, mix_norm, ffn2_norm, ffn2_w_in, ffn2_w_out,
           nsa_w_in, nsa_w_out, nsa_q_norm, nsa_k_norm, nsa_cmp_pos, nsa_cmp_w1, nsa_cmp_w2,
           diff_w_in, diff_w_out, diff_q_norm, diff_k_norm, diff_lambda, diff_subln,
           gdn_w_in, gdn_w_out, gdn_conv_w, gdn_a_log, gdn_dt_bias, gdn_o_norm,
           swa_w_in, swa_w_out, swa_q_norm, swa_k_norm, swa_sinks):
    b, s, d = x.shape
    depth = ffn1_norm.shape[0]
    n_mixers = 4
    h = x.reshape(b * s, d)
    for layer in range(depth):
        kind, j = layer % n_mixers, layer // n_mixers
        h = _ffn(h, ffn1_norm[layer], ffn1_w_in[layer].astype(BF16), ffn1_w_out[layer].astype(BF16))
        if kind == 0:
            h = _nsa_layer(h, b, s, mix_norm[layer], nsa_w_in[j], nsa_w_out[j], nsa_q_norm[j],
                           nsa_k_norm[j], nsa_cmp_pos[j], nsa_cmp_w1[j], nsa_cmp_w2[j])
        elif kind == 1:
            lam_init = 0.8 - 0.6 * math.exp(-0.3 * layer)
            h = _diff_layer(h, b, s, mix_norm[layer], diff_w_in[j], diff_w_out[j], diff_q_norm[j],
                            diff_k_norm[j], diff_lambda[j], diff_subln[j], lam_init)
        elif kind == 2:
            h = _gdn_layer(h, b, s, mix_norm[layer], gdn_w_in[j], gdn_w_out[j], gdn_conv_w[j],
                           gdn_a_log[j], gdn_dt_bias[j], gdn_o_norm[j])
        else:
            h = _swa_layer(h, b, s, mix_norm[layer], swa_w_in[j], swa_w_out[j], swa_q_norm[j],
                           swa_k_norm[j], swa_sinks[j])
        h = _ffn(h, ffn2_norm[layer], ffn2_w_in[layer].astype(BF16), ffn2_w_out[layer].astype(BF16))
    return h.reshape(b, s, d)
```

```python
import functools
import math

import numpy as np
import jax
import jax.numpy as jnp
from jax import lax
from jax.experimental import pallas as pl
from jax.experimental.pallas import tpu as pltpu

F32 = jnp.float32
BF16 = jnp.bfloat16
HIGHEST = lax.Precision.HIGHEST

EPS = 1e-6
NEG = -1e30
M_INIT = -1e29
LOG2E = 1.4426950408889634
FORCE = 1e9

HEAD_DIM = 64
LANES = 128
VMEM_LIMIT = 56 * 1024 * 1024

NSA_HQ, NSA_HKV = 16, 4
CMP_BLOCK, CMP_STRIDE = 32, 16
SEL_BLOCK, SEL_TOPK = 64, 16
NSA_WINDOW = 512
DIFF_H = 8
GDN_H, GDN_DK, GDN_DV, GDN_CONV, GDN_CHUNK = 8, 128, 128, 4, 64
SWA_HQ, SWA_HKV, SWA_WINDOW = 16, 4, 128


def _params(sem):
    return pltpu.CompilerParams(dimension_semantics=sem, vmem_limit_bytes=VMEM_LIMIT)


def _nt(a, b, precision=None):
    return lax.dot_general(a, b, (((1,), (1,)), ((), ())), precision=precision,
                           preferred_element_type=F32)


def _tn(a, b, precision=None):
    return lax.dot_general(a, b, (((0,), (0,)), ((), ())), precision=precision,
                           preferred_element_type=F32)


def _dot3(a, b, contract=((1,), (0,))):
    a_hi, b_hi = a.astype(BF16), b.astype(BF16)
    a_lo = (a - a_hi.astype(F32)).astype(BF16)
    b_lo = (b - b_hi.astype(F32)).astype(BF16)
    dot = lambda x, y: lax.dot_general(x, y, (contract, ((), ())), preferred_element_type=F32)
    return dot(a_hi, b_hi) + dot(a_hi, b_lo) + dot(a_lo, b_hi)


def _rms(x, g):
    return x * lax.rsqrt(jnp.mean(x * x, axis=-1, keepdims=True) + EPS) * g


def _sigmoid(x):
    return 1.0 / (1.0 + jnp.exp(-x))


def _alibi_slopes(n):
    return [2.0 ** (-8.0 * (i + 1) / n) for i in range(n)]


def _ffn_body(x_ref, g_ref, win_ref, wout_ref, o_ref, *, d_ff, chunk):
    x = x_ref[...]
    xn = _rms(x, g_ref[...]).astype(BF16)
    acc = jnp.zeros(x.shape, F32)
    for c in range(d_ff // chunk):
        a = jnp.dot(xn, win_ref[:, c * chunk:(c + 1) * chunk], preferred_element_type=F32)
        b = jnp.dot(xn, win_ref[:, d_ff + c * chunk:d_ff + (c + 1) * chunk],
                    preferred_element_type=F32)
        h = (a * _sigmoid(a) * b).astype(BF16)
        acc = acc + jnp.dot(h, wout_ref[c * chunk:(c + 1) * chunk, :], preferred_element_type=F32)
    o_ref[...] = x + 0.5 * acc


def _ffn(x, g, w_in, w_out, tm=512, chunk=256):
    t, d = x.shape
    d_ff = w_out.shape[0]
    return pl.pallas_call(
        functools.partial(_ffn_body, d_ff=d_ff, chunk=chunk),
        grid=(t // tm,),
        in_specs=[pl.BlockSpec((tm, d), lambda i: (i, 0)),
                  pl.BlockSpec((1, d), lambda i: (0, 0)),
                  pl.BlockSpec(w_in.shape, lambda i: (0, 0), pipeline_mode=pl.Buffered(1)),
                  pl.BlockSpec(w_out.shape, lambda i: (0, 0), pipeline_mode=pl.Buffered(1))],
        out_specs=pl.BlockSpec((tm, d), lambda i: (i, 0)),
        out_shape=jax.ShapeDtypeStruct((t, d), F32),
        compiler_params=_params(("parallel",)),
        name="ffn",
    )(x, g.reshape(1, d), w_in, w_out)


def _norm_proj_body(x_ref, g_ref, *refs, n):
    xn = _rms(x_ref[...], g_ref[...]).astype(BF16)
    for w_ref, o_ref in zip(refs[:n], refs[n:]):
        o_ref[...] = jnp.dot(xn, w_ref[...], preferred_element_type=F32).astype(o_ref.dtype)


def _norm_proj(x, g, ws, tm=512):
    t, d = x.shape
    n = len(ws)
    return pl.pallas_call(
        functools.partial(_norm_proj_body, n=n),
        grid=(t // tm,),
        in_specs=[pl.BlockSpec((tm, d), lambda i: (i, 0)),
                  pl.BlockSpec((1, d), lambda i: (0, 0))]
                 + [pl.BlockSpec(w.shape, lambda i: (0, 0)) for w in ws],
        out_specs=[pl.BlockSpec((tm, w.shape[1]), lambda i: (i, 0)) for w in ws],
        out_shape=[jax.ShapeDtypeStruct((t, w.shape[1]), F32) for w in ws],
        compiler_params=_params(("parallel",)),
        name="norm_proj",
    )(x, g.reshape(1, d), *ws)


def _proj_res_body(x_ref, a_ref, w_ref, o_ref):
    o_ref[...] = x_ref[...] + jnp.dot(a_ref[...].astype(BF16), w_ref[...],
                                      preferred_element_type=F32)


def _proj_residual(x, a, w, tm=512):
    t, d = x.shape
    k = a.shape[1]
    return pl.pallas_call(
        _proj_res_body,
        grid=(t // tm,),
        in_specs=[pl.BlockSpec((tm, d), lambda i: (i, 0)),
                  pl.BlockSpec((tm, k), lambda i: (i, 0)),
                  pl.BlockSpec((k, d), lambda i: (0, 0))],
        out_specs=pl.BlockSpec((tm, d), lambda i: (i, 0)),
        out_shape=jax.ShapeDtypeStruct((t, d), F32),
        compiler_params=_params(("parallel",)),
        name="proj_residual",
    )(x, a, w)


def _head_prep_body(y_ref, g_ref, *refs, nh, norm, scale, has_extra):
    o_ref = refs[-1]
    for h in range(nh):
        yh = y_ref[0, :, h * HEAD_DIM:(h + 1) * HEAD_DIM]
        if norm:
            yh = _rms(yh, g_ref[...])
        if scale != 1.0:
            yh = yh * scale
        yh = yh.astype(o_ref.dtype)
        if has_extra:
            yh = jnp.concatenate([yh, refs[0][...]], axis=-1)
        o_ref[0, h] = yh


def _head_prep(y, gain, scale, out_dtype, extra=None, ts=512):
    b, s, c = y.shape
    nh = c // HEAD_DIM
    norm = gain is not None
    g = (gain if norm else jnp.ones((HEAD_DIM,), F32)).reshape(1, HEAD_DIM)
    width = HEAD_DIM if extra is None else 2 * HEAD_DIM
    in_specs = [pl.BlockSpec((1, ts, c), lambda bi, i: (bi, i, 0)),
                pl.BlockSpec((1, HEAD_DIM), lambda bi, i: (0, 0))]
    args = [y, g]
    if extra is not None:
        in_specs.append(pl.BlockSpec((ts, HEAD_DIM), lambda bi, i: (0, 0)))
        args.append(extra)
    return pl.pallas_call(
        functools.partial(_head_prep_body, nh=nh, norm=norm, scale=scale,
                          has_extra=extra is not None),
        grid=(b, s // ts),
        in_specs=in_specs,
        out_specs=pl.BlockSpec((1, nh, ts, width), lambda bi, i: (bi, 0, i, 0)),
        out_shape=jax.ShapeDtypeStruct((b, nh, s, width), out_dtype),
        compiler_params=_params(("parallel", "parallel")),
        name="head_prep",
    )(*args)


def _slope_rows(n_heads, n_kv, rows_per_head, mult=LOG2E):
    g = n_heads // n_kv
    sl = np.asarray(_alibi_slopes(n_heads), np.float32).reshape(n_kv, g, 1) * np.float32(mult)
    return jnp.asarray(np.repeat(sl, rows_per_head, axis=1).reshape(n_kv, g * rows_per_head, 1))


BAND_SUB = 128


def _banded_body(*refs, g, tq, window, has_sink):
    q_ref, qa_ref, kp_ref, kc_ref, vp_ref, vc_ref, mask_ref = refs[:7]
    o_ref = refs[-1]
    i = pl.program_id(2)
    sub = BAND_SUB
    rows = g * sub
    span = window + sub
    lane = lax.broadcasted_iota(jnp.int32, (1, LANES), 1)
    pos_lanes = (lane >= HEAD_DIM + AUG_POS_HI) & (lane < HEAD_DIM + AUG_POS_HI + 3)
    shifts = jnp.concatenate(
        [jnp.broadcast_to(jnp.where(pos_lanes, float(bk - window // sub), 0.0), (sub, LANES))
         for bk in range(span // sub)], axis=0).astype(BF16)
    k_win = jnp.concatenate([kp_ref[0, 0], kc_ref[0, 0]], axis=0)
    v_win = jnp.concatenate([vp_ref[0, 0], vc_ref[0, 0]], axis=0)
    col = lax.broadcasted_iota(jnp.int32, (1, span), 1)
    before_start = jnp.where(i > 0, 0.0, NEG)
    if has_sink:
        sl_ref, sink_ref = refs[7], refs[8]
        r = (lax.broadcasted_iota(jnp.int32, (rows, LANES), 0) % sub).astype(F32)
        sink = sink_ref[0] * LOG2E + sl_ref[0] * r
    units = range(tq // sub)
    q_aug = [jnp.concatenate([q_ref[0, 0, :, u * sub:(u + 1) * sub, :].reshape(rows, HEAD_DIM),
                              qa_ref[0]], axis=-1) for u in units]
    bias = [jnp.concatenate([mask_ref[...] + jnp.where(col < window - u * sub, before_start, 0.0)] * g,
                            axis=0) for u in units]
    s = [_nt(q_aug[u], k_win[u * sub:u * sub + span] + shifts) + bias[u] for u in units]
    m = [jnp.broadcast_to(jnp.max(su, axis=-1, keepdims=True), (rows, LANES)) for su in s]
    if has_sink:
        m = [jnp.maximum(mu, sink) for mu in m]
    e = [jnp.exp2(su - jnp.concatenate([mu] * (span // LANES), axis=1)).astype(BF16)
         for su, mu in zip(s, m)]
    acc = [jnp.dot(e[u], v_win[u * sub:u * sub + span], preferred_element_type=F32) for u in units]
    for u in units:
        if has_sink:
            acc[u] = acc[u] + jnp.where(lane >= HEAD_DIM, jnp.exp2(sink - m[u]), 0.0)
        o = (acc[u] / pltpu.roll(acc[u], HEAD_DIM, 1))[:, :HEAD_DIM]
        o_ref[0, u * sub:(u + 1) * sub, :] = jnp.concatenate(
            [o[gi * sub:(gi + 1) * sub] for gi in range(g)], axis=-1)


def _key_position_lanes(period, rows):
    assert rows % period == 0
    c = np.arange(rows) % period
    cols = np.zeros((rows, HEAD_DIM), np.float32)
    cols[:, AUG_POS_HI:AUG_POS_HI + 3] = (c // 256 * 2)[:, None]
    cols[:, AUG_POS_LO:AUG_POS_LO + 3] = (c % 256)[:, None]
    return jnp.asarray(cols, BF16)


def _ones_lanes(rows):
    return jnp.ones((rows, HEAD_DIM), BF16)


def _banded_attention(q, k_aug, v_aug, n_heads, window, tq, sinks=None):
    b, hq, s, _ = q.shape
    hkv = k_aug.shape[1]
    g = hq // hkv
    sub = BAND_SUB
    assert tq % window == 0 and window % sub == 0
    span = window + sub
    dist = np.arange(sub)[:, None] + window - np.arange(span)[None, :]
    mask = np.where((dist >= 0) & (dist < window), 0.0, NEG).astype(np.float32)
    per = tq // window
    prev_spec = pl.BlockSpec((1, 1, window, LANES), lambda bi, h, i: (bi, h, jnp.maximum(i * per - 1, 0), 0))
    cur_spec = pl.BlockSpec((1, 1, tq, LANES), lambda bi, h, i: (bi, h, i, 0))
    in_specs = [pl.BlockSpec((1, 1, g, tq, HEAD_DIM), lambda bi, h, i: (bi, h, 0, i, 0)),
                pl.BlockSpec((1, g * sub, HEAD_DIM), lambda bi, h, i: (h, 0, 0)),
                prev_spec, cur_spec, prev_spec, cur_spec,
                pl.BlockSpec((sub, span), lambda bi, h, i: (0, 0))]
    args = [q.reshape(b, hkv, g, s, HEAD_DIM), _sel_query_aug(n_heads, hkv, sub),
            k_aug, k_aug, v_aug, v_aug, jnp.asarray(mask)]
    if sinks is not None:
        in_specs += [pl.BlockSpec((1, g * sub, 1), lambda bi, h, i: (h, 0, 0))] * 2
        args += [_slope_rows(n_heads, hkv, sub),
                 jnp.repeat(sinks.astype(F32).reshape(hkv, g, 1), sub, axis=1).reshape(hkv, g * sub, 1)]
    return pl.pallas_call(
        functools.partial(_banded_body, g=g, tq=tq, window=window, has_sink=sinks is not None),
        grid=(b, hkv, s // tq),
        in_specs=in_specs,
        out_specs=pl.BlockSpec((1, tq, g * HEAD_DIM), lambda bi, h, i: (bi, i, h)),
        out_shape=jax.ShapeDtypeStruct((b, s, hq * HEAD_DIM), F32),
        compiler_params=_params(("parallel", "parallel", "parallel")),
        name="banded_attention",
    )(*args)


def _compress_body(t_ref, pt_ref, pb_ref, w1t_ref, w1b_ref, w2_ref, g_ref, *refs, norm):
    o_ref = refs[-1]
    sub = t_ref[0, 0]
    n_sub = sub.shape[0]
    a = jnp.dot((sub + pt_ref[...]).astype(BF16), w1t_ref[...], preferred_element_type=F32)
    bm = jnp.dot((sub + pb_ref[...]).astype(BF16), w1b_ref[...], preferred_element_type=F32)
    hid = a + pltpu.roll(bm, shift=n_sub - 1, axis=0)
    hid = 0.5 * hid * (1.0 + jnp.tanh(0.7978845608028654 * (hid + 0.044715 * hid * hid * hid)))
    out = jnp.dot(hid.astype(BF16), w2_ref[...], preferred_element_type=F32)
    if norm:
        out = _rms(out, g_ref[...])
    out = out.astype(BF16)
    if len(refs) == 2:
        out = jnp.concatenate([out, refs[0][...]], axis=-1)
    o_ref[0, 0] = out


def _compress(t, pos, w1, w2, gain, extra=None):
    b, h, s, _ = t.shape
    n_sub = s // CMP_STRIDE
    half = CMP_STRIDE * HEAD_DIM
    hidden = w1.shape[1]
    norm = gain is not None
    g = (gain if norm else jnp.ones((HEAD_DIM,), F32)).reshape(1, HEAD_DIM)
    const = lambda shape: pl.BlockSpec(shape, lambda bi, hi: (0,) * len(shape))
    width = HEAD_DIM if extra is None else 2 * HEAD_DIM
    extras = [] if extra is None else [extra]
    return pl.pallas_call(
        functools.partial(_compress_body, norm=norm),
        grid=(b, h),
        in_specs=[pl.BlockSpec((1, 1, n_sub, half), lambda bi, hi: (bi, hi, 0, 0)),
                  const((1, half)), const((1, half)),
                  const((half, hidden)), const((half, hidden)),
                  const((hidden, HEAD_DIM)), const((1, HEAD_DIM))]
                 + [const((n_sub, HEAD_DIM)) for _ in extras],
        out_specs=pl.BlockSpec((1, 1, n_sub, width), lambda bi, hi: (bi, hi, 0, 0)),
        out_shape=jax.ShapeDtypeStruct((b, h, n_sub, width), BF16),
        compiler_params=_params(("parallel", "parallel")),
        name="nsa_compress",
    )(t.reshape(b, h, n_sub, half),
      pos[:CMP_STRIDE].reshape(1, half), pos[CMP_STRIDE:].reshape(1, half),
      w1[:half].astype(BF16), w1[half:].astype(BF16), w2.astype(BF16), g, *extras)


def _cmp_key_lanes(n_sub):
    n = np.arange(n_sub)
    cols = np.zeros((n_sub, HEAD_DIM), np.float32)
    cols[:, AUG_POS_HI:AUG_POS_HI + 3] = (n // 8)[:, None]
    cols[:, AUG_POS_LO:AUG_POS_LO + 3] = (CMP_STRIDE * (n % 8) + CMP_BLOCK - 1)[:, None]
    return jnp.asarray(cols, BF16)


def _cmp_attn_body(q_ref, qa_ref, kc_ref, vc_ref, ov_ref, o_ref, imp_ref, *, g, tq, widths):
    i = pl.program_id(2)
    rows = g * tq
    q_aug = jnp.concatenate([q_ref[0, 0].reshape(rows, HEAD_DIM), qa_ref[0]], axis=-1)
    lane = lax.broadcasted_iota(jnp.int32, (1, LANES), 1)
    pos_lanes = (lane >= HEAD_DIM + AUG_POS_HI) & (lane < HEAD_DIM + AUG_POS_HI + 3)
    shift = jnp.where(pos_lanes, (-i * (tq // LANES)).astype(F32), 0.0).astype(BF16)
    t_row = i * tq + lax.broadcasted_iota(jnp.int32, (rows, 1), 0) % tq
    n_visible = (i * tq + tq) // CMP_STRIDE - 1

    def attend(n_cols):
        s = _nt(q_aug, kc_ref[0, 0, :n_cols, :] + shift)
        cmp_end = lax.broadcasted_iota(jnp.int32, (1, n_cols), 1) * CMP_STRIDE + (CMP_BLOCK - 1)
        s = jnp.where(t_row >= cmp_end, s, NEG)
        wide = lambda x: jnp.concatenate([jnp.broadcast_to(x, (rows, LANES))] * (n_cols // LANES), axis=1)
        e = jnp.exp2(s - wide(jnp.maximum(jnp.max(s, axis=-1, keepdims=True), M_INIT)))
        l = jnp.sum(e, axis=-1, keepdims=True)
        p = e * wide(1.0 / jnp.where(l > 0.0, l, 1.0))
        o = jnp.dot(p.astype(BF16), vc_ref[0, 0, :n_cols, :], preferred_element_type=F32)
        o_ref[0] = jnp.concatenate([o[gi * tq:(gi + 1) * tq] for gi in range(g)], axis=-1)
        psum = p[0:tq]
        for gi in range(1, g):
            psum = psum + p[gi * tq:(gi + 1) * tq]
        p_hi = psum.astype(BF16)
        r1 = psum - p_hi.astype(F32)
        p_mid = r1.astype(BF16)
        p_lo = (r1 - p_mid.astype(F32)).astype(BF16)
        ov = ov_ref[:n_cols, :]
        imp_ref[0, 0] = (jnp.dot(p_hi, ov, preferred_element_type=F32)
                         + jnp.dot(p_mid, ov, preferred_element_type=F32)
                         + jnp.dot(p_lo, ov, preferred_element_type=F32))

    for lo, hi in zip((0,) + widths[:-1], widths):
        pl.when((n_visible > lo) & (n_visible <= hi) if lo else n_visible <= hi)(
            functools.partial(attend, hi))


def _cmp_attention(q, kc_aug, vc, tq):
    b, hq, s, _ = q.shape
    hkv = kc_aug.shape[1]
    g = hq // hkv
    n_sub = kc_aug.shape[2]
    n_sel = s // SEL_BLOCK
    step = 2 * LANES
    widths = tuple(range(step, n_sub, step)) + (n_sub,) if n_sub >= 2 * step else (n_sub,)
    cs = np.arange(n_sub)[:, None] * CMP_STRIDE
    ss = np.arange(n_sel)[None, :] * SEL_BLOCK
    overlap = ((cs < ss + SEL_BLOCK) & (cs + CMP_BLOCK > ss)).astype(np.float32)
    return pl.pallas_call(
        functools.partial(_cmp_attn_body, g=g, tq=tq, widths=widths),
        grid=(b, hkv, s // tq),
        in_specs=[pl.BlockSpec((1, 1, g, tq, HEAD_DIM), lambda bi, h, i: (bi, h, 0, i, 0)),
                  pl.BlockSpec((1, g * tq, HEAD_DIM), lambda bi, h, i: (h, 0, 0)),
                  pl.BlockSpec((1, 1, n_sub, LANES), lambda bi, h, i: (bi, h, 0, 0)),
                  pl.BlockSpec((1, 1, n_sub, HEAD_DIM), lambda bi, h, i: (bi, h, 0, 0)),
                  pl.BlockSpec((n_sub, n_sel), lambda bi, h, i: (0, 0))],
        out_specs=[pl.BlockSpec((1, tq, g * HEAD_DIM), lambda bi, h, i: (bi, i, h)),
                   pl.BlockSpec((1, 1, tq, n_sel), lambda bi, h, i: (bi, h, i, 0))],
        out_shape=[jax.ShapeDtypeStruct((b, s, hq * HEAD_DIM), F32),
                   jax.ShapeDtypeStruct((b, hkv, s, n_sel), F32)],
        compiler_params=_params(("parallel", "parallel", "parallel")),
        name="nsa_cmp_attention",
    )(q.reshape(b, hkv, g, s, HEAD_DIM), _sel_query_aug(hq, hkv, tq), kc_aug, vc,
      jnp.asarray(overlap, BF16))


def _topk_body(imp_ref, ch_ref, sel_ref, need_ref, *, tq, n_top, sub):
    i = pl.program_id(2)
    n_sel = imp_ref.shape[3]
    j = lax.broadcasted_iota(jnp.int32, (n_sel, 1), 0)
    cur = (i * tq + lax.broadcasted_iota(jnp.int32, (1, tq), 1)) // SEL_BLOCK
    forced = (j == 0) | (j == cur) | (j == cur - 1)
    causal = j <= cur
    imp = jnp.where(forced, FORCE, jnp.where(causal, imp_ref[0, 0].T, -1.0))
    for _ in range(n_top):
        mx = jnp.max(imp, axis=0, keepdims=True)
        first = jnp.min(jnp.where(imp == mx, j, n_sel), axis=0, keepdims=True)
        imp = jnp.where(j == first, -2.0, imp)
    sel = jnp.where(causal & (imp == -2.0), 1.0, 0.0).T
    sel_ref[0, 0] = sel.astype(sel_ref.dtype)
    for u in range(tq // sub):
        any_row = jnp.max(sel[u * sub:(u + 1) * sub], axis=0, keepdims=True)
        hits = jnp.dot(jnp.broadcast_to(any_row, (8, n_sel)).astype(BF16), ch_ref[...],
                       preferred_element_type=F32)
        need_ref[0, 0, u] = jnp.where(hits > 0.5, 1, 0).astype(jnp.int32)


def _topk_select(imp, sub, sel_tk, tq=512):
    b, hkv, s, n_sel = imp.shape
    tq = min(tq, s)
    n_top = min(SEL_TOPK, n_sel)
    assert s // sel_tk <= LANES
    chunk_of = (np.arange(n_sel)[:, None] * SEL_BLOCK // sel_tk == np.arange(LANES)[None, :])
    return pl.pallas_call(
        functools.partial(_topk_body, tq=tq, n_top=n_top, sub=sub),
        grid=(b, hkv, s // tq),
        in_specs=[pl.BlockSpec((1, 1, tq, n_sel), lambda bi, h, i: (bi, h, i, 0)),
                  pl.BlockSpec((n_sel, LANES), lambda bi, h, i: (0, 0))],
        out_specs=[pl.BlockSpec((1, 1, tq, n_sel), lambda bi, h, i: (bi, h, i, 0)),
                   pl.BlockSpec((1, 1, tq // sub, 8, LANES), lambda bi, h, i: (bi, h, i, 0, 0))],
        out_shape=[jax.ShapeDtypeStruct((b, hkv, s, n_sel), BF16),
                   jax.ShapeDtypeStruct((b, hkv, s // sub, 8, LANES), jnp.int32)],
        compiler_params=_params(("parallel", "parallel", "parallel")),
        name="nsa_topk_select",
    )(imp, jnp.asarray(chunk_of.astype(np.float32), BF16))


AUG_POS_HI, AUG_POS_LO, AUG_SEL = 0, 3, 6
MASK_BIG = 2.0 ** 100


def _sel_query_aug(n_heads, n_kv, tq):
    g = n_heads // n_kv
    sl = jnp.asarray(np.asarray(_alibi_slopes(n_heads), np.float32) * np.float32(LOG2E))
    hi = sl.astype(BF16).astype(F32)
    mid = (sl - hi).astype(BF16).astype(F32)
    lo = (sl - hi - mid).astype(BF16).astype(F32)
    pieces = jnp.stack([hi, mid, lo], axis=-1)
    cols = jnp.zeros((n_heads, HEAD_DIM), F32)
    cols = cols.at[:, AUG_POS_HI:AUG_POS_HI + 3].set(pieces * LANES)
    cols = cols.at[:, AUG_POS_LO:AUG_POS_LO + 3].set(pieces)
    cols = jnp.broadcast_to(cols.reshape(n_kv, g, 1, HEAD_DIM), (n_kv, g, tq, HEAD_DIM))
    return cols.reshape(n_kv, g * tq, HEAD_DIM).astype(BF16)


def _sel_key_aug(tk, rows):
    assert rows % tk == 0
    c = np.arange(rows) % tk
    cols = np.zeros((rows, HEAD_DIM), np.float32)
    cols[:, AUG_POS_HI:AUG_POS_HI + 3] = (c // 256 * 2)[:, None]
    cols[:, AUG_POS_LO:AUG_POS_LO + 3] = (c % 256)[:, None]
    cols[np.arange(rows), AUG_SEL + c // SEL_BLOCK] = MASK_BIG
    return jnp.asarray(cols, BF16)


def _sel_attn_body(need_ref, q_ref, qa_ref, k_ref, v_ref, sel_ref, o_ref,
                   qaug_ref, m_ref, acc_ref, *, g, tq, tk, n_chunks_total):
    bi, h, i = pl.program_id(0), pl.program_id(1), pl.program_id(2)
    n_sel = sel_ref.shape[3]
    per_chunk = tk // SEL_BLOCK
    rows = g * tq
    qaug_ref[...] = jnp.concatenate([q_ref[0, 0].reshape(rows, HEAD_DIM), qa_ref[0]], axis=-1)
    m_ref[...] = jnp.full(m_ref.shape, M_INIT, F32)
    acc_ref[...] = jnp.zeros(acc_ref.shape, F32)
    sel = sel_ref[0, 0]
    lane = lax.broadcasted_iota(jnp.int32, (1, LANES), 1)
    pos_lanes = (lane >= HEAD_DIM + AUG_POS_HI) & (lane < HEAD_DIM + AUG_POS_HI + 3)
    sel_lanes = (lane >= HEAD_DIM + AUG_SEL) & (lane < HEAD_DIM + AUG_SEL + per_chunk)
    jrow = lax.broadcasted_iota(jnp.int32, (n_sel, 1), 0)
    r = lax.broadcasted_iota(jnp.int32, (rows, 1), 0) % tq
    c = lax.broadcasted_iota(jnp.int32, (1, tk), 1)

    n_parts = 1

    def chunk(ci, diagonal):
        start = pl.multiple_of(ci * tk, tk)
        shift = (ci * (tk // LANES) - i * (tq // LANES)).astype(F32)
        k = k_ref[0, 0, pl.ds(start, tk), :] + jnp.where(pos_lanes, shift, 0.0).astype(BF16)
        v = v_ref[0, 0, pl.ds(start, tk), :]
        place = jnp.where(sel_lanes & (jrow == ci * per_chunk + lane - (HEAD_DIM + AUG_SEL)), 1.0, 0.0)
        picked = jnp.dot(sel, place.astype(BF16), preferred_element_type=F32)
        unpicked = jnp.where(sel_lanes, picked - 1.0, 0.0).astype(BF16)
        for part in range(n_parts):
            rs = slice(part * rows // n_parts, (part + 1) * rows // n_parts)
            s = _nt(qaug_ref[rs, :] + jnp.concatenate([unpicked] * (g // n_parts), axis=0), k)
            if diagonal:
                s = jnp.where(r[rs] + (i * tq - ci * tk) >= c, s, NEG)
            m_old = m_ref[rs, :]
            m_new = jnp.maximum(m_old, jnp.max(s, axis=-1, keepdims=True))
            e = jnp.exp2(s - jnp.concatenate([m_new] * (tk // LANES), axis=1))
            acc_ref[rs, :] = (jnp.exp2(m_old - m_new) * acc_ref[rs, :]
                              + jnp.dot(e.astype(BF16), v, preferred_element_type=F32))
            m_ref[rs, :] = m_new

    diag = (i * tq) // tk
    base = ((bi * pl.num_programs(1) + h) * pl.num_programs(2) + i) * n_chunks_total

    def maybe(ci, carry):
        @pl.when(need_ref[base + ci] != 0)
        def _():
            chunk(ci, False)
        return carry

    lax.fori_loop(0, diag, maybe, 0)
    chunk(diag, True)
    acc = acc_ref[...]
    o = (acc / pltpu.roll(acc, HEAD_DIM, 1))[:, :HEAD_DIM]
    o_ref[0] = jnp.concatenate([o[gi * tq:(gi + 1) * tq] for gi in range(g)], axis=-1)


def _sel_attention(q, k_aug, v, sel, need, tq, tk):
    b, hq, s, _ = q.shape
    hkv = k_aug.shape[1]
    g = hq // hkv
    n_sel = sel.shape[3]
    grid_spec = pltpu.PrefetchScalarGridSpec(
        num_scalar_prefetch=1,
        grid=(b, hkv, s // tq),
        in_specs=[pl.BlockSpec((1, 1, g, tq, HEAD_DIM), lambda bi, h, i, nd: (bi, h, 0, i, 0)),
                  pl.BlockSpec((1, g * tq, HEAD_DIM), lambda bi, h, i, nd: (h, 0, 0)),
                  pl.BlockSpec((1, 1, s, LANES), lambda bi, h, i, nd: (bi, h, 0, 0)),
                  pl.BlockSpec((1, 1, s, LANES), lambda bi, h, i, nd: (bi, h, 0, 0)),
                  pl.BlockSpec((1, 1, tq, n_sel), lambda bi, h, i, nd: (bi, h, i, 0))],
        out_specs=pl.BlockSpec((1, tq, g * HEAD_DIM), lambda bi, h, i, nd: (bi, i, h)),
        scratch_shapes=[pltpu.VMEM((g * tq, LANES), BF16),
                        pltpu.VMEM((g * tq, LANES), F32), pltpu.VMEM((g * tq, LANES), F32)])
    return pl.pallas_call(
        functools.partial(_sel_attn_body, g=g, tq=tq, tk=tk, n_chunks_total=s // tk),
        grid_spec=grid_spec,
        out_shape=jax.ShapeDtypeStruct((b, s, hq * HEAD_DIM), F32),
        compiler_params=_params(("parallel", "parallel", "arbitrary")),
        name="nsa_selected_attention",
    )(need, q.reshape(b, hkv, g, s, HEAD_DIM), _sel_query_aug(hq, hkv, tq), k_aug, v, sel)


def _nsa_out_body(x_ref, gt_ref, oc_ref, os_ref, ow_ref, ex_ref, w_ref, o_ref):
    sig = _sigmoid(gt_ref[...])
    sig_hi = sig.astype(BF16)
    sig_lo = (sig - sig_hi.astype(F32)).astype(BF16)
    mix = jnp.zeros(oc_ref.shape, F32)
    for j, br_ref in enumerate((oc_ref, os_ref, ow_ref)):
        gate = (jnp.dot(sig_hi, ex_ref[j], preferred_element_type=F32)
                + jnp.dot(sig_lo, ex_ref[j], preferred_element_type=F32))
        mix = mix + gate * br_ref[...]
    o_ref[...] = x_ref[...] + jnp.dot(mix.astype(BF16), w_ref[...], preferred_element_type=F32)


def _nsa_out(x, gates, o_cmp, o_sel, o_win, w_out, tm=256):
    t, d = x.shape
    c = o_cmp.shape[1]
    expand = np.zeros((3, LANES, c), np.float32)
    for head in range(NSA_HQ):
        for j in range(3):
            expand[j, head * 3 + j, head * HEAD_DIM:(head + 1) * HEAD_DIM] = 1.0
    row = lambda width: pl.BlockSpec((tm, width), lambda i: (i, 0))
    return pl.pallas_call(
        _nsa_out_body,
        grid=(t // tm,),
        in_specs=[row(d), row(LANES), row(c), row(c), row(c),
                  pl.BlockSpec((3, LANES, c), lambda i: (0, 0, 0)),
                  pl.BlockSpec((c, d), lambda i: (0, 0))],
        out_specs=row(d),
        out_shape=jax.ShapeDtypeStruct((t, d), F32),
        compiler_params=_params(("parallel",)),
        name="nsa_out",
    )(x, gates, o_cmp, o_sel, o_win, jnp.asarray(expand, BF16), w_out)


def _nsa_layer(x, b, s, mix_norm, w_in, w_out, q_norm, k_norm, cmp_pos, cmp_w1, cmp_w2):
    dh = HEAD_DIM
    kvw = NSA_HKV * dh
    edges = np.cumsum([0, NSA_HQ * dh] + [kvw] * 6).tolist()
    ws = [w_in[:, edges[j]:edges[j + 1]].astype(BF16) for j in range(7)]
    n_gate = 3 * NSA_HQ
    ws.append(jnp.pad(w_in[:, edges[7]:edges[7] + n_gate], ((0, 0), (0, LANES - n_gate))).astype(BF16))
    q, kc, vc, ks, vs, kw, vw, gates = _norm_proj(x, mix_norm, ws)
    as3 = lambda y: y.reshape(b, s, y.shape[-1])
    qh = _head_prep(as3(q), q_norm, dh ** -0.5 * LOG2E, BF16)
    kch = _compress(_head_prep(as3(kc), None, 1.0, F32), cmp_pos[0], cmp_w1[0], cmp_w2[0], k_norm[0],
                    extra=_cmp_key_lanes(s // CMP_STRIDE))
    vch = _compress(_head_prep(as3(vc), None, 1.0, F32), cmp_pos[1], cmp_w1[1], cmp_w2[1], None)
    sel_tq, sel_tk, prep_ts = 256, min(512, s), min(512, s)
    ksh = _head_prep(as3(ks), k_norm[1], 1.0, BF16, extra=_sel_key_aug(sel_tk, prep_ts), ts=prep_ts)
    win_tq = min(512, s)
    vsh = _head_prep(as3(vs), None, 1.0, BF16, extra=_ones_lanes(prep_ts), ts=prep_ts)
    kwh = _head_prep(as3(kw), k_norm[2], 1.0, BF16, extra=_key_position_lanes(BAND_SUB, prep_ts), ts=prep_ts)
    vwh = _head_prep(as3(vw), None, 1.0, BF16, extra=_ones_lanes(prep_ts), ts=prep_ts)
    o_cmp, imp = _cmp_attention(qh, kch, vch, tq=128)
    sel, need = _topk_select(imp, sel_tq, sel_tk)
    need = need[:, :, :, 0, :s // sel_tk].reshape(-1)
    o_sel = _sel_attention(qh, ksh, vsh, sel, need, sel_tq, sel_tk)
    o_win = _banded_attention(qh, kwh, vwh, NSA_HQ, NSA_WINDOW, tq=win_tq)
    flat = lambda y: y.reshape(b * s, y.shape[-1])
    return _nsa_out(x, gates, flat(o_cmp), flat(o_sel), flat(o_win), w_out.astype(BF16))


def _diff_prep_body(y_ref, gq_ref, gk_ref, q_ref, k_ref, v_ref, *, width, q_scale, tk):
    tm = y_ref.shape[0]
    lane = lax.broadcasted_iota(jnp.int32, (1, LANES), 1)
    low = lane < HEAD_DIM
    c = (pl.program_id(0) * tm + lax.broadcasted_iota(jnp.int32, (tm, 1), 0)) % tk
    key_lanes = jnp.where((lane >= AUG_POS_HI) & (lane < AUG_POS_HI + 3), (c // 256) * 2,
                          jnp.where((lane >= AUG_POS_LO) & (lane < AUG_POS_LO + 3), c % 256, 0)
                          ).astype(F32).astype(BF16)

    def pair_norm(ysl, gain):
        sq = ysl * ysl
        ss_lo = jnp.sum(jnp.where(low, sq, 0.0), axis=-1, keepdims=True)
        ss_hi = jnp.sum(jnp.where(low, 0.0, sq), axis=-1, keepdims=True)
        inv = jnp.where(low, lax.rsqrt(ss_lo / HEAD_DIM + EPS), lax.rsqrt(ss_hi / HEAD_DIM + EPS))
        return ysl * inv * gain

    for j in range(width // LANES):
        sl = slice(j * LANES, (j + 1) * LANES)
        q_ref[:, sl] = (pair_norm(y_ref[:, sl], gq_ref[...]) * q_scale).astype(BF16)
        k_ref[:, 2 * j * LANES:(2 * j + 1) * LANES] = pair_norm(
            y_ref[:, width + j * LANES:width + (j + 1) * LANES], gk_ref[...]).astype(BF16)
        k_ref[:, (2 * j + 1) * LANES:(2 * j + 2) * LANES] = key_lanes
        v_ref[:, sl] = y_ref[:, 2 * width + j * LANES:2 * width + (j + 1) * LANES].astype(BF16)


def _diff_prep(qkv, q_norm, k_norm, tk, tm=512):
    t, w3 = qkv.shape
    width = w3 // 3
    pair = lambda gvec: jnp.concatenate([gvec, gvec]).reshape(1, LANES).astype(F32)
    out = jax.ShapeDtypeStruct((t, width), BF16)
    return pl.pallas_call(
        functools.partial(_diff_prep_body, width=width, q_scale=HEAD_DIM ** -0.5 * LOG2E, tk=tk),
        grid=(t // tm,),
        in_specs=[pl.BlockSpec((tm, w3), lambda i: (i, 0)),
                  pl.BlockSpec((1, LANES), lambda i: (0, 0)),
                  pl.BlockSpec((1, LANES), lambda i: (0, 0))],
        out_specs=[pl.BlockSpec((tm, width), lambda i: (i, 0)),
                   pl.BlockSpec((tm, 2 * width), lambda i: (i, 0)),
                   pl.BlockSpec((tm, width), lambda i: (i, 0))],
        out_shape=[out, jax.ShapeDtypeStruct((t, 2 * width), BF16), out],
        compiler_params=_params(("parallel",)),
        name="diff_prep",
    )(qkv, pair(q_norm), pair(k_norm))


def _diff_flash_body(reach_ref, q_ref, qa_ref, k_ref, v_ref, lam_ref, sub_ref, o_ref,
                     *, tq, tk, lam_init):
    h, i = pl.program_id(1), pl.program_id(2)
    q = q_ref[0]
    lane = lax.broadcasted_iota(jnp.int32, (1, LANES), 1)
    zero = jnp.zeros_like(q)
    q_both = jnp.concatenate([jnp.where(lane < HEAD_DIM, q, zero),
                              jnp.where(lane < HEAD_DIM, zero, q)], axis=0)
    q_aug = jnp.concatenate([q_both, qa_ref[0]], axis=-1)
    lane2 = lax.broadcasted_iota(jnp.int32, (1, 2 * LANES), 1)
    pos_lanes = (lane2 >= LANES + AUG_POS_HI) & (lane2 < LANES + AUG_POS_HI + 3)
    r = lax.broadcasted_iota(jnp.int32, (2 * tq, 1), 0) % tq
    c = lax.broadcasted_iota(jnp.int32, (1, tk), 1)

    def step(ci, carry, diagonal):
        m, l, acc = carry
        start = pl.multiple_of(ci * tk, tk)
        shift = (ci * (tk // LANES) - i * (tq // LANES)).astype(F32)
        k = k_ref[0, pl.ds(start, tk), :] + jnp.where(pos_lanes, shift, 0.0).astype(BF16)
        v = v_ref[0, pl.ds(start, tk), :]
        s = _nt(q_aug, k)
        if diagonal:
            s = jnp.where(r + (i * tq - ci * tk) >= c, s, NEG)
        m_new = jnp.maximum(m, jnp.max(s, axis=-1, keepdims=True))
        alpha = jnp.exp2(m - m_new)
        e = jnp.exp2(s - jnp.concatenate([m_new] * (tk // LANES), axis=1))
        l = alpha * l + jnp.sum(e, axis=-1, keepdims=True)
        acc = alpha * acc + jnp.dot(e.astype(BF16), v, preferred_element_type=F32)
        return m_new, l, acc

    init = (jnp.full((2 * tq, LANES), M_INIT, F32), jnp.zeros((2 * tq, LANES), F32),
            jnp.zeros((2 * tq, LANES), F32))
    diag = (i * tq) // tk
    first = jnp.maximum(i * tq - reach_ref[h], 0) // tk
    carry = lax.fori_loop(first, diag, lambda ci, cr: step(ci, cr, False), init)
    _, l, acc = step(diag, carry, True)
    lam = lam_ref[...]
    lam01 = jnp.sum(lam[0:1] * lam[1:2], axis=-1, keepdims=True)
    lam23 = jnp.sum(lam[2:3] * lam[3:4], axis=-1, keepdims=True)
    lmbda = jnp.exp(lam01) - jnp.exp(lam23) + lam_init
    o = acc / l
    o = o[:tq] - lmbda * o[tq:]
    o_ref[0] = _rms(o, sub_ref[...]) * (1.0 - lam_init)


def _diff_flash(qn, k_aug, vb, reach, lam, subln, lam_init, tq, tk):
    b, s, w = qn.shape
    n_heads = w // LANES
    sl = jnp.asarray(np.asarray(_alibi_slopes(n_heads), np.float32) * np.float32(LOG2E))
    hi = sl.astype(BF16).astype(F32)
    mid = (sl - hi).astype(BF16).astype(F32)
    lo = (sl - hi - mid).astype(BF16).astype(F32)
    pieces = jnp.stack([hi, mid, lo], axis=-1)
    q_aug = jnp.zeros((n_heads, LANES), F32)
    q_aug = q_aug.at[:, AUG_POS_HI:AUG_POS_HI + 3].set(pieces * LANES)
    q_aug = q_aug.at[:, AUG_POS_LO:AUG_POS_LO + 3].set(pieces)
    q_aug = jnp.broadcast_to(q_aug[:, None, :], (n_heads, 2 * tq, LANES)).astype(BF16)
    grid_spec = pltpu.PrefetchScalarGridSpec(
        num_scalar_prefetch=1,
        grid=(b, n_heads, s // tq),
        in_specs=[pl.BlockSpec((1, tq, LANES), lambda bi, h, i, rc: (bi, i, h)),
                  pl.BlockSpec((1, 2 * tq, LANES), lambda bi, h, i, rc: (h, 0, 0)),
                  pl.BlockSpec((1, s, 2 * LANES), lambda bi, h, i, rc: (bi, 0, h)),
                  pl.BlockSpec((1, s, LANES), lambda bi, h, i, rc: (bi, 0, h)),
                  pl.BlockSpec(lam.shape, lambda bi, h, i, rc: (0, 0)),
                  pl.BlockSpec((1, LANES), lambda bi, h, i, rc: (0, 0))],
        out_specs=pl.BlockSpec((1, tq, LANES), lambda bi, h, i, rc: (bi, i, h)))
    return pl.pallas_call(
        functools.partial(_diff_flash_body, tq=tq, tk=tk, lam_init=lam_init),
        grid_spec=grid_spec,
        out_shape=jax.ShapeDtypeStruct((b, s, w), F32),
        compiler_params=_params(("parallel", "parallel", "arbitrary")),
        name="diff_flash",
    )(reach, qn, q_aug, k_aug, vb, lam.astype(F32), subln.reshape(1, LANES).astype(F32))


def _diff_reach(q_norm, k_norm, n_heads, s):
    bound = (HEAD_DIM * jnp.max(jnp.abs(q_norm)) * jnp.max(jnp.abs(k_norm))
             * (HEAD_DIM ** -0.5 * LOG2E) * 1.02)
    slopes = jnp.asarray(np.asarray(_alibi_slopes(n_heads), np.float32) * np.float32(LOG2E))
    reach = jnp.ceil((2.0 * bound + 152.0) / slopes) + 1.0
    return jnp.minimum(reach, float(s)).astype(jnp.int32)


def _diff_layer(x, b, s, mix_norm, w_in, w_out, q_norm, k_norm, lam, subln, lam_init):
    tq, tk = min(512, s), min(1024, s)
    (qkv,) = _norm_proj(x, mix_norm, [w_in.astype(BF16)])
    qn, k_aug, vb = _diff_prep(qkv, q_norm, k_norm, tk)
    as3 = lambda y: y.reshape(b, s, y.shape[-1])
    o = _diff_flash(as3(qn), as3(k_aug), as3(vb), _diff_reach(q_norm, k_norm, DIFF_H, s),
                    lam, subln, lam_init, tq, tk)
    return _proj_residual(x, o.reshape(b * s, -1), w_out.astype(BF16))


def _gdn_prep_body(cur_ref, prev_ref, cw_ref, b_ref, a_ref, alog_ref, dt_ref,
                   q_ref, k_ref, v_ref, beta_ref, gc_ref, pad_ref, *, ts, width):
    i = pl.program_id(1)
    halo = prev_ref.shape[1]
    pad_ref[0:halo, :] = jnp.where(i > 0, prev_ref[0], 0.0)
    pad_ref[halo:, :] = cur_ref[0]
    y = jnp.zeros((ts, 3 * width), F32)
    for j in range(GDN_CONV):
        off = halo - (GDN_CONV - 1) + j
        y = y + cw_ref[j:j + 1, :] * pad_ref[off:off + ts, :]
    y = y * _sigmoid(y)
    for h in range(width // GDN_DK):
        sl = slice(h * GDN_DK, (h + 1) * GDN_DK)
        qh = y[:, sl]
        q_ref[0, :, sl] = qh * lax.rsqrt(jnp.sum(qh * qh, axis=-1, keepdims=True) + EPS) * (GDN_DK ** -0.5)
        kh = y[:, width + h * GDN_DK:width + (h + 1) * GDN_DK]
        k_ref[0, :, sl] = kh * lax.rsqrt(jnp.sum(kh * kh, axis=-1, keepdims=True) + EPS)
    v_ref[0] = y[:, 2 * width:]
    beta_ref[0] = _sigmoid(b_ref[0])
    z = a_ref[0] + dt_ref[...]
    softplus = jnp.maximum(z, 0.0) + jnp.log(1.0 + jnp.exp(-jnp.abs(z)))
    gl = -jnp.exp(alog_ref[...]) * softplus
    r = lax.broadcasted_iota(jnp.int32, (ts, ts), 0)
    c = lax.broadcasted_iota(jnp.int32, (ts, ts), 1)
    tri = jnp.where((r // GDN_CHUNK == c // GDN_CHUNK) & (c <= r), 1.0, 0.0)
    gc_ref[0] = jnp.dot(tri, gl, precision=HIGHEST, preferred_element_type=F32)


def _gdn_prep(qkv, b_pad, a_pad, conv_w, a_log, dt_bias, ts=256):
    b, s, w3 = qkv.shape
    width = w3 // 3
    halo = 8
    pad_lane = lambda vec: jnp.pad(vec.astype(F32), (0, LANES - vec.shape[0])).reshape(1, LANES)
    big = jax.ShapeDtypeStruct((b, s, width), F32)
    small = jax.ShapeDtypeStruct((b, s, LANES), F32)
    tok = lambda wd: pl.BlockSpec((1, ts, wd), lambda bi, i: (bi, i, 0))
    return pl.pallas_call(
        functools.partial(_gdn_prep_body, ts=ts, width=width),
        grid=(b, s // ts),
        in_specs=[tok(w3),
                  pl.BlockSpec((1, halo, w3), lambda bi, i: (bi, jnp.maximum(i * (ts // halo) - 1, 0), 0)),
                  pl.BlockSpec((GDN_CONV, w3), lambda bi, i: (0, 0)),
                  tok(LANES), tok(LANES),
                  pl.BlockSpec((1, LANES), lambda bi, i: (0, 0)),
                  pl.BlockSpec((1, LANES), lambda bi, i: (0, 0))],
        out_specs=[tok(width), tok(width), tok(width), tok(LANES), tok(LANES)],
        out_shape=[big, big, big, small, small],
        scratch_shapes=[pltpu.VMEM((ts + halo, w3), F32)],
        compiler_params=_params(("parallel", "parallel")),
        name="gdn_prep",
    )(qkv, qkv, conv_w.astype(F32), b_pad, a_pad, pad_lane(a_log), pad_lane(dt_bias))


def _unit_lower_inverses(mats):
    n = mats[0].shape[0]
    r = lax.broadcasted_iota(jnp.int32, (n, n), 0)
    c = lax.broadcasted_iota(jnp.int32, (n, n), 1)
    eye = jnp.where(r == c, 1.0, 0.0)
    d = 1
    inv = None
    while d < n:
        below = (r // (2 * d) == c // (2 * d)) & (r // d != c // d)
        offs = [jnp.where(below, a, 0.0) for a in mats]
        if d == 1:
            inv = [eye - off for off in offs]
        else:
            left = [_dot3(t, off) for t, off in zip(inv, offs)]
            inv = [t - _dot3(lt, t) for t, lt in zip(inv, left)]
        d *= 2
    return inv


def _gdn_chunk_body(q_ref, k_ref, v_ref, beta_ref, gc_ref, gr_ref, gate_ref, on_ref, o_ref,
                    state_ref, *, nc, n_heads):
    @pl.when(pl.program_id(1) == 0)
    def _():
        state_ref[...] = jnp.zeros(state_ref.shape, F32)

    cl = GDN_CHUNK
    r = lax.broadcasted_iota(jnp.int32, (cl, cl), 0)
    c = lax.broadcasted_iota(jnp.int32, (cl, cl), 1)
    fdot = functools.partial(jnp.dot, preferred_element_type=F32)
    blocks = [(ch, h) for ch in range(nc) for h in range(n_heads)]
    rows = lambda ch: slice(ch * cl, (ch + 1) * cl)
    lanes = lambda h: slice(h * GDN_DK, (h + 1) * GDN_DK)

    qs = [q_ref[0, rows(ch), lanes(h)] for ch, h in blocks]
    ks = [k_ref[0, rows(ch), lanes(h)] for ch, h in blocks]
    bcol = [beta_ref[0, rows(ch), h:h + 1] for ch, h in blocks]
    gcol = [gc_ref[0, rows(ch), h:h + 1] for ch, h in blocks]
    glast = [gc_ref[0, ch * cl + cl - 1:ch * cl + cl, h:h + 1] for ch, h in blocks]
    decay = [jnp.exp(jnp.minimum(g - gr_ref[0, ch, h:h + 1, :], 0.0))
             for g, (ch, h) in zip(gcol, blocks)]
    kb = [k * b for k, b in zip(ks, bcol)]
    amat = [jnp.where(r > c, _dot3(x, k, ((1,), (1,))) * dc, 0.0) for x, k, dc in zip(kb, ks, decay)]
    inv = _unit_lower_inverses(amat)
    eg = [jnp.exp(g) for g in gcol]
    u = [_dot3(t, v_ref[0, rows(ch), lanes(h)] * b) for t, b, (ch, h) in zip(inv, bcol, blocks)]
    w = [_dot3(t, x * e) for t, x, e in zip(inv, kb, eg)]
    attn = [jnp.where(r >= c, _nt(q, k) * dc, 0.0) for q, k, dc in zip(qs, ks, decay)]

    for ch in range(nc):
        ids = range(ch * n_heads, (ch + 1) * n_heads)
        state = [state_ref[h] for h in range(n_heads)]
        v_new = [u[i] - fdot(w[i], st) for i, st in zip(ids, state)]
        o = [fdot(qs[i] * eg[i], st) + fdot(attn[i], vn) for i, st, vn in zip(ids, state, v_new)]
        for h, (i, st, vn) in enumerate(zip(ids, state, v_new)):
            state_ref[h] = st * jnp.exp(glast[i]) + _tn(ks[i] * jnp.exp(glast[i] - gcol[i]), vn)
        for h in range(n_heads):
            gate = gate_ref[0, rows(ch), lanes(h)]
            o_ref[0, rows(ch), lanes(h)] = _rms(o[h], on_ref[...]) * (gate * _sigmoid(gate))


def _gdn_chunks(q, k, v, beta, gc, gc_rows, gate, o_norm, nc=1):
    b, s, w = q.shape
    n_heads = w // GDN_DK
    ts = nc * GDN_CHUNK
    tok = lambda wd: pl.BlockSpec((1, ts, wd), lambda bi, i: (bi, i, 0))
    return pl.pallas_call(
        functools.partial(_gdn_chunk_body, nc=nc, n_heads=n_heads),
        grid=(b, s // ts),
        in_specs=[tok(w), tok(w), tok(w), tok(LANES), tok(LANES),
                  pl.BlockSpec((1, nc, 8, GDN_CHUNK), lambda bi, i: (bi, i, 0, 0)),
                  tok(w),
                  pl.BlockSpec((1, GDN_DV), lambda bi, i: (0, 0))],
        out_specs=tok(w),
        out_shape=jax.ShapeDtypeStruct((b, s, w), F32),
        scratch_shapes=[pltpu.VMEM((n_heads, GDN_DK, GDN_DV), F32)],
        compiler_params=_params(("parallel", "arbitrary")),
        name="gdn_chunks",
    )(q, k, v, beta, gc, gc_rows, gate, o_norm.reshape(1, GDN_DV).astype(F32))


def _gdn_layer(x, b, s, mix_norm, w_in, w_out, conv_w, a_log, dt_bias, o_norm):
    qkv_w = GDN_H * (2 * GDN_DK + GDN_DV)
    lane_pad = lambda wcols: jnp.pad(wcols, ((0, 0), (0, LANES - wcols.shape[1]))).astype(BF16)
    ws = [w_in[:, :qkv_w].astype(BF16),
          lane_pad(w_in[:, qkv_w:qkv_w + GDN_H]),
          lane_pad(w_in[:, qkv_w + GDN_H:qkv_w + 2 * GDN_H]),
          w_in[:, qkv_w + 2 * GDN_H:].astype(BF16)]
    qkv, b_pad, a_pad, gate = _norm_proj(x, mix_norm, ws)
    as3 = lambda y: y.reshape(b, s, y.shape[-1])
    q, k, v, beta, gc = _gdn_prep(as3(qkv), as3(b_pad), as3(a_pad), conv_w, a_log, dt_bias,
                                  ts=min(256, s))
    gc_rows = gc[:, :, :8].reshape(b, s // GDN_CHUNK, GDN_CHUNK, 8).transpose(0, 1, 3, 2)
    o = _gdn_chunks(q, k, v, beta, gc, gc_rows, as3(gate), o_norm)
    return _proj_residual(x, o.reshape(b * s, -1), w_out.astype(BF16))


def _swa_layer(x, b, s, mix_norm, w_in, w_out, q_norm, k_norm, sinks):
    dh = HEAD_DIM
    e0, e1 = SWA_HQ * dh, (SWA_HQ + SWA_HKV) * dh
    ws = [w_in[:, :e0].astype(BF16), w_in[:, e0:e1].astype(BF16), w_in[:, e1:].astype(BF16)]
    q, k, v = _norm_proj(x, mix_norm, ws)
    as3 = lambda y: y.reshape(b, s, y.shape[-1])
    qh = _head_prep(as3(q), q_norm, dh ** -0.5 * LOG2E, BF16)
    tq, prep_ts = min(512, s), min(512, s)
    kh = _head_prep(as3(k), k_norm, 1.0, BF16, extra=_key_position_lanes(BAND_SUB, prep_ts), ts=prep_ts)
    vh = _head_prep(as3(v), None, 1.0, BF16, extra=_ones_lanes(prep_ts), ts=prep_ts)
    o = _banded_attention(qh, kh, vh, SWA_HQ, SWA_WINDOW, tq=tq, sinks=sinks)
    return _proj_residual(x, o.reshape(b * s, -1), w_out.astype(BF16))


def kernel(x, ffn1_norm, ffn1_w_in, ffn1_w_out, mix_norm, ffn2_norm, ffn2_w_in, ffn2_w_out,
           nsa_w_in, nsa_w_out, nsa_q_norm, nsa_k_norm, nsa_cmp_pos, nsa_cmp_w1, nsa_cmp_w2,
           diff_w_in, diff_w_out, diff_q_norm, diff_k_norm, diff_lambda, diff_subln,
           gdn_w_in, gdn_w_out, gdn_conv_w, gdn_a_log, gdn_dt_bias, gdn_o_norm,
           swa_w_in, swa_w_out, swa_q_norm, swa_k_norm, swa_sinks):
    b, s, d = x.shape
    depth = ffn1_norm.shape[0]
    n_mixers = 4
    h = x.reshape(b * s, d)
    for layer in range(depth):
        kind, j = layer % n_mixers, layer // n_mixers
        h = _ffn(h, ffn1_norm[layer], ffn1_w_in[layer].astype(BF16), ffn1_w_out[layer].astype(BF16))
        if kind == 0:
            h = _nsa_layer(h, b, s, mix_norm[layer], nsa_w_in[j], nsa_w_out[j], nsa_q_norm[j],
                           nsa_k_norm[j], nsa_cmp_pos[j], nsa_cmp_w1[j], nsa_cmp_w2[j])
        elif kind == 1:
            lam_init = 0.8 - 0.6 * math.exp(-0.3 * layer)
            h = _diff_layer(h, b, s, mix_norm[layer], diff_w_in[j], diff_w_out[j], diff_q_norm[j],
                            diff_k_norm[j], diff_lambda[j], diff_subln[j], lam_init)
        elif kind == 2:
            h = _gdn_layer(h, b, s, mix_norm[layer], gdn_w_in[j], gdn_w_out[j], gdn_conv_w[j],
                           gdn_a_log[j], gdn_dt_bias[j], gdn_o_norm[j])
        else:
            h = _swa_layer(h, b, s, mix_norm[layer], swa_w_in[j], swa_w_out[j], swa_q_norm[j],
                           swa_k_norm[j], swa_sinks[j])
        h = _ffn(h, ffn2_norm[layer], ffn2_w_in[layer].astype(BF16), ffn2_w_out[layer].astype(BF16))
    return h.reshape(b, s, d)
```

```python
import functools
import math

import numpy as np
import jax
import jax.numpy as jnp
from jax import lax
from jax.experimental import pallas as pl
from jax.experimental.pallas import tpu as pltpu

F32 = jnp.float32
BF16 = jnp.bfloat16
HIGHEST = lax.Precision.HIGHEST

EPS = 1e-6
NEG = -1e30
M_INIT = -1e29
LOG2E = 1.4426950408889634
FORCE = 1e9

HEAD_DIM = 64
LANES = 128
VMEM_LIMIT = 56 * 1024 * 1024

NSA_HQ, NSA_HKV = 16, 4
CMP_BLOCK, CMP_STRIDE = 32, 16
SEL_BLOCK, SEL_TOPK = 64, 16
NSA_WINDOW = 512
DIFF_H = 8
GDN_H, GDN_DK, GDN_DV, GDN_CONV, GDN_CHUNK = 8, 128, 128, 4, 64
SWA_HQ, SWA_HKV, SWA_WINDOW = 16, 4, 128


def _params(sem):
    return pltpu.CompilerParams(dimension_semantics=sem, vmem_limit_bytes=VMEM_LIMIT)


def _nt(a, b, precision=None):
    return lax.dot_general(a, b, (((1,), (1,)), ((), ())), precision=precision,
                           preferred_element_type=F32)


def _tn(a, b, precision=None):
    return lax.dot_general(a, b, (((0,), (0,)), ((), ())), precision=precision,
                           preferred_element_type=F32)


def _dot3(a, b, contract=((1,), (0,))):
    a_hi, b_hi = a.astype(BF16), b.astype(BF16)
    a_lo = (a - a_hi.astype(F32)).astype(BF16)
    b_lo = (b - b_hi.astype(F32)).astype(BF16)
    dot = lambda x, y: lax.dot_general(x, y, (contract, ((), ())), preferred_element_type=F32)
    return dot(a_hi, b_hi) + dot(a_hi, b_lo) + dot(a_lo, b_hi)


def _rms(x, g):
    return x * lax.rsqrt(jnp.mean(x * x, axis=-1, keepdims=True) + EPS) * g


def _sigmoid(x):
    return 1.0 / (1.0 + jnp.exp(-x))


def _alibi_slopes(n):
    return [2.0 ** (-8.0 * (i + 1) / n) for i in range(n)]


def _ffn_body(x_ref, g_ref, win_ref, wout_ref, o_ref, *, d_ff, chunk):
    x = x_ref[...]
    xn = _rms(x, g_ref[...]).astype(BF16)
    acc = jnp.zeros(x.shape, F32)
    for c in range(d_ff // chunk):
        a = jnp.dot(xn, win_ref[:, c * chunk:(c + 1) * chunk], preferred_element_type=F32)
        b = jnp.dot(xn, win_ref[:, d_ff + c * chunk:d_ff + (c + 1) * chunk],
                    preferred_element_type=F32)
        h = (a * _sigmoid(a) * b).astype(BF16)
        acc = acc + jnp.dot(h, wout_ref[c * chunk:(c + 1) * chunk, :], preferred_element_type=F32)
    o_ref[...] = x + 0.5 * acc


def _ffn(x, g, w_in, w_out, tm=512, chunk=256):
    t, d = x.shape
    d_ff = w_out.shape[0]
    return pl.pallas_call(
        functools.partial(_ffn_body, d_ff=d_ff, chunk=chunk),
        grid=(t // tm,),
        in_specs=[pl.BlockSpec((tm, d), lambda i: (i, 0)),
                  pl.BlockSpec((1, d), lambda i: (0, 0)),
                  pl.BlockSpec(w_in.shape, lambda i: (0, 0), pipeline_mode=pl.Buffered(1)),
                  pl.BlockSpec(w_out.shape, lambda i: (0, 0), pipeline_mode=pl.Buffered(1))],
        out_specs=pl.BlockSpec((tm, d), lambda i: (i, 0)),
        out_shape=jax.ShapeDtypeStruct((t, d), F32),
        compiler_params=_params(("parallel",)),
        name="ffn",
    )(x, g.reshape(1, d), w_in, w_out)


def _norm_proj_body(x_ref, g_ref, *refs, n):
    xn = _rms(x_ref[...], g_ref[...]).astype(BF16)
    for w_ref, o_ref in zip(refs[:n], refs[n:]):
        o_ref[...] = jnp.dot(xn, w_ref[...], preferred_element_type=F32).astype(o_ref.dtype)


def _norm_proj(x, g, ws, tm=512):
    t, d = x.shape
    n = len(ws)
    return pl.pallas_call(
        functools.partial(_norm_proj_body, n=n),
        grid=(t // tm,),
        in_specs=[pl.BlockSpec((tm, d), lambda i: (i, 0)),
                  pl.BlockSpec((1, d), lambda i: (0, 0))]
                 + [pl.BlockSpec(w.shape, lambda i: (0, 0)) for w in ws],
        out_specs=[pl.BlockSpec((tm, w.shape[1]), lambda i: (i, 0)) for w in ws],
        out_shape=[jax.ShapeDtypeStruct((t, w.shape[1]), F32) for w in ws],
        compiler_params=_params(("parallel",)),
        name="norm_proj",
    )(x, g.reshape(1, d), *ws)


def _proj_res_body(x_ref, a_ref, w_ref, o_ref):
    o_ref[...] = x_ref[...] + jnp.dot(a_ref[...].astype(BF16), w_ref[...],
                                      preferred_element_type=F32)


def _proj_residual(x, a, w, tm=512):
    t, d = x.shape
    k = a.shape[1]
    return pl.pallas_call(
        _proj_res_body,
        grid=(t // tm,),
        in_specs=[pl.BlockSpec((tm, d), lambda i: (i, 0)),
                  pl.BlockSpec((tm, k), lambda i: (i, 0)),
                  pl.BlockSpec((k, d), lambda i: (0, 0))],
        out_specs=pl.BlockSpec((tm, d), lambda i: (i, 0)),
        out_shape=jax.ShapeDtypeStruct((t, d), F32),
        compiler_params=_params(("parallel",)),
        name="proj_residual",
    )(x, a, w)


def _head_prep_body(y_ref, g_ref, *refs, nh, norm, scale, has_extra):
    o_ref = refs[-1]
    for h in range(nh):
        yh = y_ref[0, :, h * HEAD_DIM:(h + 1) * HEAD_DIM]
        if norm:
            yh = _rms(yh, g_ref[...])
        if scale != 1.0:
            yh = yh * scale
        yh = yh.astype(o_ref.dtype)
        if has_extra:
            yh = jnp.concatenate([yh, refs[0][...]], axis=-1)
        o_ref[0, h] = yh


def _head_prep(y, gain, scale, out_dtype, extra=None, ts=512):
    b, s, c = y.shape
    nh = c // HEAD_DIM
    norm = gain is not None
    g = (gain if norm else jnp.ones((HEAD_DIM,), F32)).reshape(1, HEAD_DIM)
    width = HEAD_DIM if extra is None else 2 * HEAD_DIM
    in_specs = [pl.BlockSpec((1, ts, c), lambda bi, i: (bi, i, 0)),
                pl.BlockSpec((1, HEAD_DIM), lambda bi, i: (0, 0))]
    args = [y, g]
    if extra is not None:
        in_specs.append(pl.BlockSpec((ts, HEAD_DIM), lambda bi, i: (0, 0)))
        args.append(extra)
    return pl.pallas_call(
        functools.partial(_head_prep_body, nh=nh, norm=norm, scale=scale,
                          has_extra=extra is not None),
        grid=(b, s // ts),
        in_specs=in_specs,
        out_specs=pl.BlockSpec((1, nh, ts, width), lambda bi, i: (bi, 0, i, 0)),
        out_shape=jax.ShapeDtypeStruct((b, nh, s, width), out_dtype),
        compiler_params=_params(("parallel", "parallel")),
        name="head_prep",
    )(*args)


def _slope_rows(n_heads, n_kv, rows_per_head, mult=LOG2E):
    g = n_heads // n_kv
    sl = np.asarray(_alibi_slopes(n_heads), np.float32).reshape(n_kv, g, 1) * np.float32(mult)
    return jnp.asarray(np.repeat(sl, rows_per_head, axis=1).reshape(n_kv, g * rows_per_head, 1))


BAND_SUB = 128


def _banded_body(*refs, g, tq, window, has_sink):
    q_ref, qa_ref, kp_ref, kc_ref, vp_ref, vc_ref, mask_ref = refs[:7]
    o_ref = refs[-1]
    i = pl.program_id(2)
    sub = BAND_SUB
    rows = g * sub
    span = window + sub
    lane = lax.broadcasted_iota(jnp.int32, (1, LANES), 1)
    pos_lanes = (lane >= HEAD_DIM + AUG_POS_HI) & (lane < HEAD_DIM + AUG_POS_HI + 3)
    shifts = jnp.concatenate(
        [jnp.broadcast_to(jnp.where(pos_lanes, float(bk - window // sub), 0.0), (sub, LANES))
         for bk in range(span // sub)], axis=0).astype(BF16)
    k_win = jnp.concatenate([kp_ref[0, 0], kc_ref[0, 0]], axis=0)
    v_win = jnp.concatenate([vp_ref[0, 0], vc_ref[0, 0]], axis=0)
    col = lax.broadcasted_iota(jnp.int32, (1, span), 1)
    before_start = jnp.where(i > 0, 0.0, NEG)
    if has_sink:
        sl_ref, sink_ref = refs[7], refs[8]
        r = (lax.broadcasted_iota(jnp.int32, (rows, LANES), 0) % sub).astype(F32)
        sink = sink_ref[0] * LOG2E + sl_ref[0] * r
    units = range(tq // sub)
    q_aug = [jnp.concatenate([q_ref[0, 0, :, u * sub:(u + 1) * sub, :].reshape(rows, HEAD_DIM),
                              qa_ref[0]], axis=-1) for u in units]
    bias = [jnp.concatenate([mask_ref[...] + jnp.where(col < window - u * sub, before_start, 0.0)] * g,
                            axis=0) for u in units]
    s = [_nt(q_aug[u], k_win[u * sub:u * sub + span] + shifts) + bias[u] for u in units]
    m = [jnp.broadcast_to(jnp.max(su, axis=-1, keepdims=True), (rows, LANES)) for su in s]
    if has_sink:
        m = [jnp.maximum(mu, sink) for mu in m]
    e = [jnp.exp2(su - jnp.concatenate([mu] * (span // LANES), axis=1)).astype(BF16)
         for su, mu in zip(s, m)]
    acc = [jnp.dot(e[u], v_win[u * sub:u * sub + span], preferred_element_type=F32) for u in units]
    for u in units:
        if has_sink:
            acc[u] = acc[u] + jnp.where(lane >= HEAD_DIM, jnp.exp2(sink - m[u]), 0.0)
        o = (acc[u] / pltpu.roll(acc[u], HEAD_DIM, 1))[:, :HEAD_DIM]
        o_ref[0, u * sub:(u + 1) * sub, :] = jnp.concatenate(
            [o[gi * sub:(gi + 1) * sub] for gi in range(g)], axis=-1)


def _key_position_lanes(period, rows):
    assert rows % period == 0
    c = np.arange(rows) % period
    cols = np.zeros((rows, HEAD_DIM), np.float32)
    cols[:, AUG_POS_HI:AUG_POS_HI + 3] = (c // 256 * 2)[:, None]
    cols[:, AUG_POS_LO:AUG_POS_LO + 3] = (c % 256)[:, None]
    return jnp.asarray(cols, BF16)


def _ones_lanes(rows):
    return jnp.ones((rows, HEAD_DIM), BF16)


def _banded_attention(q, k_aug, v_aug, n_heads, window, tq, sinks=None):
    b, hq, s, _ = q.shape
    hkv = k_aug.shape[1]
    g = hq // hkv
    sub = BAND_SUB
    assert tq % window == 0 and window % sub == 0
    span = window + sub
    dist = np.arange(sub)[:, None] + window - np.arange(span)[None, :]
    mask = np.where((dist >= 0) & (dist < window), 0.0, NEG).astype(np.float32)
    per = tq // window
    prev_spec = pl.BlockSpec((1, 1, window, LANES), lambda bi, h, i: (bi, h, jnp.maximum(i * per - 1, 0), 0))
    cur_spec = pl.BlockSpec((1, 1, tq, LANES), lambda bi, h, i: (bi, h, i, 0))
    in_specs = [pl.BlockSpec((1, 1, g, tq, HEAD_DIM), lambda bi, h, i: (bi, h, 0, i, 0)),
                pl.BlockSpec((1, g * sub, HEAD_DIM), lambda bi, h, i: (h, 0, 0)),
                prev_spec, cur_spec, prev_spec, cur_spec,
                pl.BlockSpec((sub, span), lambda bi, h, i: (0, 0))]
    args = [q.reshape(b, hkv, g, s, HEAD_DIM), _sel_query_aug(n_heads, hkv, sub),
            k_aug, k_aug, v_aug, v_aug, jnp.asarray(mask)]
    if sinks is not None:
        in_specs += [pl.BlockSpec((1, g * sub, 1), lambda bi, h, i: (h, 0, 0))] * 2
        args += [_slope_rows(n_heads, hkv, sub),
                 jnp.repeat(sinks.astype(F32).reshape(hkv, g, 1), sub, axis=1).reshape(hkv, g * sub, 1)]
    return pl.pallas_call(
        functools.partial(_banded_body, g=g, tq=tq, window=window, has_sink=sinks is not None),
        grid=(b, hkv, s // tq),
        in_specs=in_specs,
        out_specs=pl.BlockSpec((1, tq, g * HEAD_DIM), lambda bi, h, i: (bi, i, h)),
        out_shape=jax.ShapeDtypeStruct((b, s, hq * HEAD_DIM), F32),
        compiler_params=_params(("parallel", "parallel", "parallel")),
        name="banded_attention",
    )(*args)


def _compress_body(t_ref, pt_ref, pb_ref, w1t_ref, w1b_ref, w2_ref, g_ref, *refs, norm):
    o_ref = refs[-1]
    sub = t_ref[0, 0]
    n_sub = sub.shape[0]
    a = jnp.dot((sub + pt_ref[...]).astype(BF16), w1t_ref[...], preferred_element_type=F32)
    bm = jnp.dot((sub + pb_ref[...]).astype(BF16), w1b_ref[...], preferred_element_type=F32)
    hid = a + pltpu.roll(bm, shift=n_sub - 1, axis=0)
    hid = 0.5 * hid * (1.0 + jnp.tanh(0.7978845608028654 * (hid + 0.044715 * hid * hid * hid)))
    out = jnp.dot(hid.astype(BF16), w2_ref[...], preferred_element_type=F32)
    if norm:
        out = _rms(out, g_ref[...])
    out = out.astype(BF16)
    if len(refs) == 2:
        out = jnp.concatenate([out, refs[0][...]], axis=-1)
    o_ref[0, 0] = out


def _compress(t, pos, w1, w2, gain, extra=None):
    b, h, s, _ = t.shape
    n_sub = s // CMP_STRIDE
    half = CMP_STRIDE * HEAD_DIM
    hidden = w1.shape[1]
    norm = gain is not None
    g = (gain if norm else jnp.ones((HEAD_DIM,), F32)).reshape(1, HEAD_DIM)
    const = lambda shape: pl.BlockSpec(shape, lambda bi, hi: (0,) * len(shape))
    width = HEAD_DIM if extra is None else 2 * HEAD_DIM
    extras = [] if extra is None else [extra]
    return pl.pallas_call(
        functools.partial(_compress_body, norm=norm),
        grid=(b, h),
        in_specs=[pl.BlockSpec((1, 1, n_sub, half), lambda bi, hi: (bi, hi, 0, 0)),
                  const((1, half)), const((1, half)),
                  const((half, hidden)), const((half, hidden)),
                  const((hidden, HEAD_DIM)), const((1, HEAD_DIM))]
                 + [const((n_sub, HEAD_DIM)) for _ in extras],
        out_specs=pl.BlockSpec((1, 1, n_sub, width), lambda bi, hi: (bi, hi, 0, 0)),
        out_shape=jax.ShapeDtypeStruct((b, h, n_sub, width), BF16),
        compiler_params=_params(("parallel", "parallel")),
        name="nsa_compress",
    )(t.reshape(b, h, n_sub, half),
      pos[:CMP_STRIDE].reshape(1, half), pos[CMP_STRIDE:].reshape(1, half),
      w1[:half].astype(BF16), w1[half:].astype(BF16), w2.astype(BF16), g, *extras)


def _cmp_key_lanes(n_sub):
    n = np.arange(n_sub)
    cols = np.zeros((n_sub, HEAD_DIM), np.float32)
    cols[:, AUG_POS_HI:AUG_POS_HI + 3] = (n // 8)[:, None]
    cols[:, AUG_POS_LO:AUG_POS_LO + 3] = (CMP_STRIDE * (n % 8) + CMP_BLOCK - 1)[:, None]
    return jnp.asarray(cols, BF16)


def _cmp_attn_body(q_ref, qa_ref, kc_ref, vc_ref, ov_ref, o_ref, imp_ref, *, g, tq, widths):
    i = pl.program_id(2)
    rows = g * tq
    q_aug = jnp.concatenate([q_ref[0, 0].reshape(rows, HEAD_DIM), qa_ref[0]], axis=-1)
    lane = lax.broadcasted_iota(jnp.int32, (1, LANES), 1)
    pos_lanes = (lane >= HEAD_DIM + AUG_POS_HI) & (lane < HEAD_DIM + AUG_POS_HI + 3)
    shift = jnp.where(pos_lanes, (-i * (tq // LANES)).astype(F32), 0.0).astype(BF16)
    t_row = i * tq + lax.broadcasted_iota(jnp.int32, (rows, 1), 0) % tq
    n_visible = (i * tq + tq) // CMP_STRIDE - 1

    def attend(n_cols):
        s = _nt(q_aug, kc_ref[0, 0, :n_cols, :] + shift)
        cmp_end = lax.broadcasted_iota(jnp.int32, (1, n_cols), 1) * CMP_STRIDE + (CMP_BLOCK - 1)
        s = jnp.where(t_row >= cmp_end, s, NEG)
        wide = lambda x: jnp.concatenate([jnp.broadcast_to(x, (rows, LANES))] * (n_cols // LANES), axis=1)
        e = jnp.exp2(s - wide(jnp.maximum(jnp.max(s, axis=-1, keepdims=True), M_INIT)))
        l = jnp.sum(e, axis=-1, keepdims=True)
        p = e * wide(1.0 / jnp.where(l > 0.0, l, 1.0))
        o = jnp.dot(p.astype(BF16), vc_ref[0, 0, :n_cols, :], preferred_element_type=F32)
        o_ref[0] = jnp.concatenate([o[gi * tq:(gi + 1) * tq] for gi in range(g)], axis=-1)
        psum = p[0:tq]
        for gi in range(1, g):
            psum = psum + p[gi * tq:(gi + 1) * tq]
        p_hi = psum.astype(BF16)
        r1 = psum - p_hi.astype(F32)
        p_mid = r1.astype(BF16)
        p_lo = (r1 - p_mid.astype(F32)).astype(BF16)
        ov = ov_ref[:n_cols, :]
        imp_ref[0, 0] = (jnp.dot(p_hi, ov, preferred_element_type=F32)
                         + jnp.dot(p_mid, ov, preferred_element_type=F32)
                         + jnp.dot(p_lo, ov, preferred_element_type=F32))

    for lo, hi in zip((0,) + widths[:-1], widths):
        pl.when((n_visible > lo) & (n_visible <= hi) if lo else n_visible <= hi)(
            functools.partial(attend, hi))


def _cmp_attention(q, kc_aug, vc, tq):
    b, hq, s, _ = q.shape
    hkv = kc_aug.shape[1]
    g = hq // hkv
    n_sub = kc_aug.shape[2]
    n_sel = s // SEL_BLOCK
    step = 2 * LANES
    widths = tuple(range(step, n_sub, step)) + (n_sub,) if n_sub >= 2 * step else (n_sub,)
    cs = np.arange(n_sub)[:, None] * CMP_STRIDE
    ss = np.arange(n_sel)[None, :] * SEL_BLOCK
    overlap = ((cs < ss + SEL_BLOCK) & (cs + CMP_BLOCK > ss)).astype(np.float32)
    return pl.pallas_call(
        functools.partial(_cmp_attn_body, g=g, tq=tq, widths=widths),
        grid=(b, hkv, s // tq),
        in_specs=[pl.BlockSpec((1, 1, g, tq, HEAD_DIM), lambda bi, h, i: (bi, h, 0, i, 0)),
                  pl.BlockSpec((1, g * tq, HEAD_DIM), lambda bi, h, i: (h, 0, 0)),
                  pl.BlockSpec((1, 1, n_sub, LANES), lambda bi, h, i: (bi, h, 0, 0)),
                  pl.BlockSpec((1, 1, n_sub, HEAD_DIM), lambda bi, h, i: (bi, h, 0, 0)),
                  pl.BlockSpec((n_sub, n_sel), lambda bi, h, i: (0, 0))],
        out_specs=[pl.BlockSpec((1, tq, g * HEAD_DIM), lambda bi, h, i: (bi, i, h)),
                   pl.BlockSpec((1, 1, tq, n_sel), lambda bi, h, i: (bi, h, i, 0))],
        out_shape=[jax.ShapeDtypeStruct((b, s, hq * HEAD_DIM), F32),
                   jax.ShapeDtypeStruct((b, hkv, s, n_sel), F32)],
        compiler_params=_params(("parallel", "parallel", "parallel")),
        name="nsa_cmp_attention",
    )(q.reshape(b, hkv, g, s, HEAD_DIM), _sel_query_aug(hq, hkv, tq), kc_aug, vc,
      jnp.asarray(overlap, BF16))


def _topk_body(imp_ref, ch_ref, sel_ref, need_ref, *, tq, n_top, sub):
    i = pl.program_id(2)
    n_sel = imp_ref.shape[3]
    j = lax.broadcasted_iota(jnp.int32, (n_sel, 1), 0)
    cur = (i * tq + lax.broadcasted_iota(jnp.int32, (1, tq), 1)) // SEL_BLOCK
    forced = (j == 0) | (j == cur) | (j == cur - 1)
    causal = j <= cur
    imp = jnp.where(forced, FORCE, jnp.where(causal, imp_ref[0, 0].T, -1.0))
    for _ in range(n_top):
        mx = jnp.max(imp, axis=0, keepdims=True)
        first = jnp.min(jnp.where(imp == mx, j, n_sel), axis=0, keepdims=True)
        imp = jnp.where(j == first, -2.0, imp)
    sel = jnp.where(causal & (imp == -2.0), 1.0, 0.0).T
    sel_ref[0, 0] = sel.astype(sel_ref.dtype)
    for u in range(tq // sub):
        any_row = jnp.max(sel[u * sub:(u + 1) * sub], axis=0, keepdims=True)
        hits = jnp.dot(jnp.broadcast_to(any_row, (8, n_sel)).astype(BF16), ch_ref[...],
                       preferred_element_type=F32)
        need_ref[0, 0, u] = jnp.where(hits > 0.5, 1, 0).astype(jnp.int32)


def _topk_select(imp, sub, sel_tk, tq=512):
    b, hkv, s, n_sel = imp.shape
    tq = min(tq, s)
    n_top = min(SEL_TOPK, n_sel)
    assert s // sel_tk <= LANES
    chunk_of = (np.arange(n_sel)[:, None] * SEL_BLOCK // sel_tk == np.arange(LANES)[None, :])
    return pl.pallas_call(
        functools.partial(_topk_body, tq=tq, n_top=n_top, sub=sub),
        grid=(b, hkv, s // tq),
        in_specs=[pl.BlockSpec((1, 1, tq, n_sel), lambda bi, h, i: (bi, h, i, 0)),
                  pl.BlockSpec((n_sel, LANES), lambda bi, h, i: (0, 0))],
        out_specs=[pl.BlockSpec((1, 1, tq, n_sel), lambda bi, h, i: (bi, h, i, 0)),
                   pl.BlockSpec((1, 1, tq // sub, 8, LANES), lambda bi, h, i: (bi, h, i, 0, 0))],
        out_shape=[jax.ShapeDtypeStruct((b, hkv, s, n_sel), BF16),
                   jax.ShapeDtypeStruct((b, hkv, s // sub, 8, LANES), jnp.int32)],
        compiler_params=_params(("parallel", "parallel", "parallel")),
        name="nsa_topk_select",
    )(imp, jnp.asarray(chunk_of.astype(np.float32), BF16))


AUG_POS_HI, AUG_POS_LO, AUG_SEL = 0, 3, 6
MASK_BIG = 2.0 ** 100


def _sel_query_aug(n_heads, n_kv, tq):
    g = n_heads // n_kv
    sl = jnp.asarray(np.asarray(_alibi_slopes(n_heads), np.float32) * np.float32(LOG2E))
    hi = sl.astype(BF16).astype(F32)
    mid = (sl - hi).astype(BF16).astype(F32)
    lo = (sl - hi - mid).astype(BF16).astype(F32)
    pieces = jnp.stack([hi, mid, lo], axis=-1)
    cols = jnp.zeros((n_heads, HEAD_DIM), F32)
    cols = cols.at[:, AUG_POS_HI:AUG_POS_HI + 3].set(pieces * LANES)
    cols = cols.at[:, AUG_POS_LO:AUG_POS_LO + 3].set(pieces)
    cols = jnp.broadcast_to(cols.reshape(n_kv, g, 1, HEAD_DIM), (n_kv, g, tq, HEAD_DIM))
    return cols.reshape(n_kv, g * tq, HEAD_DIM).astype(BF16)


def _sel_key_aug(tk, rows):
    assert rows % tk == 0
    c = np.arange(rows) % tk
    cols = np.zeros((rows, HEAD_DIM), np.float32)
    cols[:, AUG_POS_HI:AUG_POS_HI + 3] = (c // 256 * 2)[:, None]
    cols[:, AUG_POS_LO:AUG_POS_LO + 3] = (c % 256)[:, None]
    cols[np.arange(rows), AUG_SEL + c // SEL_BLOCK] = MASK_BIG
    return jnp.asarray(cols, BF16)


def _sel_attn_body(need_ref, q_ref, qa_ref, k_ref, v_ref, sel_ref, o_ref,
                   qaug_ref, m_ref, acc_ref, *, g, tq, tk, n_chunks_total):
    bi, h, i = pl.program_id(0), pl.program_id(1), pl.program_id(2)
    n_sel = sel_ref.shape[3]
    per_chunk = tk // SEL_BLOCK
    rows = g * tq
    qaug_ref[...] = jnp.concatenate([q_ref[0, 0].reshape(rows, HEAD_DIM), qa_ref[0]], axis=-1)
    m_ref[...] = jnp.full(m_ref.shape, M_INIT, F32)
    acc_ref[...] = jnp.zeros(acc_ref.shape, F32)
    sel = sel_ref[0, 0]
    lane = lax.broadcasted_iota(jnp.int32, (1, LANES), 1)
    pos_lanes = (lane >= HEAD_DIM + AUG_POS_HI) & (lane < HEAD_DIM + AUG_POS_HI + 3)
    sel_lanes = (lane >= HEAD_DIM + AUG_SEL) & (lane < HEAD_DIM + AUG_SEL + per_chunk)
    jrow = lax.broadcasted_iota(jnp.int32, (n_sel, 1), 0)
    r = lax.broadcasted_iota(jnp.int32, (rows, 1), 0) % tq
    c = lax.broadcasted_iota(jnp.int32, (1, tk), 1)

    n_parts = 1

    def chunk(ci, diagonal):
        start = pl.multiple_of(ci * tk, tk)
        shift = (ci * (tk // LANES) - i * (tq // LANES)).astype(F32)
        k = k_ref[0, 0, pl.ds(start, tk), :] + jnp.where(pos_lanes, shift, 0.0).astype(BF16)
        v = v_ref[0, 0, pl.ds(start, tk), :]
        place = jnp.where(sel_lanes & (jrow == ci * per_chunk + lane - (HEAD_DIM + AUG_SEL)), 1.0, 0.0)
        picked = jnp.dot(sel, place.astype(BF16), preferred_element_type=F32)
        unpicked = jnp.where(sel_lanes, picked - 1.0, 0.0).astype(BF16)
        for part in range(n_parts):
            rs = slice(part * rows // n_parts, (part + 1) * rows // n_parts)
            s = _nt(qaug_ref[rs, :] + jnp.concatenate([unpicked] * (g // n_parts), axis=0), k)
            if diagonal:
                s = jnp.where(r[rs] + (i * tq - ci * tk) >= c, s, NEG)
            m_old = m_ref[rs, :]
            m_new = jnp.maximum(m_old, jnp.max(s, axis=-1, keepdims=True))
            e = jnp.exp2(s - jnp.concatenate([m_new] * (tk // LANES), axis=1))
            acc_ref[rs, :] = (jnp.exp2(m_old - m_new) * acc_ref[rs, :]
                              + jnp.dot(e.astype(BF16), v, preferred_element_type=F32))
            m_ref[rs, :] = m_new

    diag = (i * tq) // tk
    base = ((bi * pl.num_programs(1) + h) * pl.num_programs(2) + i) * n_chunks_total

    def maybe(ci, carry):
        @pl.when(need_ref[base + ci] != 0)
        def _():
            chunk(ci, False)
        return carry

    lax.fori_loop(0, diag, maybe, 0)
    chunk(diag, True)
    acc = acc_ref[...]
    o = (acc / pltpu.roll(acc, HEAD_DIM, 1))[:, :HEAD_DIM]
    o_ref[0] = jnp.concatenate([o[gi * tq:(gi + 1) * tq] for gi in range(g)], axis=-1)


def _sel_attention(q, k_aug, v, sel, need, tq, tk):
    b, hq, s, _ = q.shape
    hkv = k_aug.shape[1]
    g = hq // hkv
    n_sel = sel.shape[3]
    grid_spec = pltpu.PrefetchScalarGridSpec(
        num_scalar_prefetch=1,
        grid=(b, hkv, s // tq),
        in_specs=[pl.BlockSpec((1, 1, g, tq, HEAD_DIM), lambda bi, h, i, nd: (bi, h, 0, i, 0)),
                  pl.BlockSpec((1, g * tq, HEAD_DIM), lambda bi, h, i, nd: (h, 0, 0)),
                  pl.BlockSpec((1, 1, s, LANES), lambda bi, h, i, nd: (bi, h, 0, 0)),
                  pl.BlockSpec((1, 1, s, LANES), lambda bi, h, i, nd: (bi, h, 0, 0)),
                  pl.BlockSpec((1, 1, tq, n_sel), lambda bi, h, i, nd: (bi, h, i, 0))],
        out_specs=pl.BlockSpec((1, tq, g * HEAD_DIM), lambda bi, h, i, nd: (bi, i, h)),
        scratch_shapes=[pltpu.VMEM((g * tq, LANES), BF16),
                        pltpu.VMEM((g * tq, LANES), F32), pltpu.VMEM((g * tq, LANES), F32)])
    return pl.pallas_call(
        functools.partial(_sel_attn_body, g=g, tq=tq, tk=tk, n_chunks_total=s // tk),
        grid_spec=grid_spec,
        out_shape=jax.ShapeDtypeStruct((b, s, hq * HEAD_DIM), F32),
        compiler_params=_params(("parallel", "parallel", "arbitrary")),
        name="nsa_selected_attention",
    )(need, q.reshape(b, hkv, g, s, HEAD_DIM), _sel_query_aug(hq, hkv, tq), k_aug, v, sel)


def _nsa_out_body(x_ref, gt_ref, oc_ref, os_ref, ow_ref, ex_ref, w_ref, o_ref):
    sig = _sigmoid(gt_ref[...])
    sig_hi = sig.astype(BF16)
    sig_lo = (sig - sig_hi.astype(F32)).astype(BF16)
    mix = jnp.zeros(oc_ref.shape, F32)
    for j, br_ref in enumerate((oc_ref, os_ref, ow_ref)):
        gate = (jnp.dot(sig_hi, ex_ref[j], preferred_element_type=F32)
                + jnp.dot(sig_lo, ex_ref[j], preferred_element_type=F32))
        mix = mix + gate * br_ref[...]
    o_ref[...] = x_ref[...] + jnp.dot(mix.astype(BF16), w_ref[...], preferred_element_type=F32)


def _nsa_out(x, gates, o_cmp, o_sel, o_win, w_out, tm=256):
    t, d = x.shape
    c = o_cmp.shape[1]
    expand = np.zeros((3, LANES, c), np.float32)
    for head in range(NSA_HQ):
        for j in range(3):
            expand[j, head * 3 + j, head * HEAD_DIM:(head + 1) * HEAD_DIM] = 1.0
    row = lambda width: pl.BlockSpec((tm, width), lambda i: (i, 0))
    return pl.pallas_call(
        _nsa_out_body,
        grid=(t // tm,),
        in_specs=[row(d), row(LANES), row(c), row(c), row(c),
                  pl.BlockSpec((3, LANES, c), lambda i: (0, 0, 0)),
                  pl.BlockSpec((c, d), lambda i: (0, 0))],
        out_specs=row(d),
        out_shape=jax.ShapeDtypeStruct((t, d), F32),
        compiler_params=_params(("parallel",)),
        name="nsa_out",
    )(x, gates, o_cmp, o_sel, o_win, jnp.asarray(expand, BF16), w_out)


def _nsa_layer(x, b, s, mix_norm, w_in, w_out, q_norm, k_norm, cmp_pos, cmp_w1, cmp_w2):
    dh = HEAD_DIM
    kvw = NSA_HKV * dh
    edges = np.cumsum([0, NSA_HQ * dh] + [kvw] * 6).tolist()
    ws = [w_in[:, edges[j]:edges[j + 1]].astype(BF16) for j in range(7)]
    n_gate = 3 * NSA_HQ
    ws.append(jnp.pad(w_in[:, edges[7]:edges[7] + n_gate], ((0, 0), (0, LANES - n_gate))).astype(BF16))
    q, kc, vc, ks, vs, kw, vw, gates = _norm_proj(x, mix_norm, ws)
    as3 = lambda y: y.reshape(b, s, y.shape[-1])
    qh = _head_prep(as3(q), q_norm, dh ** -0.5 * LOG2E, BF16)
    kch = _compress(_head_prep(as3(kc), None, 1.0, F32), cmp_pos[0], cmp_w1[0], cmp_w2[0], k_norm[0],
                    extra=_cmp_key_lanes(s // CMP_STRIDE))
    vch = _compress(_head_prep(as3(vc), None, 1.0, F32), cmp_pos[1], cmp_w1[1], cmp_w2[1], None)
    sel_tq, sel_tk, prep_ts = 256, min(512, s), min(512, s)
    ksh = _head_prep(as3(ks), k_norm[1], 1.0, BF16, extra=_sel_key_aug(sel_tk, prep_ts), ts=prep_ts)
    win_tq = min(512, s)
    vsh = _head_prep(as3(vs), None, 1.0, BF16, extra=_ones_lanes(prep_ts), ts=prep_ts)
    kwh = _head_prep(as3(kw), k_norm[2], 1.0, BF16, extra=_key_position_lanes(BAND_SUB, prep_ts), ts=prep_ts)
    vwh = _head_prep(as3(vw), None, 1.0, BF16, extra=_ones_lanes(prep_ts), ts=prep_ts)
    o_cmp, imp = _cmp_attention(qh, kch, vch, tq=256)
    sel, need = _topk_select(imp, sel_tq, sel_tk)
    need = need[:, :, :, 0, :s // sel_tk].reshape(-1)
    o_sel = _sel_attention(qh, ksh, vsh, sel, need, sel_tq, sel_tk)
    o_win = _banded_attention(qh, kwh, vwh, NSA_HQ, NSA_WINDOW, tq=win_tq)
    flat = lambda y: y.reshape(b * s, y.shape[-1])
    return _nsa_out(x, gates, flat(o_cmp), flat(o_sel), flat(o_win), w_out.astype(BF16))


def _diff_prep_body(y_ref, gq_ref, gk_ref, q_ref, k_ref, v_ref, *, width, q_scale, tk):
    tm = y_ref.shape[0]
    lane = lax.broadcasted_iota(jnp.int32, (1, LANES), 1)
    low = lane < HEAD_DIM
    c = (pl.program_id(0) * tm + lax.broadcasted_iota(jnp.int32, (tm, 1), 0)) % tk
    key_lanes = jnp.where((lane >= AUG_POS_HI) & (lane < AUG_POS_HI + 3), (c // 256) * 2,
                          jnp.where((lane >= AUG_POS_LO) & (lane < AUG_POS_LO + 3), c % 256, 0)
                          ).astype(F32).astype(BF16)

    def pair_norm(ysl, gain):
        sq = ysl * ysl
        ss_lo = jnp.sum(jnp.where(low, sq, 0.0), axis=-1, keepdims=True)
        ss_hi = jnp.sum(jnp.where(low, 0.0, sq), axis=-1, keepdims=True)
        inv = jnp.where(low, lax.rsqrt(ss_lo / HEAD_DIM + EPS), lax.rsqrt(ss_hi / HEAD_DIM + EPS))
        return ysl * inv * gain

    for j in range(width // LANES):
        sl = slice(j * LANES, (j + 1) * LANES)
        q_ref[:, sl] = (pair_norm(y_ref[:, sl], gq_ref[...]) * q_scale).astype(BF16)
        k_ref[:, 2 * j * LANES:(2 * j + 1) * LANES] = pair_norm(
            y_ref[:, width + j * LANES:width + (j + 1) * LANES], gk_ref[...]).astype(BF16)
        k_ref[:, (2 * j + 1) * LANES:(2 * j + 2) * LANES] = key_lanes
        v_ref[:, sl] = y_ref[:, 2 * width + j * LANES:2 * width + (j + 1) * LANES].astype(BF16)


def _diff_prep(qkv, q_norm, k_norm, tk, tm=512):
    t, w3 = qkv.shape
    width = w3 // 3
    pair = lambda gvec: jnp.concatenate([gvec, gvec]).reshape(1, LANES).astype(F32)
    out = jax.ShapeDtypeStruct((t, width), BF16)
    return pl.pallas_call(
        functools.partial(_diff_prep_body, width=width, q_scale=HEAD_DIM ** -0.5 * LOG2E, tk=tk),
        grid=(t // tm,),
        in_specs=[pl.BlockSpec((tm, w3), lambda i: (i, 0)),
                  pl.BlockSpec((1, LANES), lambda i: (0, 0)),
                  pl.BlockSpec((1, LANES), lambda i: (0, 0))],
        out_specs=[pl.BlockSpec((tm, width), lambda i: (i, 0)),
                   pl.BlockSpec((tm, 2 * width), lambda i: (i, 0)),
                   pl.BlockSpec((tm, width), lambda i: (i, 0))],
        out_shape=[out, jax.ShapeDtypeStruct((t, 2 * width), BF16), out],
        compiler_params=_params(("parallel",)),
        name="diff_prep",
    )(qkv, pair(q_norm), pair(k_norm))


def _diff_flash_body(reach_ref, q_ref, qa_ref, k_ref, v_ref, lam_ref, sub_ref, o_ref,
                     *, tq, tk, lam_init):
    h, i = pl.program_id(1), pl.program_id(2)
    q = q_ref[0]
    lane = lax.broadcasted_iota(jnp.int32, (1, LANES), 1)
    zero = jnp.zeros_like(q)
    q_both = jnp.concatenate([jnp.where(lane < HEAD_DIM, q, zero),
                              jnp.where(lane < HEAD_DIM, zero, q)], axis=0)
    q_aug = jnp.concatenate([q_both, qa_ref[0]], axis=-1)
    lane2 = lax.broadcasted_iota(jnp.int32, (1, 2 * LANES), 1)
    pos_lanes = (lane2 >= LANES + AUG_POS_HI) & (lane2 < LANES + AUG_POS_HI + 3)
    r = lax.broadcasted_iota(jnp.int32, (2 * tq, 1), 0) % tq
    c = lax.broadcasted_iota(jnp.int32, (1, tk), 1)

    def step(ci, carry, diagonal):
        m, l, acc = carry
        start = pl.multiple_of(ci * tk, tk)
        shift = (ci * (tk // LANES) - i * (tq // LANES)).astype(F32)
        k = k_ref[0, pl.ds(start, tk), :] + jnp.where(pos_lanes, shift, 0.0).astype(BF16)
        v = v_ref[0, pl.ds(start, tk), :]
        s = _nt(q_aug, k)
        if diagonal:
            s = jnp.where(r + (i * tq - ci * tk) >= c, s, NEG)
        m_new = jnp.maximum(m, jnp.max(s, axis=-1, keepdims=True))
        alpha = jnp.exp2(m - m_new)
        e = jnp.exp2(s - jnp.concatenate([m_new] * (tk // LANES), axis=1))
        l = alpha * l + jnp.sum(e, axis=-1, keepdims=True)
        acc = alpha * acc + jnp.dot(e.astype(BF16), v, preferred_element_type=F32)
        return m_new, l, acc

    init = (jnp.full((2 * tq, LANES), M_INIT, F32), jnp.zeros((2 * tq, LANES), F32),
            jnp.zeros((2 * tq, LANES), F32))
    diag = (i * tq) // tk
    first = jnp.maximum(i * tq - reach_ref[h], 0) // tk
    carry = lax.fori_loop(first, diag, lambda ci, cr: step(ci, cr, False), init)
    _, l, acc = step(diag, carry, True)
    lam = lam_ref[...]
    lam01 = jnp.sum(lam[0:1] * lam[1:2], axis=-1, keepdims=True)
    lam23 = jnp.sum(lam[2:3] * lam[3:4], axis=-1, keepdims=True)
    lmbda = jnp.exp(lam01) - jnp.exp(lam23) + lam_init
    o = acc / l
    o = o[:tq] - lmbda * o[tq:]
    o_ref[0] = _rms(o, sub_ref[...]) * (1.0 - lam_init)


def _diff_flash(qn, k_aug, vb, reach, lam, subln, lam_init, tq, tk):
    b, s, w = qn.shape
    n_heads = w // LANES
    sl = jnp.asarray(np.asarray(_alibi_slopes(n_heads), np.float32) * np.float32(LOG2E))
    hi = sl.astype(BF16).astype(F32)
    mid = (sl - hi).astype(BF16).astype(F32)
    lo = (sl - hi - mid).astype(BF16).astype(F32)
    pieces = jnp.stack([hi, mid, lo], axis=-1)
    q_aug = jnp.zeros((n_heads, LANES), F32)
    q_aug = q_aug.at[:, AUG_POS_HI:AUG_POS_HI + 3].set(pieces * LANES)
    q_aug = q_aug.at[:, AUG_POS_LO:AUG_POS_LO + 3].set(pieces)
    q_aug = jnp.broadcast_to(q_aug[:, None, :], (n_heads, 2 * tq, LANES)).astype(BF16)
    grid_spec = pltpu.PrefetchScalarGridSpec(
        num_scalar_prefetch=1,
        grid=(b, n_heads, s // tq),
        in_specs=[pl.BlockSpec((1, tq, LANES), lambda bi, h, i, rc: (bi, i, h)),
                  pl.BlockSpec((1, 2 * tq, LANES), lambda bi, h, i, rc: (h, 0, 0)),
                  pl.BlockSpec((1, s, 2 * LANES), lambda bi, h, i, rc: (bi, 0, h)),
                  pl.BlockSpec((1, s, LANES), lambda bi, h, i, rc: (bi, 0, h)),
                  pl.BlockSpec(lam.shape, lambda bi, h, i, rc: (0, 0)),
                  pl.BlockSpec((1, LANES), lambda bi, h, i, rc: (0, 0))],
        out_specs=pl.BlockSpec((1, tq, LANES), lambda bi, h, i, rc: (bi, i, h)))
    return pl.pallas_call(
        functools.partial(_diff_flash_body, tq=tq, tk=tk, lam_init=lam_init),
        grid_spec=grid_spec,
        out_shape=jax.ShapeDtypeStruct((b, s, w), F32),
        compiler_params=_params(("parallel", "parallel", "arbitrary")),
        name="diff_flash",
    )(reach, qn, q_aug, k_aug, vb, lam.astype(F32), subln.reshape(1, LANES).astype(F32))


def _diff_reach(q_norm, k_norm, n_heads, s):
    bound = (HEAD_DIM * jnp.max(jnp.abs(q_norm)) * jnp.max(jnp.abs(k_norm))
             * (HEAD_DIM ** -0.5 * LOG2E) * 1.02)
    slopes = jnp.asarray(np.asarray(_alibi_slopes(n_heads), np.float32) * np.float32(LOG2E))
    reach = jnp.ceil((2.0 * bound + 152.0) / slopes) + 1.0
    return jnp.minimum(reach, float(s)).astype(jnp.int32)


def _diff_layer(x, b, s, mix_norm, w_in, w_out, q_norm, k_norm, lam, subln, lam_init):
    tq, tk = min(1024, s), min(1024, s)
    (qkv,) = _norm_proj(x, mix_norm, [w_in.astype(BF16)])
    qn, k_aug, vb = _diff_prep(qkv, q_norm, k_norm, tk)
    as3 = lambda y: y.reshape(b, s, y.shape[-1])
    o = _diff_flash(as3(qn), as3(k_aug), as3(vb), _diff_reach(q_norm, k_norm, DIFF_H, s),
                    lam, subln, lam_init, tq, tk)
    return _proj_residual(x, o.reshape(b * s, -1), w_out.astype(BF16))


def _gdn_prep_body(cur_ref, prev_ref, cw_ref, b_ref, a_ref, alog_ref, dt_ref,
                   q_ref, k_ref, v_ref, beta_ref, gc_ref, pad_ref, *, ts, width):
    i = pl.program_id(1)
    halo = prev_ref.shape[1]
    pad_ref[0:halo, :] = jnp.where(i > 0, prev_ref[0], 0.0)
    pad_ref[halo:, :] = cur_ref[0]
    y = jnp.zeros((ts, 3 * width), F32)
    for j in range(GDN_CONV):
        off = halo - (GDN_CONV - 1) + j
        y = y + cw_ref[j:j + 1, :] * pad_ref[off:off + ts, :]
    y = y * _sigmoid(y)
    for h in range(width // GDN_DK):
        sl = slice(h * GDN_DK, (h + 1) * GDN_DK)
        qh = y[:, sl]
        q_ref[0, :, sl] = qh * lax.rsqrt(jnp.sum(qh * qh, axis=-1, keepdims=True) + EPS) * (GDN_DK ** -0.5)
        kh = y[:, width + h * GDN_DK:width + (h + 1) * GDN_DK]
        k_ref[0, :, sl] = kh * lax.rsqrt(jnp.sum(kh * kh, axis=-1, keepdims=True) + EPS)
    v_ref[0] = y[:, 2 * width:]
    beta_ref[0] = _sigmoid(b_ref[0])
    z = a_ref[0] + dt_ref[...]
    softplus = jnp.maximum(z, 0.0) + jnp.log(1.0 + jnp.exp(-jnp.abs(z)))
    gl = -jnp.exp(alog_ref[...]) * softplus
    r = lax.broadcasted_iota(jnp.int32, (ts, ts), 0)
    c = lax.broadcasted_iota(jnp.int32, (ts, ts), 1)
    tri = jnp.where((r // GDN_CHUNK == c // GDN_CHUNK) & (c <= r), 1.0, 0.0)
    gc_ref[0] = jnp.dot(tri, gl, precision=HIGHEST, preferred_element_type=F32)


def _gdn_prep(qkv, b_pad, a_pad, conv_w, a_log, dt_bias, ts=256):
    b, s, w3 = qkv.shape
    width = w3 // 3
    halo = 8
    pad_lane = lambda vec: jnp.pad(vec.astype(F32), (0, LANES - vec.shape[0])).reshape(1, LANES)
    big = jax.ShapeDtypeStruct((b, s, width), F32)
    small = jax.ShapeDtypeStruct((b, s, LANES), F32)
    tok = lambda wd: pl.BlockSpec((1, ts, wd), lambda bi, i: (bi, i, 0))
    return pl.pallas_call(
        functools.partial(_gdn_prep_body, ts=ts, width=width),
        grid=(b, s // ts),
        in_specs=[tok(w3),
                  pl.BlockSpec((1, halo, w3), lambda bi, i: (bi, jnp.maximum(i * (ts // halo) - 1, 0), 0)),
                  pl.BlockSpec((GDN_CONV, w3), lambda bi, i: (0, 0)),
                  tok(LANES), tok(LANES),
                  pl.BlockSpec((1, LANES), lambda bi, i: (0, 0)),
                  pl.BlockSpec((1, LANES), lambda bi, i: (0, 0))],
        out_specs=[tok(width), tok(width), tok(width), tok(LANES), tok(LANES)],
        out_shape=[big, big, big, small, small],
        scratch_shapes=[pltpu.VMEM((ts + halo, w3), F32)],
        compiler_params=_params(("parallel", "parallel")),
        name="gdn_prep",
    )(qkv, qkv, conv_w.astype(F32), b_pad, a_pad, pad_lane(a_log), pad_lane(dt_bias))


def _unit_lower_inverses(mats):
    n = mats[0].shape[0]
    r = lax.broadcasted_iota(jnp.int32, (n, n), 0)
    c = lax.broadcasted_iota(jnp.int32, (n, n), 1)
    eye = jnp.where(r == c, 1.0, 0.0)
    d = 1
    inv = None
    while d < n:
        below = (r // (2 * d) == c // (2 * d)) & (r // d != c // d)
        offs = [jnp.where(below, a, 0.0) for a in mats]
        if d == 1:
            inv = [eye - off for off in offs]
        else:
            left = [_dot3(t, off) for t, off in zip(inv, offs)]
            inv = [t - _dot3(lt, t) for t, lt in zip(inv, left)]
        d *= 2
    return inv


def _gdn_chunk_body(q_ref, k_ref, v_ref, beta_ref, gc_ref, gr_ref, gate_ref, on_ref, o_ref,
                    state_ref, *, nc, n_heads):
    @pl.when(pl.program_id(1) == 0)
    def _():
        state_ref[...] = jnp.zeros(state_ref.shape, F32)

    cl = GDN_CHUNK
    r = lax.broadcasted_iota(jnp.int32, (cl, cl), 0)
    c = lax.broadcasted_iota(jnp.int32, (cl, cl), 1)
    fdot = functools.partial(jnp.dot, preferred_element_type=F32)
    blocks = [(ch, h) for ch in range(nc) for h in range(n_heads)]
    rows = lambda ch: slice(ch * cl, (ch + 1) * cl)
    lanes = lambda h: slice(h * GDN_DK, (h + 1) * GDN_DK)

    qs = [q_ref[0, rows(ch), lanes(h)] for ch, h in blocks]
    ks = [k_ref[0, rows(ch), lanes(h)] for ch, h in blocks]
    bcol = [beta_ref[0, rows(ch), h:h + 1] for ch, h in blocks]
    gcol = [gc_ref[0, rows(ch), h:h + 1] for ch, h in blocks]
    glast = [gc_ref[0, ch * cl + cl - 1:ch * cl + cl, h:h + 1] for ch, h in blocks]
    decay = [jnp.exp(jnp.minimum(g - gr_ref[0, ch, h:h + 1, :], 0.0))
             for g, (ch, h) in zip(gcol, blocks)]
    kb = [k * b for k, b in zip(ks, bcol)]
    amat = [jnp.where(r > c, _dot3(x, k, ((1,), (1,))) * dc, 0.0) for x, k, dc in zip(kb, ks, decay)]
    inv = _unit_lower_inverses(amat)
    eg = [jnp.exp(g) for g in gcol]
    u = [_dot3(t, v_ref[0, rows(ch), lanes(h)] * b) for t, b, (ch, h) in zip(inv, bcol, blocks)]
    w = [_dot3(t, x * e) for t, x, e in zip(inv, kb, eg)]
    attn = [jnp.where(r >= c, _nt(q, k) * dc, 0.0) for q, k, dc in zip(qs, ks, decay)]

    for ch in range(nc):
        ids = range(ch * n_heads, (ch + 1) * n_heads)
        state = [state_ref[h] for h in range(n_heads)]
        v_new = [u[i] - fdot(w[i], st) for i, st in zip(ids, state)]
        o = [fdot(qs[i] * eg[i], st) + fdot(attn[i], vn) for i, st, vn in zip(ids, state, v_new)]
        for h, (i, st, vn) in enumerate(zip(ids, state, v_new)):
            state_ref[h] = st * jnp.exp(glast[i]) + _tn(ks[i] * jnp.exp(glast[i] - gcol[i]), vn)
        for h in range(n_heads):
            gate = gate_ref[0, rows(ch), lanes(h)]
            o_ref[0, rows(ch), lanes(h)] = _rms(o[h], on_ref[...]) * (gate * _sigmoid(gate))


def _gdn_chunks(q, k, v, beta, gc, gc_rows, gate, o_norm, nc=1):
    b, s, w = q.shape
    n_heads = w // GDN_DK
    ts = nc * GDN_CHUNK
    tok = lambda wd: pl.BlockSpec((1, ts, wd), lambda bi, i: (bi, i, 0))
    return pl.pallas_call(
        functools.partial(_gdn_chunk_body, nc=nc, n_heads=n_heads),
        grid=(b, s // ts),
        in_specs=[tok(w), tok(w), tok(w), tok(LANES), tok(LANES),
                  pl.BlockSpec((1, nc, 8, GDN_CHUNK), lambda bi, i: (bi, i, 0, 0)),
                  tok(w),
                  pl.BlockSpec((1, GDN_DV), lambda bi, i: (0, 0))],
        out_specs=tok(w),
        out_shape=jax.ShapeDtypeStruct((b, s, w), F32),
        scratch_shapes=[pltpu.VMEM((n_heads, GDN_DK, GDN_DV), F32)],
        compiler_params=_params(("parallel", "arbitrary")),
        name="gdn_chunks",
    )(q, k, v, beta, gc, gc_rows, gate, o_norm.reshape(1, GDN_DV).astype(F32))


def _gdn_layer(x, b, s, mix_norm, w_in, w_out, conv_w, a_log, dt_bias, o_norm):
    qkv_w = GDN_H * (2 * GDN_DK + GDN_DV)
    lane_pad = lambda wcols: jnp.pad(wcols, ((0, 0), (0, LANES - wcols.shape[1]))).astype(BF16)
    ws = [w_in[:, :qkv_w].astype(BF16),
          lane_pad(w_in[:, qkv_w:qkv_w + GDN_H]),
          lane_pad(w_in[:, qkv_w + GDN_H:qkv_w + 2 * GDN_H]),
          w_in[:, qkv_w + 2 * GDN_H:].astype(BF16)]
    qkv, b_pad, a_pad, gate = _norm_proj(x, mix_norm, ws)
    as3 = lambda y: y.reshape(b, s, y.shape[-1])
    q, k, v, beta, gc = _gdn_prep(as3(qkv), as3(b_pad), as3(a_pad), conv_w, a_log, dt_bias,
                                  ts=min(256, s))
    gc_rows = gc[:, :, :8].reshape(b, s // GDN_CHUNK, GDN_CHUNK, 8).transpose(0, 1, 3, 2)
    o = _gdn_chunks(q, k, v, beta, gc, gc_rows, as3(gate), o_norm, nc=2)
    return _proj_residual(x, o.reshape(b * s, -1), w_out.astype(BF16))


def _swa_layer(x, b, s, mix_norm, w_in, w_out, q_norm, k_norm, sinks):
    dh = HEAD_DIM
    e0, e1 = SWA_HQ * dh, (SWA_HQ + SWA_HKV) * dh
    ws = [w_in[:, :e0].astype(BF16), w_in[:, e0:e1].astype(BF16), w_in[:, e1:].astype(BF16)]
    q, k, v = _norm_proj(x, mix_norm, ws)
    as3 = lambda y: y.reshape(b, s, y.shape[-1])
    qh = _head_prep(as3(q), q_norm, dh ** -0.5 * LOG2E, BF16)
    tq, prep_ts = min(512, s), min(512, s)
    kh = _head_prep(as3(k), k_norm, 1.0, BF16, extra=_key_position_lanes(BAND_SUB, prep_ts), ts=prep_ts)
    vh = _head_prep(as3(v), None, 1.0, BF16, extra=_ones_lanes(prep_ts), ts=prep_ts)
    o = _banded_attention(qh, kh, vh, SWA_HQ, SWA_WINDOW, tq=tq, sinks=sinks)
    return _proj_residual(x, o.reshape(b * s, -1), w_out.astype(BF16))


def kernel(x, ffn1_norm, ffn1_w_in, ffn1_w_out, mix_norm, ffn2_norm, ffn2_w_in, ffn2_w_out,
           nsa_w_in, nsa_w_out, nsa_q_norm, nsa_k_norm, nsa_cmp_pos, nsa_cmp_w1, nsa_cmp_w2,
           diff_w_in, diff_w_out, diff_q_norm, diff_k_norm, diff_lambda, diff_subln,
           gdn_w_in, gdn_w_out, gdn_conv_w, gdn_a_log, gdn_dt_bias, gdn_o_norm,
           swa_w_in, swa_w_out, swa_q_norm, swa_k_norm, swa_sinks):
    b, s, d = x.shape
    depth = ffn1_norm.shape[0]
    n_mixers = 4
    h = x.reshape(b * s, d)
    for layer in range(depth):
        kind, j = layer % n_mixers, layer // n_mixers
        h = _ffn(h, ffn1_norm[layer], ffn1_w_in[layer].astype(BF16), ffn1_w_out[layer].astype(BF16))
        if kind == 0:
            h = _nsa_layer(h, b, s, mix_norm[layer], nsa_w_in[j], nsa_w_out[j], nsa_q_norm[j],
                           nsa_k_norm[j], nsa_cmp_pos[j], nsa_cmp_w1[j], nsa_cmp_w2[j])
        elif kind == 1:
            lam_init = 0.8 - 0.6 * math.exp(-0.3 * layer)
            h = _diff_layer(h, b, s, mix_norm[layer], diff_w_in[j], diff_w_out[j], diff_q_norm[j],
                            diff_k_norm[j], diff_lambda[j], diff_subln[j], lam_init)
        elif kind == 2:
            h = _gdn_layer(h, b, s, mix_norm[layer], gdn_w_in[j], gdn_w_out[j], gdn_conv_w[j],
                           gdn_a_log[j], gdn_dt_bias[j], gdn_o_norm[j])
        else:
            h = _swa_layer(h, b, s, mix_norm[layer], swa_w_in[j], swa_w_out[j], swa_q_norm[j],
                           swa_k_norm[j], swa_sinks[j])
        h = _ffn(h, ffn2_norm[layer], ffn2_w_in[layer].astype(BF16), ffn2_w_out[layer].astype(BF16))
    return h.reshape(b, s, d)
```

```python
import functools
import math

import numpy as np
import jax
import jax.numpy as jnp
from jax import lax
from jax.experimental import pallas as pl
from jax.experimental.pallas import tpu as pltpu

F32 = jnp.float32
BF16 = jnp.bfloat16
HIGHEST = lax.Precision.HIGHEST

EPS = 1e-6
NEG = -1e30
M_INIT = -1e29
LOG2E = 1.4426950408889634
FORCE = 1e9

HEAD_DIM = 64
LANES = 128
VMEM_LIMIT = 56 * 1024 * 1024

NSA_HQ, NSA_HKV = 16, 4
CMP_BLOCK, CMP_STRIDE = 32, 16
SEL_BLOCK, SEL_TOPK = 64, 16
NSA_WINDOW = 512
DIFF_H = 8
GDN_H, GDN_DK, GDN_DV, GDN_CONV, GDN_CHUNK = 8, 128, 128, 4, 64
SWA_HQ, SWA_HKV, SWA_WINDOW = 16, 4, 128


def _params(sem):
    return pltpu.CompilerParams(dimension_semantics=sem, vmem_limit_bytes=VMEM_LIMIT)


def _nt(a, b, precision=None):
    return lax.dot_general(a, b, (((1,), (1,)), ((), ())), precision=precision,
                           preferred_element_type=F32)


def _tn(a, b, precision=None):
    return lax.dot_general(a, b, (((0,), (0,)), ((), ())), precision=precision,
                           preferred_element_type=F32)


def _dot3(a, b, contract=((1,), (0,))):
    a_hi, b_hi = a.astype(BF16), b.astype(BF16)
    a_lo = (a - a_hi.astype(F32)).astype(BF16)
    b_lo = (b - b_hi.astype(F32)).astype(BF16)
    dot = lambda x, y: lax.dot_general(x, y, (contract, ((), ())), preferred_element_type=F32)
    return dot(a_hi, b_hi) + dot(a_hi, b_lo) + dot(a_lo, b_hi)


def _rms(x, g):
    return x * lax.rsqrt(jnp.mean(x * x, axis=-1, keepdims=True) + EPS) * g


def _sigmoid(x):
    return 1.0 / (1.0 + jnp.exp(-x))


def _alibi_slopes(n):
    return [2.0 ** (-8.0 * (i + 1) / n) for i in range(n)]


def _ffn_body(x_ref, g_ref, win_ref, wout_ref, o_ref, *, d_ff, chunk):
    x = x_ref[...]
    xn = _rms(x, g_ref[...]).astype(BF16)
    acc = jnp.zeros(x.shape, F32)
    for c in range(d_ff // chunk):
        a = jnp.dot(xn, win_ref[:, c * chunk:(c + 1) * chunk], preferred_element_type=F32)
        b = jnp.dot(xn, win_ref[:, d_ff + c * chunk:d_ff + (c + 1) * chunk],
                    preferred_element_type=F32)
        h = (a * _sigmoid(a) * b).astype(BF16)
        acc = acc + jnp.dot(h, wout_ref[c * chunk:(c + 1) * chunk, :], preferred_element_type=F32)
    o_ref[...] = x + 0.5 * acc


def _ffn(x, g, w_in, w_out, tm=512, chunk=256):
    t, d = x.shape
    d_ff = w_out.shape[0]
    return pl.pallas_call(
        functools.partial(_ffn_body, d_ff=d_ff, chunk=chunk),
        grid=(t // tm,),
        in_specs=[pl.BlockSpec((tm, d), lambda i: (i, 0)),
                  pl.BlockSpec((1, d), lambda i: (0, 0)),
                  pl.BlockSpec(w_in.shape, lambda i: (0, 0), pipeline_mode=pl.Buffered(1)),
                  pl.BlockSpec(w_out.shape, lambda i: (0, 0), pipeline_mode=pl.Buffered(1))],
        out_specs=pl.BlockSpec((tm, d), lambda i: (i, 0)),
        out_shape=jax.ShapeDtypeStruct((t, d), F32),
        compiler_params=_params(("parallel",)),
        name="ffn",
    )(x, g.reshape(1, d), w_in, w_out)


def _norm_proj_body(x_ref, g_ref, *refs, n):
    xn = _rms(x_ref[...], g_ref[...]).astype(BF16)
    for w_ref, o_ref in zip(refs[:n], refs[n:]):
        o_ref[...] = jnp.dot(xn, w_ref[...], preferred_element_type=F32).astype(o_ref.dtype)


def _norm_proj(x, g, ws, tm=512):
    t, d = x.shape
    n = len(ws)
    return pl.pallas_call(
        functools.partial(_norm_proj_body, n=n),
        grid=(t // tm,),
        in_specs=[pl.BlockSpec((tm, d), lambda i: (i, 0)),
                  pl.BlockSpec((1, d), lambda i: (0, 0))]
                 + [pl.BlockSpec(w.shape, lambda i: (0, 0)) for w in ws],
        out_specs=[pl.BlockSpec((tm, w.shape[1]), lambda i: (i, 0)) for w in ws],
        out_shape=[jax.ShapeDtypeStruct((t, w.shape[1]), F32) for w in ws],
        compiler_params=_params(("parallel",)),
        name="norm_proj",
    )(x, g.reshape(1, d), *ws)


def _proj_res_body(x_ref, a_ref, w_ref, o_ref):
    o_ref[...] = x_ref[...] + jnp.dot(a_ref[...].astype(BF16), w_ref[...],
                                      preferred_element_type=F32)


def _proj_residual(x, a, w, tm=512):
    t, d = x.shape
    k = a.shape[1]
    return pl.pallas_call(
        _proj_res_body,
        grid=(t // tm,),
        in_specs=[pl.BlockSpec((tm, d), lambda i: (i, 0)),
                  pl.BlockSpec((tm, k), lambda i: (i, 0)),
                  pl.BlockSpec((k, d), lambda i: (0, 0))],
        out_specs=pl.BlockSpec((tm, d), lambda i: (i, 0)),
        out_shape=jax.ShapeDtypeStruct((t, d), F32),
        compiler_params=_params(("parallel",)),
        name="proj_residual",
    )(x, a, w)


def _head_prep_body(y_ref, g_ref, *refs, nh, norm, scale, has_extra):
    o_ref = refs[-1]
    for h in range(nh):
        yh = y_ref[0, :, h * HEAD_DIM:(h + 1) * HEAD_DIM]
        if norm:
            yh = _rms(yh, g_ref[...])
        if scale != 1.0:
            yh = yh * scale
        yh = yh.astype(o_ref.dtype)
        if has_extra:
            yh = jnp.concatenate([yh, refs[0][...]], axis=-1)
        o_ref[0, h] = yh


def _head_prep(y, gain, scale, out_dtype, extra=None, ts=512):
    b, s, c = y.shape
    nh = c // HEAD_DIM
    norm = gain is not None
    g = (gain if norm else jnp.ones((HEAD_DIM,), F32)).reshape(1, HEAD_DIM)
    width = HEAD_DIM if extra is None else 2 * HEAD_DIM
    in_specs = [pl.BlockSpec((1, ts, c), lambda bi, i: (bi, i, 0)),
                pl.BlockSpec((1, HEAD_DIM), lambda bi, i: (0, 0))]
    args = [y, g]
    if extra is not None:
        in_specs.append(pl.BlockSpec((ts, HEAD_DIM), lambda bi, i: (0, 0)))
        args.append(extra)
    return pl.pallas_call(
        functools.partial(_head_prep_body, nh=nh, norm=norm, scale=scale,
                          has_extra=extra is not None),
        grid=(b, s // ts),
        in_specs=in_specs,
        out_specs=pl.BlockSpec((1, nh, ts, width), lambda bi, i: (bi, 0, i, 0)),
        out_shape=jax.ShapeDtypeStruct((b, nh, s, width), out_dtype),
        compiler_params=_params(("parallel", "parallel")),
        name="head_prep",
    )(*args)


def _head_proj_body(x_ref, g_ref, w_ref, *refs, plan):
    y_ref = refs[-1]
    y_ref[0] = jnp.dot(_rms(x_ref[0], g_ref[...]).astype(BF16), w_ref[...], preferred_element_type=F32)
    n_in = sum(1 + has_extra for nh, _, _, _, has_extra in plan if nh is not None)
    ins, outs = refs[:n_in], refs[n_in:-1]
    pos = 0
    for (nh, col, norm, scale, has_extra), o_ref in zip(plan, outs):
        if nh is None:
            o_ref[0] = y_ref[0, :, col:col + LANES]
            continue
        view = y_ref.at[:, :, col:col + nh * HEAD_DIM]
        extra = (ins[pos + 1],) if has_extra else ()
        _head_prep_body(view, ins[pos], *extra, o_ref, nh=nh, norm=norm, scale=scale,
                        has_extra=has_extra)
        pos += 1 + has_extra


def _head_proj(x, g, w, outputs, ts=512):
    b, s, d = x.shape
    n = w.shape[1]
    plan, in_specs, args, out_specs, out_shape = [], [], [], [], []
    for nh, col, gain, scale, dtype, extra in outputs:
        plan.append((nh, col, gain is not None, scale, extra is not None))
        if nh is None:
            out_specs.append(pl.BlockSpec((1, ts, LANES), lambda bi, i: (bi, i, 0)))
            out_shape.append(jax.ShapeDtypeStruct((b, s, LANES), F32))
            continue
        in_specs.append(pl.BlockSpec((1, HEAD_DIM), lambda bi, i: (0, 0)))
        args.append((gain if gain is not None else jnp.ones((HEAD_DIM,), F32)).reshape(1, HEAD_DIM))
        if extra is not None:
            in_specs.append(pl.BlockSpec((ts, HEAD_DIM), lambda bi, i: (0, 0)))
            args.append(extra)
        width = HEAD_DIM if extra is None else 2 * HEAD_DIM
        out_specs.append(pl.BlockSpec((1, nh, ts, width), lambda bi, i: (bi, 0, i, 0)))
        out_shape.append(jax.ShapeDtypeStruct((b, nh, s, width), dtype))
    return pl.pallas_call(
        functools.partial(_head_proj_body, plan=tuple(plan)),
        grid=(b, s // ts),
        in_specs=[pl.BlockSpec((1, ts, d), lambda bi, i: (bi, i, 0)),
                  pl.BlockSpec((1, d), lambda bi, i: (0, 0)),
                  pl.BlockSpec((d, n), lambda bi, i: (0, 0))] + in_specs,
        out_specs=out_specs,
        out_shape=out_shape,
        scratch_shapes=[pltpu.VMEM((1, ts, n), F32)],
        compiler_params=_params(("parallel", "parallel")),
        name="head_proj",
    )(x, g.reshape(1, d), w, *args)


def _slope_rows(n_heads, n_kv, rows_per_head, mult=LOG2E):
    g = n_heads // n_kv
    sl = np.asarray(_alibi_slopes(n_heads), np.float32).reshape(n_kv, g, 1) * np.float32(mult)
    return jnp.asarray(np.repeat(sl, rows_per_head, axis=1).reshape(n_kv, g * rows_per_head, 1))


BAND_SUB = 128


def _banded_body(*refs, g, tq, window, has_sink):
    q_ref, qa_ref, kp_ref, kc_ref, vp_ref, vc_ref, mask_ref = refs[:7]
    o_ref = refs[-1]
    i = pl.program_id(2)
    sub = BAND_SUB
    rows = g * sub
    span = window + sub
    lane = lax.broadcasted_iota(jnp.int32, (1, LANES), 1)
    pos_lanes = (lane >= HEAD_DIM + AUG_POS_HI) & (lane < HEAD_DIM + AUG_POS_HI + 3)
    shifts = jnp.concatenate(
        [jnp.broadcast_to(jnp.where(pos_lanes, float(bk - window // sub), 0.0), (sub, LANES))
         for bk in range(span // sub)], axis=0).astype(BF16)
    k_win = jnp.concatenate([kp_ref[0, 0], kc_ref[0, 0]], axis=0)
    v_win = jnp.concatenate([vp_ref[0, 0], vc_ref[0, 0]], axis=0)
    col = lax.broadcasted_iota(jnp.int32, (1, span), 1)
    before_start = jnp.where(i > 0, 0.0, NEG)
    if has_sink:
        sl_ref, sink_ref = refs[7], refs[8]
        r = (lax.broadcasted_iota(jnp.int32, (rows, LANES), 0) % sub).astype(F32)
        sink = sink_ref[0] * LOG2E + sl_ref[0] * r
    units = range(tq // sub)
    q_aug = [jnp.concatenate([q_ref[0, 0, :, u * sub:(u + 1) * sub, :].reshape(rows, HEAD_DIM),
                              qa_ref[0]], axis=-1) for u in units]
    bias = [jnp.concatenate([mask_ref[...] + jnp.where(col < window - u * sub, before_start, 0.0)] * g,
                            axis=0) for u in units]
    s = [_nt(q_aug[u], k_win[u * sub:u * sub + span] + shifts) + bias[u] for u in units]
    m = [jnp.broadcast_to(jnp.max(su, axis=-1, keepdims=True), (rows, LANES)) for su in s]
    if has_sink:
        m = [jnp.maximum(mu, sink) for mu in m]
    e = [jnp.exp2(su - jnp.concatenate([mu] * (span // LANES), axis=1)).astype(BF16)
         for su, mu in zip(s, m)]
    acc = [jnp.dot(e[u], v_win[u * sub:u * sub + span], preferred_element_type=F32) for u in units]
    for u in units:
        if has_sink:
            acc[u] = acc[u] + jnp.where(lane >= HEAD_DIM, jnp.exp2(sink - m[u]), 0.0)
        o = (acc[u] / pltpu.roll(acc[u], HEAD_DIM, 1))[:, :HEAD_DIM]
        o_ref[0, u * sub:(u + 1) * sub, :] = jnp.concatenate(
            [o[gi * sub:(gi + 1) * sub] for gi in range(g)], axis=-1)


def _key_position_lanes(period, rows):
    assert rows % period == 0
    c = np.arange(rows) % period
    cols = np.zeros((rows, HEAD_DIM), np.float32)
    cols[:, AUG_POS_HI:AUG_POS_HI + 3] = (c // 256 * 2)[:, None]
    cols[:, AUG_POS_LO:AUG_POS_LO + 3] = (c % 256)[:, None]
    return jnp.asarray(cols, BF16)


def _ones_lanes(rows):
    return jnp.ones((rows, HEAD_DIM), BF16)


def _banded_attention(q, k_aug, v_aug, n_heads, window, tq, sinks=None):
    b, hq, s, _ = q.shape
    hkv = k_aug.shape[1]
    g = hq // hkv
    sub = BAND_SUB
    assert tq % window == 0 and window % sub == 0
    span = window + sub
    dist = np.arange(sub)[:, None] + window - np.arange(span)[None, :]
    mask = np.where((dist >= 0) & (dist < window), 0.0, NEG).astype(np.float32)
    per = tq // window
    prev_spec = pl.BlockSpec((1, 1, window, LANES), lambda bi, h, i: (bi, h, jnp.maximum(i * per - 1, 0), 0))
    cur_spec = pl.BlockSpec((1, 1, tq, LANES), lambda bi, h, i: (bi, h, i, 0))
    in_specs = [pl.BlockSpec((1, 1, g, tq, HEAD_DIM), lambda bi, h, i: (bi, h, 0, i, 0)),
                pl.BlockSpec((1, g * sub, HEAD_DIM), lambda bi, h, i: (h, 0, 0)),
                prev_spec, cur_spec, prev_spec, cur_spec,
                pl.BlockSpec((sub, span), lambda bi, h, i: (0, 0))]
    args = [q.reshape(b, hkv, g, s, HEAD_DIM), _sel_query_aug(n_heads, hkv, sub),
            k_aug, k_aug, v_aug, v_aug, jnp.asarray(mask)]
    if sinks is not None:
        in_specs += [pl.BlockSpec((1, g * sub, 1), lambda bi, h, i: (h, 0, 0))] * 2
        args += [_slope_rows(n_heads, hkv, sub),
                 jnp.repeat(sinks.astype(F32).reshape(hkv, g, 1), sub, axis=1).reshape(hkv, g * sub, 1)]
    return pl.pallas_call(
        functools.partial(_banded_body, g=g, tq=tq, window=window, has_sink=sinks is not None),
        grid=(b, hkv, s // tq),
        in_specs=in_specs,
        out_specs=pl.BlockSpec((1, tq, g * HEAD_DIM), lambda bi, h, i: (bi, i, h)),
        out_shape=jax.ShapeDtypeStruct((b, s, hq * HEAD_DIM), F32),
        compiler_params=_params(("parallel", "parallel", "parallel")),
        name="banded_attention",
    )(*args)


def _compress_body(t_ref, pt_ref, pb_ref, w1t_ref, w1b_ref, w2_ref, g_ref, *refs, norm):
    o_ref = refs[-1]
    sub = t_ref[0, 0]
    n_sub = sub.shape[0]
    a = jnp.dot((sub + pt_ref[...]).astype(BF16), w1t_ref[...], preferred_element_type=F32)
    bm = jnp.dot((sub + pb_ref[...]).astype(BF16), w1b_ref[...], preferred_element_type=F32)
    hid = a + pltpu.roll(bm, shift=n_sub - 1, axis=0)
    hid = 0.5 * hid * (1.0 + jnp.tanh(0.7978845608028654 * (hid + 0.044715 * hid * hid * hid)))
    out = jnp.dot(hid.astype(BF16), w2_ref[...], preferred_element_type=F32)
    if norm:
        out = _rms(out, g_ref[...])
    out = out.astype(BF16)
    if len(refs) == 2:
        out = jnp.concatenate([out, refs[0][...]], axis=-1)
    o_ref[0, 0] = out


def _compress(t, pos, w1, w2, gain, extra=None):
    b, h, s, _ = t.shape
    n_sub = s // CMP_STRIDE
    half = CMP_STRIDE * HEAD_DIM
    hidden = w1.shape[1]
    norm = gain is not None
    g = (gain if norm else jnp.ones((HEAD_DIM,), F32)).reshape(1, HEAD_DIM)
    const = lambda shape: pl.BlockSpec(shape, lambda bi, hi: (0,) * len(shape))
    width = HEAD_DIM if extra is None else 2 * HEAD_DIM
    extras = [] if extra is None else [extra]
    return pl.pallas_call(
        functools.partial(_compress_body, norm=norm),
        grid=(b, h),
        in_specs=[pl.BlockSpec((1, 1, n_sub, half), lambda bi, hi: (bi, hi, 0, 0)),
                  const((1, half)), const((1, half)),
                  const((half, hidden)), const((half, hidden)),
                  const((hidden, HEAD_DIM)), const((1, HEAD_DIM))]
                 + [const((n_sub, HEAD_DIM)) for _ in extras],
        out_specs=pl.BlockSpec((1, 1, n_sub, width), lambda bi, hi: (bi, hi, 0, 0)),
        out_shape=jax.ShapeDtypeStruct((b, h, n_sub, width), BF16),
        compiler_params=_params(("parallel", "parallel")),
        name="nsa_compress",
    )(t.reshape(b, h, n_sub, half),
      pos[:CMP_STRIDE].reshape(1, half), pos[CMP_STRIDE:].reshape(1, half),
      w1[:half].astype(BF16), w1[half:].astype(BF16), w2.astype(BF16), g, *extras)


def _cmp_key_lanes(n_sub):
    n = np.arange(n_sub)
    cols = np.zeros((n_sub, HEAD_DIM), np.float32)
    cols[:, AUG_POS_HI:AUG_POS_HI + 3] = (n // 8)[:, None]
    cols[:, AUG_POS_LO:AUG_POS_LO + 3] = (CMP_STRIDE * (n % 8) + CMP_BLOCK - 1)[:, None]
    return jnp.asarray(cols, BF16)


def _cmp_attn_body(q_ref, qa_ref, kc_ref, vc_ref, ov_ref, o_ref, imp_ref, *, g, tq, widths):
    i = pl.program_id(2)
    rows = g * tq
    q_aug = jnp.concatenate([q_ref[0, 0].reshape(rows, HEAD_DIM), qa_ref[0]], axis=-1)
    lane = lax.broadcasted_iota(jnp.int32, (1, LANES), 1)
    pos_lanes = (lane >= HEAD_DIM + AUG_POS_HI) & (lane < HEAD_DIM + AUG_POS_HI + 3)
    shift = jnp.where(pos_lanes, (-i * (tq // LANES)).astype(F32), 0.0).astype(BF16)
    t_row = i * tq + lax.broadcasted_iota(jnp.int32, (rows, 1), 0) % tq
    n_visible = (i * tq + tq) // CMP_STRIDE - 1

    def attend(n_cols):
        s = _nt(q_aug, kc_ref[0, 0, :n_cols, :] + shift)
        cmp_end = lax.broadcasted_iota(jnp.int32, (1, n_cols), 1) * CMP_STRIDE + (CMP_BLOCK - 1)
        s = jnp.where(t_row >= cmp_end, s, NEG)
        wide = lambda x: jnp.concatenate([jnp.broadcast_to(x, (rows, LANES))] * (n_cols // LANES), axis=1)
        e = jnp.exp2(s - wide(jnp.maximum(jnp.max(s, axis=-1, keepdims=True), M_INIT)))
        l = jnp.sum(e, axis=-1, keepdims=True)
        p = e * wide(1.0 / jnp.where(l > 0.0, l, 1.0))
        o = jnp.dot(p.astype(BF16), vc_ref[0, 0, :n_cols, :], preferred_element_type=F32)
        o_ref[0] = jnp.concatenate([o[gi * tq:(gi + 1) * tq] for gi in range(g)], axis=-1)
        psum = p[0:tq]
        for gi in range(1, g):
            psum = psum + p[gi * tq:(gi + 1) * tq]
        p_hi = psum.astype(BF16)
        r1 = psum - p_hi.astype(F32)
        p_mid = r1.astype(BF16)
        p_lo = (r1 - p_mid.astype(F32)).astype(BF16)
        ov = ov_ref[:n_cols, :]
        imp_ref[0, 0] = (jnp.dot(p_hi, ov, preferred_element_type=F32)
                         + jnp.dot(p_mid, ov, preferred_element_type=F32)
                         + jnp.dot(p_lo, ov, preferred_element_type=F32))

    for lo, hi in zip((0,) + widths[:-1], widths):
        pl.when((n_visible > lo) & (n_visible <= hi) if lo else n_visible <= hi)(
            functools.partial(attend, hi))


def _cmp_attention(q, kc_aug, vc, tq):
    b, hq, s, _ = q.shape
    hkv = kc_aug.shape[1]
    g = hq // hkv
    n_sub = kc_aug.shape[2]
    n_sel = s // SEL_BLOCK
    step = 2 * LANES
    widths = tuple(range(step, n_sub, step)) + (n_sub,) if n_sub >= 2 * step else (n_sub,)
    cs = np.arange(n_sub)[:, None] * CMP_STRIDE
    ss = np.arange(n_sel)[None, :] * SEL_BLOCK
    overlap = ((cs < ss + SEL_BLOCK) & (cs + CMP_BLOCK > ss)).astype(np.float32)
    return pl.pallas_call(
        functools.partial(_cmp_attn_body, g=g, tq=tq, widths=widths),
        grid=(b, hkv, s // tq),
        in_specs=[pl.BlockSpec((1, 1, g, tq, HEAD_DIM), lambda bi, h, i: (bi, h, 0, i, 0)),
                  pl.BlockSpec((1, g * tq, HEAD_DIM), lambda bi, h, i: (h, 0, 0)),
                  pl.BlockSpec((1, 1, n_sub, LANES), lambda bi, h, i: (bi, h, 0, 0)),
                  pl.BlockSpec((1, 1, n_sub, HEAD_DIM), lambda bi, h, i: (bi, h, 0, 0)),
                  pl.BlockSpec((n_sub, n_sel), lambda bi, h, i: (0, 0))],
        out_specs=[pl.BlockSpec((1, tq, g * HEAD_DIM), lambda bi, h, i: (bi, i, h)),
                   pl.BlockSpec((1, 1, tq, n_sel), lambda bi, h, i: (bi, h, i, 0))],
        out_shape=[jax.ShapeDtypeStruct((b, s, hq * HEAD_DIM), F32),
                   jax.ShapeDtypeStruct((b, hkv, s, n_sel), F32)],
        compiler_params=_params(("parallel", "parallel", "parallel")),
        name="nsa_cmp_attention",
    )(q.reshape(b, hkv, g, s, HEAD_DIM), _sel_query_aug(hq, hkv, tq), kc_aug, vc,
      jnp.asarray(overlap, BF16))


def _topk_body(imp_ref, ch_ref, sel_ref, need_ref, *, tq, n_top, sub):
    i = pl.program_id(2)
    n_sel = imp_ref.shape[3]
    j = lax.broadcasted_iota(jnp.int32, (n_sel, 1), 0)
    cur = (i * tq + lax.broadcasted_iota(jnp.int32, (1, tq), 1)) // SEL_BLOCK
    forced = (j == 0) | (j == cur) | (j == cur - 1)
    causal = j <= cur
    imp = jnp.where(forced, FORCE, jnp.where(causal, imp_ref[0, 0].T, -1.0))
    for _ in range(n_top):
        mx = jnp.max(imp, axis=0, keepdims=True)
        first = jnp.min(jnp.where(imp == mx, j, n_sel), axis=0, keepdims=True)
        imp = jnp.where(j == first, -2.0, imp)
    sel = jnp.where(causal & (imp == -2.0), 1.0, 0.0).T
    sel_ref[0, 0] = sel.astype(sel_ref.dtype)
    for u in range(tq // sub):
        any_row = jnp.max(sel[u * sub:(u + 1) * sub], axis=0, keepdims=True)
        hits = jnp.dot(jnp.broadcast_to(any_row, (8, n_sel)).astype(BF16), ch_ref[...],
                       preferred_element_type=F32)
        need_ref[0, 0, u] = jnp.where(hits > 0.5, 1, 0).astype(jnp.int32)


def _topk_select(imp, sub, sel_tk, tq=512):
    b, hkv, s, n_sel = imp.shape
    tq = min(tq, s)
    n_top = min(SEL_TOPK, n_sel)
    assert s // sel_tk <= LANES
    chunk_of = (np.arange(n_sel)[:, None] * SEL_BLOCK // sel_tk == np.arange(LANES)[None, :])
    return pl.pallas_call(
        functools.partial(_topk_body, tq=tq, n_top=n_top, sub=sub),
        grid=(b, hkv, s // tq),
        in_specs=[pl.BlockSpec((1, 1, tq, n_sel), lambda bi, h, i: (bi, h, i, 0)),
                  pl.BlockSpec((n_sel, LANES), lambda bi, h, i: (0, 0))],
        out_specs=[pl.BlockSpec((1, 1, tq, n_sel), lambda bi, h, i: (bi, h, i, 0)),
                   pl.BlockSpec((1, 1, tq // sub, 8, LANES), lambda bi, h, i: (bi, h, i, 0, 0))],
        out_shape=[jax.ShapeDtypeStruct((b, hkv, s, n_sel), BF16),
                   jax.ShapeDtypeStruct((b, hkv, s // sub, 8, LANES), jnp.int32)],
        compiler_params=_params(("parallel", "parallel", "parallel")),
        name="nsa_topk_select",
    )(imp, jnp.asarray(chunk_of.astype(np.float32), BF16))


AUG_POS_HI, AUG_POS_LO, AUG_SEL = 0, 3, 6
MASK_BIG = 2.0 ** 100


def _sel_query_aug(n_heads, n_kv, tq):
    g = n_heads // n_kv
    sl = jnp.asarray(np.asarray(_alibi_slopes(n_heads), np.float32) * np.float32(LOG2E))
    hi = sl.astype(BF16).astype(F32)
    mid = (sl - hi).astype(BF16).astype(F32)
    lo = (sl - hi - mid).astype(BF16).astype(F32)
    pieces = jnp.stack([hi, mid, lo], axis=-1)
    cols = jnp.zeros((n_heads, HEAD_DIM), F32)
    cols = cols.at[:, AUG_POS_HI:AUG_POS_HI + 3].set(pieces * LANES)
    cols = cols.at[:, AUG_POS_LO:AUG_POS_LO + 3].set(pieces)
    cols = jnp.broadcast_to(cols.reshape(n_kv, g, 1, HEAD_DIM), (n_kv, g, tq, HEAD_DIM))
    return cols.reshape(n_kv, g * tq, HEAD_DIM).astype(BF16)


def _sel_key_aug(tk, rows):
    assert rows % tk == 0
    c = np.arange(rows) % tk
    cols = np.zeros((rows, HEAD_DIM), np.float32)
    cols[:, AUG_POS_HI:AUG_POS_HI + 3] = (c // 256 * 2)[:, None]
    cols[:, AUG_POS_LO:AUG_POS_LO + 3] = (c % 256)[:, None]
    cols[np.arange(rows), AUG_SEL + c // SEL_BLOCK] = MASK_BIG
    return jnp.asarray(cols, BF16)


def _sel_attn_body(need_ref, q_ref, qa_ref, k_ref, v_ref, sel_ref, o_ref,
                   qaug_ref, m_ref, acc_ref, *, g, tq, tk, n_chunks_total):
    bi, h, i = pl.program_id(0), pl.program_id(1), pl.program_id(2)
    n_sel = sel_ref.shape[3]
    per_chunk = tk // SEL_BLOCK
    rows = g * tq
    qaug_ref[...] = jnp.concatenate([q_ref[0, 0].reshape(rows, HEAD_DIM), qa_ref[0]], axis=-1)
    m_ref[...] = jnp.full(m_ref.shape, M_INIT, F32)
    acc_ref[...] = jnp.zeros(acc_ref.shape, F32)
    sel = sel_ref[0, 0]
    lane = lax.broadcasted_iota(jnp.int32, (1, LANES), 1)
    pos_lanes = (lane >= HEAD_DIM + AUG_POS_HI) & (lane < HEAD_DIM + AUG_POS_HI + 3)
    sel_lanes = (lane >= HEAD_DIM + AUG_SEL) & (lane < HEAD_DIM + AUG_SEL + per_chunk)
    jrow = lax.broadcasted_iota(jnp.int32, (n_sel, 1), 0)
    r = lax.broadcasted_iota(jnp.int32, (rows, 1), 0) % tq
    c = lax.broadcasted_iota(jnp.int32, (1, tk), 1)

    n_parts = 1

    def chunk(ci, diagonal):
        start = pl.multiple_of(ci * tk, tk)
        shift = (ci * (tk // LANES) - i * (tq // LANES)).astype(F32)
        k = k_ref[0, 0, pl.ds(start, tk), :] + jnp.where(pos_lanes, shift, 0.0).astype(BF16)
        v = v_ref[0, 0, pl.ds(start, tk), :]
        place = jnp.where(sel_lanes & (jrow == ci * per_chunk + lane - (HEAD_DIM + AUG_SEL)), 1.0, 0.0)
        picked = jnp.dot(sel, place.astype(BF16), preferred_element_type=F32)
        unpicked = jnp.where(sel_lanes, picked - 1.0, 0.0).astype(BF16)
        for part in range(n_parts):
            rs = slice(part * rows // n_parts, (part + 1) * rows // n_parts)
            s = _nt(qaug_ref[rs, :] + jnp.concatenate([unpicked] * (g // n_parts), axis=0), k)
            if diagonal:
                s = jnp.where(r[rs] + (i * tq - ci * tk) >= c, s, NEG)
            m_old = m_ref[rs, :]
            m_new = jnp.maximum(m_old, jnp.max(s, axis=-1, keepdims=True))
            e = jnp.exp2(s - jnp.concatenate([m_new] * (tk // LANES), axis=1))
            acc_ref[rs, :] = (jnp.exp2(m_old - m_new) * acc_ref[rs, :]
                              + jnp.dot(e.astype(BF16), v, preferred_element_type=F32))
            m_ref[rs, :] = m_new

    diag = (i * tq) // tk
    base = ((bi * pl.num_programs(1) + h) * pl.num_programs(2) + i) * n_chunks_total

    def maybe(ci, carry):
        @pl.when(need_ref[base + ci] != 0)
        def _():
            chunk(ci, False)
        return carry

    lax.fori_loop(0, diag, maybe, 0)
    chunk(diag, True)
    acc = acc_ref[...]
    o = (acc / pltpu.roll(acc, HEAD_DIM, 1))[:, :HEAD_DIM]
    o_ref[0] = jnp.concatenate([o[gi * tq:(gi + 1) * tq] for gi in range(g)], axis=-1)


def _sel_attention(q, k_aug, v, sel, need, tq, tk):
    b, hq, s, _ = q.shape
    hkv = k_aug.shape[1]
    g = hq // hkv
    n_sel = sel.shape[3]
    grid_spec = pltpu.PrefetchScalarGridSpec(
        num_scalar_prefetch=1,
        grid=(b, hkv, s // tq),
        in_specs=[pl.BlockSpec((1, 1, g, tq, HEAD_DIM), lambda bi, h, i, nd: (bi, h, 0, i, 0)),
                  pl.BlockSpec((1, g * tq, HEAD_DIM), lambda bi, h, i, nd: (h, 0, 0)),
                  pl.BlockSpec((1, 1, s, LANES), lambda bi, h, i, nd: (bi, h, 0, 0)),
                  pl.BlockSpec((1, 1, s, LANES), lambda bi, h, i, nd: (bi, h, 0, 0)),
                  pl.BlockSpec((1, 1, tq, n_sel), lambda bi, h, i, nd: (bi, h, i, 0))],
        out_specs=pl.BlockSpec((1, tq, g * HEAD_DIM), lambda bi, h, i, nd: (bi, i, h)),
        scratch_shapes=[pltpu.VMEM((g * tq, LANES), BF16),
                        pltpu.VMEM((g * tq, LANES), F32), pltpu.VMEM((g * tq, LANES), F32)])
    return pl.pallas_call(
        functools.partial(_sel_attn_body, g=g, tq=tq, tk=tk, n_chunks_total=s // tk),
        grid_spec=grid_spec,
        out_shape=jax.ShapeDtypeStruct((b, s, hq * HEAD_DIM), F32),
        compiler_params=_params(("parallel", "parallel", "arbitrary")),
        name="nsa_selected_attention",
    )(need, q.reshape(b, hkv, g, s, HEAD_DIM), _sel_query_aug(hq, hkv, tq), k_aug, v, sel)


def _nsa_out_body(x_ref, gt_ref, oc_ref, os_ref, ow_ref, ex_ref, w_ref, o_ref):
    sig = _sigmoid(gt_ref[...])
    sig_hi = sig.astype(BF16)
    sig_lo = (sig - sig_hi.astype(F32)).astype(BF16)
    mix = jnp.zeros(oc_ref.shape, F32)
    for j, br_ref in enumerate((oc_ref, os_ref, ow_ref)):
        gate = (jnp.dot(sig_hi, ex_ref[j], preferred_element_type=F32)
                + jnp.dot(sig_lo, ex_ref[j], preferred_element_type=F32))
        mix = mix + gate * br_ref[...]
    o_ref[...] = x_ref[...] + jnp.dot(mix.astype(BF16), w_ref[...], preferred_element_type=F32)


def _nsa_out(x, gates, o_cmp, o_sel, o_win, w_out, tm=256):
    t, d = x.shape
    c = o_cmp.shape[1]
    expand = np.zeros((3, LANES, c), np.float32)
    for head in range(NSA_HQ):
        for j in range(3):
            expand[j, head * 3 + j, head * HEAD_DIM:(head + 1) * HEAD_DIM] = 1.0
    row = lambda width: pl.BlockSpec((tm, width), lambda i: (i, 0))
    return pl.pallas_call(
        _nsa_out_body,
        grid=(t // tm,),
        in_specs=[row(d), row(LANES), row(c), row(c), row(c),
                  pl.BlockSpec((3, LANES, c), lambda i: (0, 0, 0)),
                  pl.BlockSpec((c, d), lambda i: (0, 0))],
        out_specs=row(d),
        out_shape=jax.ShapeDtypeStruct((t, d), F32),
        compiler_params=_params(("parallel",)),
        name="nsa_out",
    )(x, gates, o_cmp, o_sel, o_win, jnp.asarray(expand, BF16), w_out)


def _nsa_layer(x, b, s, mix_norm, w_in, w_out, q_norm, k_norm, cmp_pos, cmp_w1, cmp_w2):
    dh = HEAD_DIM
    kvw = NSA_HKV * dh
    edges = np.cumsum([0, NSA_HQ * dh] + [kvw] * 6).tolist()
    n_gate = 3 * NSA_HQ
    w_all = jnp.pad(w_in, ((0, 0), (0, LANES - n_gate))).astype(BF16)
    sel_tq, sel_tk, prep_ts = 256, min(512, s), min(512, s)
    win_tq = min(512, s)
    ones = _ones_lanes(prep_ts)
    qh, kc, vc, ksh, vsh, kwh, vwh, gates = _head_proj(x.reshape(b, s, -1), mix_norm, w_all, [
        (NSA_HQ, edges[0], q_norm, dh ** -0.5 * LOG2E, BF16, None),
        (NSA_HKV, edges[1], None, 1.0, F32, None),
        (NSA_HKV, edges[2], None, 1.0, F32, None),
        (NSA_HKV, edges[3], k_norm[1], 1.0, BF16, _sel_key_aug(sel_tk, prep_ts)),
        (NSA_HKV, edges[4], None, 1.0, BF16, ones),
        (NSA_HKV, edges[5], k_norm[2], 1.0, BF16, _key_position_lanes(BAND_SUB, prep_ts)),
        (NSA_HKV, edges[6], None, 1.0, BF16, ones),
        (None, edges[7], None, 1.0, F32, None)], ts=prep_ts)
    gates = gates.reshape(b * s, LANES)
    kch = _compress(kc, cmp_pos[0], cmp_w1[0], cmp_w2[0], k_norm[0], extra=_cmp_key_lanes(s // CMP_STRIDE))
    vch = _compress(vc, cmp_pos[1], cmp_w1[1], cmp_w2[1], None)
    o_cmp, imp = _cmp_attention(qh, kch, vch, tq=256)
    sel, need = _topk_select(imp, sel_tq, sel_tk)
    need = need[:, :, :, 0, :s // sel_tk].reshape(-1)
    o_sel = _sel_attention(qh, ksh, vsh, sel, need, sel_tq, sel_tk)
    o_win = _banded_attention(qh, kwh, vwh, NSA_HQ, NSA_WINDOW, tq=win_tq)
    flat = lambda y: y.reshape(b * s, y.shape[-1])
    return _nsa_out(x, gates, flat(o_cmp), flat(o_sel), flat(o_win), w_out.astype(BF16))


def _diff_prep_body(y_ref, gq_ref, gk_ref, q_ref, k_ref, v_ref, *, width, q_scale, tk):
    tm = y_ref.shape[0]
    lane = lax.broadcasted_iota(jnp.int32, (1, LANES), 1)
    low = lane < HEAD_DIM
    c = (pl.program_id(0) * tm + lax.broadcasted_iota(jnp.int32, (tm, 1), 0)) % tk
    key_lanes = jnp.where((lane >= AUG_POS_HI) & (lane < AUG_POS_HI + 3), (c // 256) * 2,
                          jnp.where((lane >= AUG_POS_LO) & (lane < AUG_POS_LO + 3), c % 256, 0)
                          ).astype(F32).astype(BF16)

    def pair_norm(ysl, gain):
        sq = ysl * ysl
        ss_lo = jnp.sum(jnp.where(low, sq, 0.0), axis=-1, keepdims=True)
        ss_hi = jnp.sum(jnp.where(low, 0.0, sq), axis=-1, keepdims=True)
        inv = jnp.where(low, lax.rsqrt(ss_lo / HEAD_DIM + EPS), lax.rsqrt(ss_hi / HEAD_DIM + EPS))
        return ysl * inv * gain

    for j in range(width // LANES):
        sl = slice(j * LANES, (j + 1) * LANES)
        q_ref[:, sl] = (pair_norm(y_ref[:, sl], gq_ref[...]) * q_scale).astype(BF16)
        k_ref[:, 2 * j * LANES:(2 * j + 1) * LANES] = pair_norm(
            y_ref[:, width + j * LANES:width + (j + 1) * LANES], gk_ref[...]).astype(BF16)
        k_ref[:, (2 * j + 1) * LANES:(2 * j + 2) * LANES] = key_lanes
        v_ref[:, sl] = y_ref[:, 2 * width + j * LANES:2 * width + (j + 1) * LANES].astype(BF16)


def _diff_prep(qkv, q_norm, k_norm, tk, tm=512):
    t, w3 = qkv.shape
    width = w3 // 3
    pair = lambda gvec: jnp.concatenate([gvec, gvec]).reshape(1, LANES).astype(F32)
    out = jax.ShapeDtypeStruct((t, width), BF16)
    return pl.pallas_call(
        functools.partial(_diff_prep_body, width=width, q_scale=HEAD_DIM ** -0.5 * LOG2E, tk=tk),
        grid=(t // tm,),
        in_specs=[pl.BlockSpec((tm, w3), lambda i: (i, 0)),
                  pl.BlockSpec((1, LANES), lambda i: (0, 0)),
                  pl.BlockSpec((1, LANES), lambda i: (0, 0))],
        out_specs=[pl.BlockSpec((tm, width), lambda i: (i, 0)),
                   pl.BlockSpec((tm, 2 * width), lambda i: (i, 0)),
                   pl.BlockSpec((tm, width), lambda i: (i, 0))],
        out_shape=[out, jax.ShapeDtypeStruct((t, 2 * width), BF16), out],
        compiler_params=_params(("parallel",)),
        name="diff_prep",
    )(qkv, pair(q_norm), pair(k_norm))


def _diff_proj_body(x_ref, g_ref, w_ref, gq_ref, gk_ref, q_ref, k_ref, v_ref, y_ref,
                    *, width, q_scale, tk):
    y_ref[...] = jnp.dot(_rms(x_ref[...], g_ref[...]).astype(BF16), w_ref[...],
                         preferred_element_type=F32)
    _diff_prep_body(y_ref, gq_ref, gk_ref, q_ref, k_ref, v_ref, width=width, q_scale=q_scale, tk=tk)


def _diff_proj(x, g, w, q_norm, k_norm, tk, tm=512):
    t, d = x.shape
    w3 = w.shape[1]
    width = w3 // 3
    pair = lambda gvec: jnp.concatenate([gvec, gvec]).reshape(1, LANES).astype(F32)
    out = jax.ShapeDtypeStruct((t, width), BF16)
    return pl.pallas_call(
        functools.partial(_diff_proj_body, width=width, q_scale=HEAD_DIM ** -0.5 * LOG2E, tk=tk),
        grid=(t // tm,),
        in_specs=[pl.BlockSpec((tm, d), lambda i: (i, 0)),
                  pl.BlockSpec((1, d), lambda i: (0, 0)),
                  pl.BlockSpec((d, w3), lambda i: (0, 0)),
                  pl.BlockSpec((1, LANES), lambda i: (0, 0)),
                  pl.BlockSpec((1, LANES), lambda i: (0, 0))],
        out_specs=[pl.BlockSpec((tm, width), lambda i: (i, 0)),
                   pl.BlockSpec((tm, 2 * width), lambda i: (i, 0)),
                   pl.BlockSpec((tm, width), lambda i: (i, 0))],
        out_shape=[out, jax.ShapeDtypeStruct((t, 2 * width), BF16), out],
        scratch_shapes=[pltpu.VMEM((tm, w3), F32)],
        compiler_params=_params(("parallel",)),
        name="diff_proj",
    )(x, g.reshape(1, d), w, pair(q_norm), pair(k_norm))


def _diff_flash_body(reach_ref, q_ref, qa_ref, k_ref, v_ref, lam_ref, sub_ref, o_ref,
                     *, tq, tk, lam_init):
    h, i = pl.program_id(1), pl.program_id(2)
    q = q_ref[0]
    lane = lax.broadcasted_iota(jnp.int32, (1, LANES), 1)
    zero = jnp.zeros_like(q)
    q_both = jnp.concatenate([jnp.where(lane < HEAD_DIM, q, zero),
                              jnp.where(lane < HEAD_DIM, zero, q)], axis=0)
    q_aug = jnp.concatenate([q_both, qa_ref[0]], axis=-1)
    lane2 = lax.broadcasted_iota(jnp.int32, (1, 2 * LANES), 1)
    pos_lanes = (lane2 >= LANES + AUG_POS_HI) & (lane2 < LANES + AUG_POS_HI + 3)
    r = lax.broadcasted_iota(jnp.int32, (2 * tq, 1), 0) % tq
    c = lax.broadcasted_iota(jnp.int32, (1, tk), 1)

    def step(ci, carry, diagonal):
        m, l, acc = carry
        start = pl.multiple_of(ci * tk, tk)
        shift = (ci * (tk // LANES) - i * (tq // LANES)).astype(F32)
        k = k_ref[0, pl.ds(start, tk), :] + jnp.where(pos_lanes, shift, 0.0).astype(BF16)
        v = v_ref[0, pl.ds(start, tk), :]
        s = _nt(q_aug, k)
        if diagonal:
            s = jnp.where(r + (i * tq - ci * tk) >= c, s, NEG)
        m_new = jnp.maximum(m, jnp.max(s, axis=-1, keepdims=True))
        alpha = jnp.exp2(m - m_new)
        e = jnp.exp2(s - jnp.concatenate([m_new] * (tk // LANES), axis=1))
        l = alpha * l + jnp.sum(e, axis=-1, keepdims=True)
        acc = alpha * acc + jnp.dot(e.astype(BF16), v, preferred_element_type=F32)
        return m_new, l, acc

    init = (jnp.full((2 * tq, LANES), M_INIT, F32), jnp.zeros((2 * tq, LANES), F32),
            jnp.zeros((2 * tq, LANES), F32))
    diag = (i * tq) // tk
    first = jnp.maximum(i * tq - reach_ref[h], 0) // tk
    carry = lax.fori_loop(first, diag, lambda ci, cr: step(ci, cr, False), init)
    _, l, acc = step(diag, carry, True)
    lam = lam_ref[...]
    lam01 = jnp.sum(lam[0:1] * lam[1:2], axis=-1, keepdims=True)
    lam23 = jnp.sum(lam[2:3] * lam[3:4], axis=-1, keepdims=True)
    lmbda = jnp.exp(lam01) - jnp.exp(lam23) + lam_init
    o = acc / l
    o = o[:tq] - lmbda * o[tq:]
    o_ref[0] = _rms(o, sub_ref[...]) * (1.0 - lam_init)


def _diff_flash(qn, k_aug, vb, reach, lam, subln, lam_init, tq, tk):
    b, s, w = qn.shape
    n_heads = w // LANES
    sl = jnp.asarray(np.asarray(_alibi_slopes(n_heads), np.float32) * np.float32(LOG2E))
    hi = sl.astype(BF16).astype(F32)
    mid = (sl - hi).astype(BF16).astype(F32)
    lo = (sl - hi - mid).astype(BF16).astype(F32)
    pieces = jnp.stack([hi, mid, lo], axis=-1)
    q_aug = jnp.zeros((n_heads, LANES), F32)
    q_aug = q_aug.at[:, AUG_POS_HI:AUG_POS_HI + 3].set(pieces * LANES)
    q_aug = q_aug.at[:, AUG_POS_LO:AUG_POS_LO + 3].set(pieces)
    q_aug = jnp.broadcast_to(q_aug[:, None, :], (n_heads, 2 * tq, LANES)).astype(BF16)
    grid_spec = pltpu.PrefetchScalarGridSpec(
        num_scalar_prefetch=1,
        grid=(b, n_heads, s // tq),
        in_specs=[pl.BlockSpec((1, tq, LANES), lambda bi, h, i, rc: (bi, i, h)),
                  pl.BlockSpec((1, 2 * tq, LANES), lambda bi, h, i, rc: (h, 0, 0)),
                  pl.BlockSpec((1, s, 2 * LANES), lambda bi, h, i, rc: (bi, 0, h)),
                  pl.BlockSpec((1, s, LANES), lambda bi, h, i, rc: (bi, 0, h)),
                  pl.BlockSpec(lam.shape, lambda bi, h, i, rc: (0, 0)),
                  pl.BlockSpec((1, LANES), lambda bi, h, i, rc: (0, 0))],
        out_specs=pl.BlockSpec((1, tq, LANES), lambda bi, h, i, rc: (bi, i, h)))
    return pl.pallas_call(
        functools.partial(_diff_flash_body, tq=tq, tk=tk, lam_init=lam_init),
        grid_spec=grid_spec,
        out_shape=jax.ShapeDtypeStruct((b, s, w), F32),
        compiler_params=_params(("parallel", "parallel", "arbitrary")),
        name="diff_flash",
    )(reach, qn, q_aug, k_aug, vb, lam.astype(F32), subln.reshape(1, LANES).astype(F32))


def _diff_reach(q_norm, k_norm, n_heads, s):
    bound = (HEAD_DIM * jnp.max(jnp.abs(q_norm)) * jnp.max(jnp.abs(k_norm))
             * (HEAD_DIM ** -0.5 * LOG2E) * 1.02)
    slopes = jnp.asarray(np.asarray(_alibi_slopes(n_heads), np.float32) * np.float32(LOG2E))
    reach = jnp.ceil((2.0 * bound + 152.0) / slopes) + 1.0
    return jnp.minimum(reach, float(s)).astype(jnp.int32)


def _diff_layer(x, b, s, mix_norm, w_in, w_out, q_norm, k_norm, lam, subln, lam_init):
    tq, tk = min(1024, s), min(1024, s)
    qn, k_aug, vb = _diff_proj(x, mix_norm, w_in.astype(BF16), q_norm, k_norm, tk)
    as3 = lambda y: y.reshape(b, s, y.shape[-1])
    o = _diff_flash(as3(qn), as3(k_aug), as3(vb), _diff_reach(q_norm, k_norm, DIFF_H, s),
                    lam, subln, lam_init, tq, tk)
    return _proj_residual(x, o.reshape(b * s, -1), w_out.astype(BF16))


def _gdn_prep_body(cur_ref, prev_ref, cw_ref, b_ref, a_ref, alog_ref, dt_ref,
                   q_ref, k_ref, v_ref, beta_ref, gc_ref, pad_ref, *, ts, width):
    i = pl.program_id(1)
    halo = prev_ref.shape[1]
    pad_ref[0:halo, :] = jnp.where(i > 0, prev_ref[0], 0.0)
    pad_ref[halo:, :] = cur_ref[0]
    y = jnp.zeros((ts, 3 * width), F32)
    for j in range(GDN_CONV):
        off = halo - (GDN_CONV - 1) + j
        y = y + cw_ref[j:j + 1, :] * pad_ref[off:off + ts, :]
    y = y * _sigmoid(y)
    for h in range(width // GDN_DK):
        sl = slice(h * GDN_DK, (h + 1) * GDN_DK)
        qh = y[:, sl]
        q_ref[0, :, sl] = qh * lax.rsqrt(jnp.sum(qh * qh, axis=-1, keepdims=True) + EPS) * (GDN_DK ** -0.5)
        kh = y[:, width + h * GDN_DK:width + (h + 1) * GDN_DK]
        k_ref[0, :, sl] = kh * lax.rsqrt(jnp.sum(kh * kh, axis=-1, keepdims=True) + EPS)
    v_ref[0] = y[:, 2 * width:]
    beta_ref[0] = _sigmoid(b_ref[0])
    z = a_ref[0] + dt_ref[...]
    softplus = jnp.maximum(z, 0.0) + jnp.log(1.0 + jnp.exp(-jnp.abs(z)))
    gl = -jnp.exp(alog_ref[...]) * softplus
    r = lax.broadcasted_iota(jnp.int32, (ts, ts), 0)
    c = lax.broadcasted_iota(jnp.int32, (ts, ts), 1)
    tri = jnp.where((r // GDN_CHUNK == c // GDN_CHUNK) & (c <= r), 1.0, 0.0)
    gc_ref[0] = jnp.dot(tri, gl, precision=HIGHEST, preferred_element_type=F32)


def _gdn_prep(qkv, b_pad, a_pad, conv_w, a_log, dt_bias, ts=256):
    b, s, w3 = qkv.shape
    width = w3 // 3
    halo = 8
    pad_lane = lambda vec: jnp.pad(vec.astype(F32), (0, LANES - vec.shape[0])).reshape(1, LANES)
    big = jax.ShapeDtypeStruct((b, s, width), F32)
    small = jax.ShapeDtypeStruct((b, s, LANES), F32)
    tok = lambda wd: pl.BlockSpec((1, ts, wd), lambda bi, i: (bi, i, 0))
    return pl.pallas_call(
        functools.partial(_gdn_prep_body, ts=ts, width=width),
        grid=(b, s // ts),
        in_specs=[tok(w3),
                  pl.BlockSpec((1, halo, w3), lambda bi, i: (bi, jnp.maximum(i * (ts // halo) - 1, 0), 0)),
                  pl.BlockSpec((GDN_CONV, w3), lambda bi, i: (0, 0)),
                  tok(LANES), tok(LANES),
                  pl.BlockSpec((1, LANES), lambda bi, i: (0, 0)),
                  pl.BlockSpec((1, LANES), lambda bi, i: (0, 0))],
        out_specs=[tok(width), tok(width), tok(width), tok(LANES), tok(LANES)],
        out_shape=[big, big, big, small, small],
        scratch_shapes=[pltpu.VMEM((ts + halo, w3), F32)],
        compiler_params=_params(("parallel", "parallel")),
        name="gdn_prep",
    )(qkv, qkv, conv_w.astype(F32), b_pad, a_pad, pad_lane(a_log), pad_lane(dt_bias))


def _unit_lower_inverses(mats):
    n = mats[0].shape[0]
    r = lax.broadcasted_iota(jnp.int32, (n, n), 0)
    c = lax.broadcasted_iota(jnp.int32, (n, n), 1)
    eye = jnp.where(r == c, 1.0, 0.0)
    d = 1
    inv = None
    while d < n:
        below = (r // (2 * d) == c // (2 * d)) & (r // d != c // d)
        offs = [jnp.where(below, a, 0.0) for a in mats]
        if d == 1:
            inv = [eye - off for off in offs]
        else:
            left = [_dot3(t, off) for t, off in zip(inv, offs)]
            inv = [t - _dot3(lt, t) for t, lt in zip(inv, left)]
        d *= 2
    return inv


def _gdn_chunk_body(q_ref, k_ref, v_ref, beta_ref, gc_ref, gr_ref, gate_ref, on_ref, o_ref,
                    state_ref, *, nc, n_heads):
    @pl.when(pl.program_id(1) == 0)
    def _():
        state_ref[...] = jnp.zeros(state_ref.shape, F32)

    cl = GDN_CHUNK
    r = lax.broadcasted_iota(jnp.int32, (cl, cl), 0)
    c = lax.broadcasted_iota(jnp.int32, (cl, cl), 1)
    fdot = functools.partial(jnp.dot, preferred_element_type=F32)
    blocks = [(ch, h) for ch in range(nc) for h in range(n_heads)]
    rows = lambda ch: slice(ch * cl, (ch + 1) * cl)
    lanes = lambda h: slice(h * GDN_DK, (h + 1) * GDN_DK)

    qs = [q_ref[0, rows(ch), lanes(h)] for ch, h in blocks]
    ks = [k_ref[0, rows(ch), lanes(h)] for ch, h in blocks]
    bcol = [beta_ref[0, rows(ch), h:h + 1] for ch, h in blocks]
    gcol = [gc_ref[0, rows(ch), h:h + 1] for ch, h in blocks]
    glast = [gc_ref[0, ch * cl + cl - 1:ch * cl + cl, h:h + 1] for ch, h in blocks]
    decay = [jnp.exp(jnp.minimum(g - gr_ref[0, ch, h:h + 1, :], 0.0))
             for g, (ch, h) in zip(gcol, blocks)]
    kb = [k * b for k, b in zip(ks, bcol)]
    amat = [jnp.where(r > c, _dot3(x, k, ((1,), (1,))) * dc, 0.0) for x, k, dc in zip(kb, ks, decay)]
    inv = _unit_lower_inverses(amat)
    eg = [jnp.exp(g) for g in gcol]
    u = [_dot3(t, v_ref[0, rows(ch), lanes(h)] * b) for t, b, (ch, h) in zip(inv, bcol, blocks)]
    w = [_dot3(t, x * e) for t, x, e in zip(inv, kb, eg)]
    attn = [jnp.where(r >= c, _nt(q, k) * dc, 0.0) for q, k, dc in zip(qs, ks, decay)]

    for ch in range(nc):
        ids = range(ch * n_heads, (ch + 1) * n_heads)
        state = [state_ref[h] for h in range(n_heads)]
        v_new = [u[i] - fdot(w[i], st) for i, st in zip(ids, state)]
        o = [fdot(qs[i] * eg[i], st) + fdot(attn[i], vn) for i, st, vn in zip(ids, state, v_new)]
        for h, (i, st, vn) in enumerate(zip(ids, state, v_new)):
            state_ref[h] = st * jnp.exp(glast[i]) + _tn(ks[i] * jnp.exp(glast[i] - gcol[i]), vn)
        for h in range(n_heads):
            gate = gate_ref[0, rows(ch), lanes(h)]
            o_ref[0, rows(ch), lanes(h)] = _rms(o[h], on_ref[...]) * (gate * _sigmoid(gate))


def _gdn_chunks(q, k, v, beta, gc, gc_rows, gate, o_norm, nc=1):
    b, s, w = q.shape
    n_heads = w // GDN_DK
    ts = nc * GDN_CHUNK
    tok = lambda wd: pl.BlockSpec((1, ts, wd), lambda bi, i: (bi, i, 0))
    return pl.pallas_call(
        functools.partial(_gdn_chunk_body, nc=nc, n_heads=n_heads),
        grid=(b, s // ts),
        in_specs=[tok(w), tok(w), tok(w), tok(LANES), tok(LANES),
                  pl.BlockSpec((1, nc, 8, GDN_CHUNK), lambda bi, i: (bi, i, 0, 0)),
                  tok(w),
                  pl.BlockSpec((1, GDN_DV), lambda bi, i: (0, 0))],
        out_specs=tok(w),
        out_shape=jax.ShapeDtypeStruct((b, s, w), F32),
        scratch_shapes=[pltpu.VMEM((n_heads, GDN_DK, GDN_DV), F32)],
        compiler_params=_params(("parallel", "arbitrary")),
        name="gdn_chunks",
    )(q, k, v, beta, gc, gc_rows, gate, o_norm.reshape(1, GDN_DV).astype(F32))


def _gdn_layer(x, b, s, mix_norm, w_in, w_out, conv_w, a_log, dt_bias, o_norm):
    qkv_w = GDN_H * (2 * GDN_DK + GDN_DV)
    lane_pad = lambda wcols: jnp.pad(wcols, ((0, 0), (0, LANES - wcols.shape[1]))).astype(BF16)
    ws = [w_in[:, :qkv_w].astype(BF16),
          lane_pad(w_in[:, qkv_w:qkv_w + GDN_H]),
          lane_pad(w_in[:, qkv_w + GDN_H:qkv_w + 2 * GDN_H]),
          w_in[:, qkv_w + 2 * GDN_H:].astype(BF16)]
    qkv, b_pad, a_pad, gate = _norm_proj(x, mix_norm, ws)
    as3 = lambda y: y.reshape(b, s, y.shape[-1])
    q, k, v, beta, gc = _gdn_prep(as3(qkv), as3(b_pad), as3(a_pad), conv_w, a_log, dt_bias,
                                  ts=min(256, s))
    gc_rows = gc[:, :, :8].reshape(b, s // GDN_CHUNK, GDN_CHUNK, 8).transpose(0, 1, 3, 2)
    o = _gdn_chunks(q, k, v, beta, gc, gc_rows, as3(gate), o_norm, nc=2)
    return _proj_residual(x, o.reshape(b * s, -1), w_out.astype(BF16))


def _swa_layer(x, b, s, mix_norm, w_in, w_out, q_norm, k_norm, sinks):
    dh = HEAD_DIM
    e0, e1 = SWA_HQ * dh, (SWA_HQ + SWA_HKV) * dh
    tq, prep_ts = min(512, s), min(512, s)
    qh, kh, vh = _head_proj(x.reshape(b, s, -1), mix_norm, w_in.astype(BF16), [
        (SWA_HQ, 0, q_norm, dh ** -0.5 * LOG2E, BF16, None),
        (SWA_HKV, e0, k_norm, 1.0, BF16, _key_position_lanes(BAND_SUB, prep_ts)),
        (SWA_HKV, e1, None, 1.0, BF16, _ones_lanes(prep_ts))], ts=prep_ts)
    o = _banded_attention(qh, kh, vh, SWA_HQ, SWA_WINDOW, tq=tq, sinks=sinks)
    return _proj_residual(x, o.reshape(b * s, -1), w_out.astype(BF16))


def kernel(x, ffn1_norm, ffn1_w_in, ffn1_w_out, mix_norm, ffn2_norm, ffn2_w_in, ffn2_w_out,
           nsa_w_in, nsa_w_out, nsa_q_norm, nsa_k_norm, nsa_cmp_pos, nsa_cmp_w1, nsa_cmp_w2,
           diff_w_in, diff_w_out, diff_q_norm, diff_k_norm, diff_lambda, diff_subln,
           gdn_w_in, gdn_w_out, gdn_conv_w, gdn_a_log, gdn_dt_bias, gdn_o_norm,
           swa_w_in, swa_w_out, swa_q_norm, swa_k_norm, swa_sinks):
    b, s, d = x.shape
    depth = ffn1_norm.shape[0]
    n_mixers = 4
    h = x.reshape(b * s, d)
    for layer in range(depth):
        kind, j = layer % n_mixers, layer // n_mixers
        h = _ffn(h, ffn1_norm[layer], ffn1_w_in[layer].astype(BF16), ffn1_w_out[layer].astype(BF16))
        if kind == 0:
            h = _nsa_layer(h, b, s, mix_norm[layer], nsa_w_in[j], nsa_w_out[j], nsa_q_norm[j],
                           nsa_k_norm[j], nsa_cmp_pos[j], nsa_cmp_w1[j], nsa_cmp_w2[j])
        elif kind == 1:
            lam_init = 0.8 - 0.6 * math.exp(-0.3 * layer)
            h = _diff_layer(h, b, s, mix_norm[layer], diff_w_in[j], diff_w_out[j], diff_q_norm[j],
                            diff_k_norm[j], diff_lambda[j], diff_subln[j], lam_init)
        elif kind == 2:
            h = _gdn_layer(h, b, s, mix_norm[layer], gdn_w_in[j], gdn_w_out[j], gdn_conv_w[j],
                           gdn_a_log[j], gdn_dt_bias[j], gdn_o_norm[j])
        else:
            h = _swa_layer(h, b, s, mix_norm[layer], swa_w_in[j], swa_w_out[j], swa_q_norm[j],
                           swa_k_norm[j], swa_sinks[j])
        h = _ffn(h, ffn2_norm[layer], ffn2_w_in[layer].astype(BF16), ffn2_w_out[layer].astype(BF16))
    return h.reshape(b, s, d)
```
